```python
import jax, jax.numpy as jnp
from jax import lax
import numpy as np

D_MODEL = 1024
BATCH = 8
SEQ = 2048
DEPTH = 4

N_MIXERS = 2
N_NSA_LAYERS = (DEPTH + 1) // 2
N_RG_LAYERS = DEPTH // 2
EPS = 1e-6

NSA_HEADS = 16
NSA_GROUPS = 4
NSA_HPG = NSA_HEADS // NSA_GROUPS
HEAD_DIM = 64
CMP_STRIDE = 16
CMP_BLOCK = 2 * CMP_STRIDE
CMP_HIDDEN = 256
SEL_BLOCK = 64
SEL_TOP_N = 8
WINDOW = 512
Q_BLOCK = 128
N_BRANCH = 3
Q_DIM = NSA_HEADS * HEAD_DIM
KV_DIM = NSA_GROUPS * HEAD_DIM
NSA_IN = Q_DIM + 6 * KV_DIM + N_BRANCH * NSA_HEADS

RNN_WIDTH = 1408
RNN_BLOCKS = 16
RNN_BLOCK_W = RNN_WIDTH // RNN_BLOCKS
CONV_WIDTH = 4
RG_C = 8.0

FFN_HIDDEN = ((8 * D_MODEL // 3 + 255) // 256) * 256

kernel_name = "hybrid_nsa_rglru_block"


def rms_norm(x, g):
    xf = x.astype(jnp.float32)
    y = xf * lax.rsqrt(jnp.mean(xf * xf, axis=-1, keepdims=True) + EPS)
    return (y * g.astype(jnp.float32)).astype(x.dtype)


def masked_softmax(s, mask):
    s = jnp.where(mask, s.astype(jnp.float32), -jnp.inf)
    m = jnp.max(s, axis=-1, keepdims=True)
    m = jnp.where(jnp.isfinite(m), m, 0.0)
    p = jnp.exp(s - m)
    return p / jnp.maximum(jnp.sum(p, axis=-1, keepdims=True), jnp.finfo(jnp.float32).tiny)


def cmp_to_sel_map(seq):
    n_c = seq // CMP_STRIDE - 1
    n_s = seq // SEL_BLOCK
    tok = np.arange(seq)
    start = np.arange(n_c) * CMP_STRIDE
    cover_c = (tok[None, :] >= start[:, None]) & (tok[None, :] < start[:, None] + CMP_BLOCK)
    cover_s = (tok[:, None] // SEL_BLOCK) == np.arange(n_s)[None, :]
    m = cover_c.astype(np.float32) @ cover_s.astype(np.float32) / np.float32(CMP_BLOCK)
    return jnp.asarray(m, dtype=jnp.float32)


def compress(t, pos, w1, w2):
    B, S, G, DH = t.shape
    ch = t.reshape(B, S // CMP_STRIDE, CMP_STRIDE, G, DH)
    blocks = jnp.concatenate([ch[:, :-1], ch[:, 1:]], axis=2) + pos[:, None, :]
    flat = jnp.transpose(blocks, (0, 1, 3, 2, 4)).reshape(B, S // CMP_STRIDE - 1, G, CMP_BLOCK * DH)
    return jax.nn.silu(flat @ w1) @ w2


def nsa_mixer(h, w_in, w_out, cmp_pos, cmp_w1, cmp_w2, q_gain, k_gain):
    B, S, _ = h.shape
    G, HPG, DH = NSA_GROUPS, NSA_HPG, HEAD_DIM
    n_c = S // CMP_STRIDE - 1
    n_s = S // SEL_BLOCK
    n_qb = S // Q_BLOCK
    top_k = min(SEL_TOP_N, n_s)
    scale = HEAD_DIM ** -0.5

    proj = h @ w_in
    splits = [Q_DIM + j * KV_DIM for j in range(7)]
    q, k_c, v_c, k_s, v_s, k_w, v_w, g = jnp.split(proj, splits, axis=-1)

    q = rms_norm(q.reshape(B, S, NSA_HEADS, DH), q_gain).reshape(B, S, G, HPG, DH)
    kv = lambda t: t.reshape(B, S, G, DH)
    k_c = rms_norm(compress(kv(k_c), cmp_pos[0], cmp_w1[0], cmp_w2[0]), k_gain[0])
    v_c = compress(kv(v_c), cmp_pos[1], cmp_w1[1], cmp_w2[1])
    k_s = rms_norm(kv(k_s), k_gain[1])
    v_s = kv(v_s)
    k_w = rms_norm(kv(k_w), k_gain[2])
    v_w = kv(v_w)
    gates = jax.nn.sigmoid(g.astype(jnp.float32)).reshape(B, S, G, HPG, N_BRANCH)

    to_blocks = lambda t: jnp.transpose(t.reshape(B, n_s, SEL_BLOCK, G, DH), (0, 3, 1, 2, 4))
    k_sb, v_sb = to_blocks(k_s), to_blocks(v_s)
    pad = ((0, 0), (WINDOW, 0), (0, 0), (0, 0))
    k_wp, v_wp = jnp.pad(k_w, pad), jnp.pad(v_w, pad)
    sel_map = cmp_to_sel_map(S)
    cmp_end = jnp.arange(n_c) * CMP_STRIDE + CMP_BLOCK - 1
    blk = jnp.arange(n_s)
    gather = jax.vmap(jax.vmap(lambda blocks, ix: blocks[ix]))

    def block_fn(args):
        qb, gb, i = args
        t = i * Q_BLOCK + jnp.arange(Q_BLOCK)
        s = jnp.einsum('bqgnd,bcgd->bgnqc', qb, k_c) * scale
        p_c = masked_softmax(s, cmp_end[None, :] <= t[:, None])
        o_c = jnp.einsum('bgnqc,bcgd->bqgnd', p_c.astype(v_c.dtype), v_c)
        imp = jnp.einsum('bgnqc,cs->bgqs', p_c, sel_map)
        cur = (t // SEL_BLOCK)[:, None]
        imp = jnp.where(blk[None, :] <= cur, imp, -jnp.inf)
        forced = (blk[None, :] == 0) | (blk[None, :] == cur) | (blk[None, :] == cur - 1)
        imp = jnp.where(forced, jnp.inf, imp)
        _, idx = lax.top_k(imp, top_k)
        k_sel = gather(k_sb, idx).reshape(B, G, Q_BLOCK, top_k * SEL_BLOCK, DH)
        v_sel = gather(v_sb, idx).reshape(B, G, Q_BLOCK, top_k * SEL_BLOCK, DH)
        kpos = (idx[..., None] * SEL_BLOCK + jnp.arange(SEL_BLOCK)).reshape(B, G, Q_BLOCK, top_k * SEL_BLOCK)
        mask_s = (kpos <= t[None, None, :, None])[:, :, None]
        s = jnp.einsum('bqgnd,bgqjd->bgnqj', qb, k_sel) * scale
        p = masked_softmax(s, mask_s)
        o_s = jnp.einsum('bgnqj,bgqjd->bqgnd', p.astype(v_sel.dtype), v_sel)
        kw = lax.dynamic_slice_in_dim(k_wp, i * Q_BLOCK, WINDOW + Q_BLOCK, axis=1)
        vw = lax.dynamic_slice_in_dim(v_wp, i * Q_BLOCK, WINDOW + Q_BLOCK, axis=1)
        wpos = i * Q_BLOCK - WINDOW + jnp.arange(WINDOW + Q_BLOCK)
        mask_w = (wpos[None, :] <= t[:, None]) & (wpos[None, :] > t[:, None] - WINDOW) & (wpos[None, :] >= 0)
        s = jnp.einsum('bqgnd,bkgd->bgnqk', qb, kw) * scale
        p = masked_softmax(s, mask_w)
        o_w = jnp.einsum('bgnqk,bkgd->bqgnd', p.astype(vw.dtype), vw)
        o = gb[..., 0:1] * o_c + gb[..., 1:2] * o_s + gb[..., 2:3] * o_w
        return o.astype(qb.dtype)

    q_blocks = jnp.moveaxis(q.reshape(B, n_qb, Q_BLOCK, G, HPG, DH), 1, 0)
    g_blocks = jnp.moveaxis(gates.reshape(B, n_qb, Q_BLOCK, G, HPG, N_BRANCH), 1, 0)
    o = lax.map(block_fn, (q_blocks, g_blocks, jnp.arange(n_qb)))
    o = jnp.moveaxis(o, 0, 1).reshape(B, S, Q_DIM)
    return o @ w_out


def lin_rec(left, right):
    a_l, b_l = left
    a_r, b_r = right
    return a_l * a_r, a_r * b_l + b_r


def rglru_mixer(h, w_in, conv_w, conv_b, w_a, b_a, w_x, b_x, lam, w_out):
    B, S, _ = h.shape
    gate, u = jnp.split(h @ w_in, 2, axis=-1)
    u = lax.conv_general_dilated(u, conv_w[:, None, :].astype(u.dtype), window_strides=(1,),
                                 padding=[(CONV_WIDTH - 1, 0)],
                                 dimension_numbers=('NWC', 'WIO', 'NWC'),
                                 feature_group_count=RNN_WIDTH) + conv_b
    ub = u.reshape(B, S, RNN_BLOCKS, RNN_BLOCK_W)
    r = jax.nn.sigmoid(jnp.einsum('bsnd,nde->bsne', ub, w_a) + b_a).reshape(B, S, RNN_WIDTH)
    i_g = jax.nn.sigmoid(jnp.einsum('bsnd,nde->bsne', ub, w_x) + b_x).reshape(B, S, RNN_WIDTH)
    log_a = -RG_C * jax.nn.softplus(-lam.astype(jnp.float32)) * r.astype(jnp.float32)
    a = jnp.exp(log_a)
    b = jnp.sqrt(-jnp.expm1(2.0 * log_a)) * (i_g * u).astype(jnp.float32)
    _, hs = lax.associative_scan(lin_rec, (a, b), axis=1)
    y = hs.astype(h.dtype) * jax.nn.gelu(gate)
    return y @ w_out


def swiglu(h, w_in, w_out):
    g, u = jnp.split(h @ w_in, 2, axis=-1)
    return (jax.nn.silu(g) * u) @ w_out


def setup_inputs(seed: int = 0) -> dict:
    key = jax.random.key(seed)
    ks = jax.random.split(key, 24)
    nrm = lambda k, shape, s: jax.random.normal(k, shape, jnp.float32) * s
    D, NN, NR = D_MODEL, N_NSA_LAYERS, N_RG_LAYERS
    a0 = jax.random.uniform(ks[20], (NR, RNN_WIDTH), jnp.float32, 0.9, 0.999)
    s0 = a0 ** (1.0 / RG_C)
    return {
        "x": nrm(ks[0], (BATCH, SEQ, D), 1.0),
        "c": nrm(ks[1], (BATCH, D), 1.0),
        "ada_w": nrm(ks[2], (DEPTH, D, 6 * D), 0.5 * D ** -0.5),
        "ada_b": nrm(ks[3], (DEPTH, 6 * D), 0.02),
        "norm1_g": 1.0 + nrm(ks[4], (DEPTH, D), 0.1),
        "norm2_g": 1.0 + nrm(ks[5], (DEPTH, D), 0.1),
        "nsa_w_in": nrm(ks[6], (NN, D, NSA_IN), D ** -0.5),
        "nsa_w_out": nrm(ks[7], (NN, Q_DIM, D), Q_DIM ** -0.5),
        "nsa_cmp_pos": nrm(ks[8], (NN, 2, CMP_BLOCK, HEAD_DIM), 0.1),
        "nsa_cmp_w1": nrm(ks[9], (NN, 2, CMP_BLOCK * HEAD_DIM, CMP_HIDDEN), (CMP_BLOCK * HEAD_DIM) ** -0.5),
        "nsa_cmp_w2": nrm(ks[10], (NN, 2, CMP_HIDDEN, HEAD_DIM), CMP_HIDDEN ** -0.5),
        "nsa_q_gain": 1.0 + nrm(ks[11], (NN, HEAD_DIM), 0.1),
        "nsa_k_gain": 1.0 + nrm(ks[12], (NN, N_BRANCH, HEAD_DIM), 0.1),
        "rg_w_in": nrm(ks[13], (NR, D, 2 * RNN_WIDTH), D ** -0.5),
        "rg_conv_w": nrm(ks[14], (NR, CONV_WIDTH, RNN_WIDTH), CONV_WIDTH ** -0.5),
        "rg_conv_b": nrm(ks[15], (NR, RNN_WIDTH), 0.02),
        "rg_w_a": nrm(ks[16], (NR, RNN_BLOCKS, RNN_BLOCK_W, RNN_BLOCK_W), RNN_BLOCK_W ** -0.5),
        "rg_b_a": nrm(ks[17], (NR, RNN_BLOCKS, RNN_BLOCK_W), 0.02),
        "rg_w_x": nrm(ks[18], (NR, RNN_BLOCKS, RNN_BLOCK_W, RNN_BLOCK_W), RNN_BLOCK_W ** -0.5),
        "rg_b_x": nrm(ks[19], (NR, RNN_BLOCKS, RNN_BLOCK_W), 0.02),
        "rg_lam": jnp.log(s0) - jnp.log1p(-s0),
        "rg_w_out": nrm(ks[21], (NR, RNN_WIDTH, D), RNN_WIDTH ** -0.5),
        "ffn_w_in": nrm(ks[22], (DEPTH, D, 2 * FFN_HIDDEN), D ** -0.5),
        "ffn_w_out": nrm(ks[23], (DEPTH, FFN_HIDDEN, D), FFN_HIDDEN ** -0.5),
    }


def reference(x, c, ada_w, ada_b, norm1_g, norm2_g, nsa_w_in, nsa_w_out, nsa_cmp_pos,
              nsa_cmp_w1, nsa_cmp_w2, nsa_q_gain, nsa_k_gain, rg_w_in, rg_conv_w, rg_conv_b,
              rg_w_a, rg_b_a, rg_w_x, rg_b_x, rg_lam, rg_w_out, ffn_w_in, ffn_w_out):
    cond = jax.nn.silu(c)
    for layer in range(DEPTH):
        mod = (cond @ ada_w[layer] + ada_b[layer])[:, None, :]
        sh1, sc1, g1, sh2, sc2, g2 = jnp.split(mod, 6, axis=-1)
        hh = rms_norm(x, norm1_g[layer]) * (1.0 + sc1) + sh1
        j = layer // N_MIXERS
        if layer % N_MIXERS == 0:
            mix = nsa_mixer(hh, nsa_w_in[j], nsa_w_out[j], nsa_cmp_pos[j], nsa_cmp_w1[j],
                            nsa_cmp_w2[j], nsa_q_gain[j], nsa_k_gain[j])
        else:
            mix = rglru_mixer(hh, rg_w_in[j], rg_conv_w[j], rg_conv_b[j], rg_w_a[j], rg_b_a[j],
                              rg_w_x[j], rg_b_x[j], rg_lam[j], rg_w_out[j])
        x = x + g1 * mix
        hh = rms_norm(x, norm2_g[layer]) * (1.0 + sc2) + sh2
        x = x + g2 * swiglu(hh, ffn_w_in[layer], ffn_w_out[layer])
    return x
```

```python
import functools

import numpy as np
import jax
import jax.numpy as jnp
from jax import lax
from jax.experimental import pallas as pl
from jax.experimental.pallas import tpu as pltpu

F32 = jnp.float32
BF16 = jnp.bfloat16

EPS = 1e-6
N_MIXERS = 2

NSA_HEADS = 16
NSA_GROUPS = 4
NSA_HPG = NSA_HEADS // NSA_GROUPS
HEAD_DIM = 64
CMP_STRIDE = 16
CMP_BLOCK = 2 * CMP_STRIDE
SEL_BLOCK = 64
SEL_TOP_N = 8
WINDOW = 512
Q_BLOCK = 128
N_BRANCH = 3
Q_DIM = NSA_HEADS * HEAD_DIM
KV_DIM = NSA_GROUPS * HEAD_DIM
GATE_COLS = N_BRANCH * NSA_HEADS
GATE_PER_GROUP = N_BRANCH * NSA_HPG

RNN_BLOCKS = 16
CONV_WIDTH = 4
RG_C = 8.0

LANES = 128
SUBLANES = 8
MXU_WIDTH = 256
VMEM_LIMIT_BYTES = 56 * 1024 * 1024

NEG_BIG = -1e30
SEL_CHUNK = 256
RG_TIME_TILE = 64
ROW_TILE = 512


def _cparams(n_axes):
    return pltpu.CompilerParams(
        dimension_semantics=("arbitrary",) * n_axes,
        vmem_limit_bytes=VMEM_LIMIT_BYTES,
    )


def _norm_mod(x, gain, scale, shift):
    ms = jnp.mean(x * x, axis=-1, keepdims=True)
    y = x * lax.rsqrt(ms + EPS) * gain
    return y * (1.0 + scale) + shift


def _head_norm(t, gain):
    ms = jnp.mean(t * t, axis=-1, keepdims=True)
    return t * lax.rsqrt(ms + EPS) * gain


def _ada_kernel(c_ref, w_ref, b_ref, o_ref):
    cond = jax.nn.silu(c_ref[...])
    o_ref[0] = jnp.dot(cond, w_ref[0], preferred_element_type=F32,
                       precision=lax.Precision.HIGHEST) + b_ref[0]


def _ada_mod(c, ada_w, ada_b):
    depth, d, n = ada_w.shape
    b = c.shape[0]
    tn = n // 4
    return pl.pallas_call(
        _ada_kernel,
        grid=(depth, n // tn),
        in_specs=[
            pl.BlockSpec((b, d), lambda l, j: (0, 0)),
            pl.BlockSpec((1, d, tn), lambda l, j: (l, 0, j)),
            pl.BlockSpec((1, 1, tn), lambda l, j: (l, 0, j)),
        ],
        out_specs=pl.BlockSpec((1, b, tn), lambda l, j: (l, 0, j)),
        out_shape=jax.ShapeDtypeStruct((depth, b, n), F32),
        compiler_params=_cparams(2),
        name="ada_mod",
    )(c, ada_w, ada_b.reshape(depth, 1, n))


def _nsa_proj_kernel(x_ref, g_ref, sc_ref, sh_ref, w_ref, q_ref, kv_ref, gate_ref):
    hh = _norm_mod(x_ref[0], g_ref[...], sc_ref[0], sh_ref[0])
    p = jnp.dot(hh.astype(BF16), w_ref[...], preferred_element_type=F32)
    q_ref[0] = p[:, :Q_DIM]
    for j in range(6):
        for g in range(NSA_GROUPS):
            lo = Q_DIM + j * KV_DIM + g * HEAD_DIM
            kv_ref[j, 0, g] = p[:, lo:lo + HEAD_DIM]
    gate = jax.nn.sigmoid(p[:, Q_DIM + 6 * KV_DIM:])
    for g in range(NSA_GROUPS):
        shift = (LANES - GATE_PER_GROUP * g) % LANES
        gate_ref[0, g] = gate if shift == 0 else pltpu.roll(gate, shift, 1)


def _nsa_proj(x, gain, sc, sh, w_pad):
    b, s, d = x.shape
    tm = min(ROW_TILE, s)
    n_pad = w_pad.shape[1]
    vec = pl.BlockSpec((1, 1, d), lambda bi, i: (bi, 0, 0))
    return pl.pallas_call(
        _nsa_proj_kernel,
        grid=(b, s // tm),
        in_specs=[
            pl.BlockSpec((1, tm, d), lambda bi, i: (bi, i, 0)),
            pl.BlockSpec((1, d), lambda bi, i: (0, 0)),
            vec, vec,
            pl.BlockSpec((d, n_pad), lambda bi, i: (0, 0)),
        ],
        out_specs=[
            pl.BlockSpec((1, tm, Q_DIM), lambda bi, i: (bi, i, 0)),
            pl.BlockSpec((6, 1, NSA_GROUPS, tm, HEAD_DIM), lambda bi, i: (0, bi, 0, i, 0)),
            pl.BlockSpec((1, NSA_GROUPS, tm, LANES), lambda bi, i: (bi, 0, i, 0)),
        ],
        out_shape=[
            jax.ShapeDtypeStruct((b, s, Q_DIM), F32),
            jax.ShapeDtypeStruct((6, b, NSA_GROUPS, s, HEAD_DIM), F32),
            jax.ShapeDtypeStruct((b, NSA_GROUPS, s, LANES), F32),
        ],
        compiler_params=_cparams(2),
        name="nsa_proj",
    )(x, gain, sc, sh, w_pad)


def _compress_kernel(ch_ref, pos_ref, w1_ref, w2_ref, kg_ref, o_ref):
    j = pl.program_id(0)
    g, nch, width = ch_ref.shape[2:]
    rows = g * nch
    ch = ch_ref[0, 0].reshape(rows, width)
    first = jnp.dot((ch + pos_ref[0, 0:1]).astype(BF16), w1_ref[0, 0], preferred_element_type=F32)
    second = jnp.dot((ch + pos_ref[0, 1:2]).astype(BF16), w1_ref[0, 1], preferred_element_type=F32)
    h1 = first + pltpu.roll(second, rows - 1, 0)
    out = jnp.dot(jax.nn.silu(h1).astype(BF16), w2_ref[0], preferred_element_type=F32)
    out = jnp.where(j == 0, _head_norm(out, kg_ref[0:1]), out)
    r = lax.broadcasted_iota(jnp.int32, out.shape, 0)
    out = jnp.where((r & (nch - 1)) == nch - 1, 0.0, out)
    o_ref[0, 0] = out.reshape(g, nch, HEAD_DIM)


def _compress(kv, pos, w1, w2, k_gain):
    _, b, g, s, dh = kv.shape
    nch = s // CMP_STRIDE
    width = CMP_STRIDE * dh
    chunks = kv.reshape(6, b, g, nch, width)
    hidden = w1.shape[-1]
    return pl.pallas_call(
        _compress_kernel,
        grid=(2, b),
        in_specs=[
            pl.BlockSpec((1, 1, g, nch, width), lambda j, bi: (j, bi, 0, 0, 0)),
            pl.BlockSpec((1, 2, width), lambda j, bi: (j, 0, 0)),
            pl.BlockSpec((1, 2, width, hidden), lambda j, bi: (j, 0, 0, 0)),
            pl.BlockSpec((1, hidden, dh), lambda j, bi: (j, 0, 0)),
            pl.BlockSpec((N_BRANCH, dh), lambda j, bi: (0, 0)),
        ],
        out_specs=pl.BlockSpec((1, 1, g, nch, dh), lambda j, bi: (j, bi, 0, 0, 0)),
        out_shape=jax.ShapeDtypeStruct((2, b, g, nch, dh), F32),
        compiler_params=_cparams(2),
        name="nsa_compress",
    )(chunks, pos.reshape(2, 2, width), w1.astype(BF16).reshape(2, 2, width, hidden),
      w2.astype(BF16), k_gain)


def _softmax_rows(s):
    m = jnp.max(s, axis=-1, keepdims=True)
    p = jnp.exp(s - m)
    return p, jnp.sum(p, axis=-1, keepdims=True)


def _attn_kernel(q_ref, gate_ref, kc_ref, vc_ref, ks_ref, vs_ref, kw_ref, vw_ref,
                 selmap_ref, expand_ref, qg_ref, kg_ref, o_ref,
                 ksn, vsb, kwp, vwp, m_scr, l_scr, acc_scr):
    i = pl.program_id(2)
    seq = ks_ref.shape[3]
    ncmp = kc_ref.shape[3]
    nsel = selmap_ref.shape[0]
    rows = NSA_HPG * Q_BLOCK
    scale = HEAD_DIM ** -0.5
    nt_dims = (((1,), (1,)), ((), ()))

    @pl.when(i == 0)
    def _prepare_kv():
        kwp[0:WINDOW, :] = jnp.zeros((WINDOW, HEAD_DIM), BF16)
        vwp[0:WINDOW, :] = jnp.zeros((WINDOW, HEAD_DIM), BF16)
        step = 256

        def body(c, carry):
            r0 = pl.multiple_of(c * step, step)
            ksn[pl.ds(r0, step), :] = _head_norm(ks_ref[0, 0, 0, pl.ds(r0, step), :], kg_ref[1:2]).astype(BF16)
            vsb[pl.ds(r0, step), :] = vs_ref[0, 0, 0, pl.ds(r0, step), :].astype(BF16)
            kwp[pl.ds(WINDOW + r0, step), :] = _head_norm(kw_ref[0, 0, 0, pl.ds(r0, step), :], kg_ref[2:3]).astype(BF16)
            vwp[pl.ds(WINDOW + r0, step), :] = vw_ref[0, 0, 0, pl.ds(r0, step), :].astype(BF16)
            return carry

        lax.fori_loop(0, seq // step, body, 0)

    q = q_ref[0]
    qn = jnp.concatenate(
        [_head_norm(q[:, n * HEAD_DIM:(n + 1) * HEAD_DIM], qg_ref[...]).astype(BF16) for n in range(NSA_HPG)],
        axis=0)
    t0 = i * Q_BLOCK

    s = lax.dot_general(qn, kc_ref[0, 0, 0].astype(BF16), nt_dims, preferred_element_type=F32) * scale
    tq = t0 + (lax.broadcasted_iota(jnp.int32, (rows, ncmp), 0) & (Q_BLOCK - 1))
    cmp_end = lax.broadcasted_iota(jnp.int32, (rows, ncmp), 1) * CMP_STRIDE + (CMP_BLOCK - 1)
    visible = cmp_end <= tq
    s = jnp.where(visible, s, NEG_BIG)
    m = jnp.max(s, axis=-1, keepdims=True)
    p = jnp.where(visible, jnp.exp(s - m), 0.0)
    l = jnp.sum(p, axis=-1, keepdims=True)
    p_c = p / jnp.maximum(l, jnp.finfo(F32).tiny)
    o_c = jnp.dot(p_c.astype(BF16), vc_ref[0, 0, 0].astype(BF16), preferred_element_type=F32)

    p_sum = p_c[0:Q_BLOCK]
    for n in range(1, NSA_HPG):
        p_sum = p_sum + p_c[n * Q_BLOCK:(n + 1) * Q_BLOCK]
    imp = lax.dot_general(selmap_ref[...], p_sum, nt_dims, preferred_element_type=F32,
                          precision=lax.Precision.HIGHEST)
    blk = lax.broadcasted_iota(jnp.int32, (nsel, Q_BLOCK), 0)
    cur = lax.shift_right_logical(t0 + lax.broadcasted_iota(jnp.int32, (nsel, Q_BLOCK), 1), 6)
    valid = blk <= cur
    forced = (blk == 0) | (blk == cur) | (blk == cur - 1)
    val = jnp.where(valid, imp, -jnp.inf)
    val = jnp.where(forced, jnp.inf, val)
    rank = jnp.zeros((nsel, Q_BLOCK), jnp.int32)
    for j2 in range(nsel):
        row = val[j2:j2 + 1, :]
        beats = (row > val) | ((row == val) & (blk > j2))
        rank = rank + beats.astype(jnp.int32)
    sel_t = ((rank < SEL_TOP_N) & valid).astype(F32)
    sel_t = jnp.concatenate([sel_t, jnp.zeros((LANES - nsel, Q_BLOCK), F32)], axis=0)
    sel = sel_t.T.astype(BF16)

    m_scr[...] = jnp.full(m_scr.shape, NEG_BIG, F32)
    l_scr[...] = jnp.zeros(l_scr.shape, F32)
    acc_scr[...] = jnp.zeros(acc_scr.shape, F32)
    reps = SEL_CHUNK // LANES

    def sel_step(kb, carry):
        k0 = pl.multiple_of(kb * SEL_CHUNK, SEL_CHUNK)
        sc = lax.dot_general(qn, ksn[pl.ds(k0, SEL_CHUNK), :], nt_dims, preferred_element_type=F32) * scale
        picked = jnp.dot(sel, expand_ref[kb], preferred_element_type=F32)
        kpos = k0 + lax.broadcasted_iota(jnp.int32, (Q_BLOCK, SEL_CHUNK), 1)
        tq1 = t0 + lax.broadcasted_iota(jnp.int32, (Q_BLOCK, SEL_CHUNK), 0)
        bias = jnp.where((picked > 0.5) & (kpos <= tq1), 0.0, NEG_BIG)
        sc = sc + jnp.concatenate([bias] * NSA_HPG, axis=0)
        m_prev = m_scr[...]
        m_new = jnp.maximum(m_prev, jnp.max(sc, axis=-1, keepdims=True))
        alpha = jnp.exp(m_prev - m_new)
        pr = jnp.exp(sc - jnp.concatenate([m_new] * reps, axis=1))
        l_scr[...] = alpha * l_scr[...] + jnp.sum(pr, axis=-1, keepdims=True)
        acc_scr[...] = acc_scr[...] * alpha[:, :HEAD_DIM] + jnp.dot(
            pr.astype(BF16), vsb[pl.ds(k0, SEL_CHUNK), :], preferred_element_type=F32)
        m_scr[...] = m_new
        return carry

    n_chunks = (t0 + Q_BLOCK + SEL_CHUNK - 1) // SEL_CHUNK
    lax.fori_loop(0, n_chunks, sel_step, 0)
    o_s = acc_scr[...] / l_scr[:, :HEAD_DIM]

    span = WINDOW + Q_BLOCK
    w0 = pl.multiple_of(t0, Q_BLOCK)
    sw = lax.dot_general(qn, kwp[pl.ds(w0, span), :], nt_dims, preferred_element_type=F32) * scale
    wpos = t0 - WINDOW + lax.broadcasted_iota(jnp.int32, (Q_BLOCK, span), 1)
    tq2 = t0 + lax.broadcasted_iota(jnp.int32, (Q_BLOCK, span), 0)
    in_win = (wpos <= tq2) & (wpos > tq2 - WINDOW) & (wpos >= 0)
    sw = sw + jnp.concatenate([jnp.where(in_win, 0.0, NEG_BIG)] * NSA_HPG, axis=0)
    pw, lw = _softmax_rows(sw)
    o_w = jnp.dot(pw.astype(BF16), vwp[pl.ds(w0, span), :], preferred_element_type=F32) / lw

    gate = gate_ref[0, 0]
    for n in range(NSA_HPG):
        r = slice(n * Q_BLOCK, (n + 1) * Q_BLOCK)
        c0 = N_BRANCH * n
        o_ref[0, :, n * HEAD_DIM:(n + 1) * HEAD_DIM] = (
            gate[:, c0:c0 + 1] * o_c[r] + gate[:, c0 + 1:c0 + 2] * o_s[r] + gate[:, c0 + 2:c0 + 3] * o_w[r])


def _sel_constants(seq):
    n_c = seq // CMP_STRIDE - 1
    n_s = seq // SEL_BLOCK
    tok = np.arange(seq)
    start = np.arange(n_c) * CMP_STRIDE
    cover_c = (tok[None, :] >= start[:, None]) & (tok[None, :] < start[:, None] + CMP_BLOCK)
    cover_s = (tok[:, None] // SEL_BLOCK) == np.arange(n_s)[None, :]
    m = cover_c.astype(np.float32) @ cover_s.astype(np.float32) / np.float32(CMP_BLOCK)
    selmap_t = np.zeros((n_s, n_c + 1), np.float32)
    selmap_t[:, :n_c] = m.T
    n_chunks = seq // SEL_CHUNK
    expand = np.zeros((n_chunks, LANES, SEL_CHUNK), np.float32)
    for kb in range(n_chunks):
        kpos = kb * SEL_CHUNK + np.arange(SEL_CHUNK)
        expand[kb, kpos // SEL_BLOCK, np.arange(SEL_CHUNK)] = 1.0
    return jnp.asarray(selmap_t), jnp.asarray(expand, dtype=BF16)


def _attention(q, gates, kvc, kv, q_gain, k_gain):
    b, s, _ = q.shape
    g = NSA_GROUPS
    nqb = s // Q_BLOCK
    ncmp = s // CMP_STRIDE
    nsel = s // SEL_BLOCK
    assert nsel <= LANES and s % SEL_CHUNK == 0
    selmap_t, expand = _sel_constants(s)
    gw = NSA_HPG * HEAD_DIM
    rows = NSA_HPG * Q_BLOCK

    def kv_spec(j):
        return pl.BlockSpec((1, 1, 1, s, HEAD_DIM), lambda bi, gi, i: (j, bi, gi, 0, 0))

    def kvc_spec(j):
        return pl.BlockSpec((1, 1, 1, ncmp, HEAD_DIM), lambda bi, gi, i: (j, bi, gi, 0, 0))

    return pl.pallas_call(
        _attn_kernel,
        grid=(b, g, nqb),
        in_specs=[
            pl.BlockSpec((1, Q_BLOCK, gw), lambda bi, gi, i: (bi, i, gi)),
            pl.BlockSpec((1, 1, Q_BLOCK, LANES), lambda bi, gi, i: (bi, gi, i, 0)),
            kvc_spec(0), kvc_spec(1),
            kv_spec(2), kv_spec(3), kv_spec(4), kv_spec(5),
            pl.BlockSpec((nsel, ncmp), lambda bi, gi, i: (0, 0)),
            pl.BlockSpec((s // SEL_CHUNK, LANES, SEL_CHUNK), lambda bi, gi, i: (0, 0, 0)),
            pl.BlockSpec((1, HEAD_DIM), lambda bi, gi, i: (0, 0)),
            pl.BlockSpec((N_BRANCH, HEAD_DIM), lambda bi, gi, i: (0, 0)),
        ],
        out_specs=pl.BlockSpec((1, Q_BLOCK, gw), lambda bi, gi, i: (bi, i, gi)),
        out_shape=jax.ShapeDtypeStruct((b, s, Q_DIM), F32),
        scratch_shapes=[
            pltpu.VMEM((s, HEAD_DIM), BF16),
            pltpu.VMEM((s, HEAD_DIM), BF16),
            pltpu.VMEM((WINDOW + s, HEAD_DIM), BF16),
            pltpu.VMEM((WINDOW + s, HEAD_DIM), BF16),
            pltpu.VMEM((rows, LANES), F32),
            pltpu.VMEM((rows, LANES), F32),
            pltpu.VMEM((rows, HEAD_DIM), F32),
        ],
        compiler_params=_cparams(3),
        name="nsa_attention",
    )(q, gates, kvc, kvc, kv, kv, kv, kv, selmap_t, expand, q_gain, k_gain)


def _out_proj_kernel(y_ref, x_ref, gate_ref, w_ref, o_ref):
    mix = jnp.dot(y_ref[0].astype(BF16), w_ref[...], preferred_element_type=F32)
    o_ref[0] = x_ref[0] + gate_ref[0] * mix


def _out_proj(y, x, gate, w):
    b, s, d = x.shape
    k = y.shape[-1]
    tm = min(ROW_TILE, s)
    return pl.pallas_call(
        _out_proj_kernel,
        grid=(b, s // tm),
        in_specs=[
            pl.BlockSpec((1, tm, k), lambda bi, i: (bi, i, 0)),
            pl.BlockSpec((1, tm, d), lambda bi, i: (bi, i, 0)),
            pl.BlockSpec((1, 1, d), lambda bi, i: (bi, 0, 0)),
            pl.BlockSpec((k, d), lambda bi, i: (0, 0)),
        ],
        out_specs=pl.BlockSpec((1, tm, d), lambda bi, i: (bi, i, 0)),
        out_shape=jax.ShapeDtypeStruct((b, s, d), F32),
        compiler_params=_cparams(2),
        name="out_proj",
    )(y, x, gate, w)


def _ffn_kernel(x_ref, g_ref, sc_ref, sh_ref, gate_ref, wg_ref, wu_ref, wo_ref, o_ref):
    x = x_ref[0]
    hh = _norm_mod(x, g_ref[...], sc_ref[0], sh_ref[0]).astype(BF16)
    a = jnp.dot(hh, wg_ref[...], preferred_element_type=F32)
    u = jnp.dot(hh, wu_ref[...], preferred_element_type=F32)
    act = (jax.nn.silu(a) * u).astype(BF16)
    o_ref[0] = x + gate_ref[0] * jnp.dot(act, wo_ref[...], preferred_element_type=F32)


def _ffn(x, gain, sc, sh, gate, w_in, w_out):
    b, s, d = x.shape
    hidden = w_out.shape[0]
    tm = min(ROW_TILE, s)
    vec = pl.BlockSpec((1, 1, d), lambda bi, i: (bi, 0, 0))
    return pl.pallas_call(
        _ffn_kernel,
        grid=(b, s // tm),
        in_specs=[
            pl.BlockSpec((1, tm, d), lambda bi, i: (bi, i, 0)),
            pl.BlockSpec((1, d), lambda bi, i: (0, 0)),
            vec, vec, vec,
            pl.BlockSpec((d, hidden), lambda bi, i: (0, 0)),
            pl.BlockSpec((d, hidden), lambda bi, i: (0, 1)),
            pl.BlockSpec((hidden, d), lambda bi, i: (0, 0)),
        ],
        out_specs=pl.BlockSpec((1, tm, d), lambda bi, i: (bi, i, 0)),
        out_shape=jax.ShapeDtypeStruct((b, s, d), F32),
        compiler_params=_cparams(2),
        name="ffn",
    )(x, gain, sc, sh, gate, w_in, w_in, w_out)


def _rg_proj_kernel(x_ref, g_ref, sc_ref, sh_ref, w_ref, gate_ref, u_ref):
    hh = _norm_mod(x_ref[0], g_ref[...], sc_ref[0], sh_ref[0])
    p = jnp.dot(hh.astype(BF16), w_ref[...], preferred_element_type=F32)
    r = gate_ref.shape[-1]
    gate_ref[0] = p[:, :r]
    u_ref[0] = p[:, r:]


def _rg_proj(x, gain, sc, sh, w):
    b, s, d = x.shape
    r = w.shape[1] // 2
    tm = min(ROW_TILE, s)
    vec = pl.BlockSpec((1, 1, d), lambda bi, i: (bi, 0, 0))
    out = pl.BlockSpec((1, tm, r), lambda bi, i: (bi, i, 0))
    return pl.pallas_call(
        _rg_proj_kernel,
        grid=(b, s // tm),
        in_specs=[
            pl.BlockSpec((1, tm, d), lambda bi, i: (bi, i, 0)),
            pl.BlockSpec((1, d), lambda bi, i: (0, 0)),
            vec, vec,
            pl.BlockSpec((d, 2 * r), lambda bi, i: (0, 0)),
        ],
        out_specs=[out, out],
        out_shape=[jax.ShapeDtypeStruct((b, s, r), F32)] * 2,
        compiler_params=_cparams(2),
        name="rg_proj",
    )(x, gain, sc, sh, w)


def _band_starts(width, block_w):
    band = 2 * MXU_WIDTH
    starts = []
    for c0 in range(0, width, MXU_WIDTH):
        c1 = min(c0 + MXU_WIDTH, width)
        lo = (c0 // block_w) * block_w
        hi = ((c1 - 1) // block_w + 1) * block_w
        k0 = min((lo // LANES) * LANES, width - band)
        assert k0 >= 0 and k0 <= lo and hi <= k0 + band
        starts.append(k0)
    return starts


def _band_weights(w_a, w_x):
    nb, bw, _ = w_a.shape
    width = nb * bw
    eye = jnp.eye(nb, dtype=w_a.dtype)
    dense = lambda w: jnp.einsum('nde,nm->ndme', w, eye).reshape(width, width)
    da, dx = dense(w_a), dense(w_x)
    band = 2 * MXU_WIDTH
    tiles = []
    for t, k0 in enumerate(_band_starts(width, bw)):
        c0 = t * MXU_WIDTH
        c1 = min(c0 + MXU_WIDTH, width)
        pad = ((0, 0), (0, MXU_WIDTH - (c1 - c0)))
        tiles.append(jnp.concatenate(
            [jnp.pad(da[k0:k0 + band, c0:c1], pad), jnp.pad(dx[k0:k0 + band, c0:c1], pad)], axis=1))
    return jnp.stack(tiles).astype(BF16)


def _rg_scan_kernel(u_ref, gate_ref, cw_ref, cb_ref, wb_ref, ba_ref, bx_ref, lam_ref, y_ref,
                    u_halo, h_scr, a_scr, b_scr, hs_scr, *, band_starts, block_w):
    step = pl.program_id(0)
    nb, ts, width = u_ref.shape
    rows = nb * ts
    halo = SUBLANES
    band = 2 * MXU_WIDTH

    @pl.when(step == 0)
    def _init():
        u_halo[:, 0:halo, :] = jnp.zeros((nb, halo, width), F32)
        h_scr[...] = jnp.zeros(h_scr.shape, F32)

    u_halo[:, halo:, :] = u_ref[...]
    uc = jnp.zeros((nb, ts, width), F32) + cb_ref[...]
    for k in range(CONV_WIDTH):
        off = halo - (CONV_WIDTH - 1) + k
        uc = uc + cw_ref[k:k + 1, :] * u_halo[:, off:off + ts, :]
    u_halo[:, 0:halo, :] = u_halo[:, ts:ts + halo, :]
    uc = uc.reshape(rows, width)
    ub = uc.astype(BF16)

    z = -lam_ref[...]
    log_a_unit = -RG_C * (jnp.maximum(z, 0.0) + jnp.log1p(jnp.exp(-jnp.abs(z))))

    for t, k0 in enumerate(band_starts):
        c0 = t * MXU_WIDTH
        c1 = min(c0 + MXU_WIDTH, width)
        n = c1 - c0
        zz = jnp.dot(ub[:, k0:k0 + band], wb_ref[t], preferred_element_type=F32)
        r = jax.nn.sigmoid(zz[:, :n] + ba_ref[:, c0:c1])
        ig = jax.nn.sigmoid(zz[:, MXU_WIDTH:MXU_WIDTH + n] + bx_ref[:, c0:c1])
        log_a = log_a_unit[:, c0:c1] * r
        a = jnp.exp(log_a)
        bb = jnp.sqrt(-jnp.tanh(log_a) * (a * a + 1.0)) * (ig * uc[:, c0:c1])
        for ct in range(n // LANES):
            lanes = slice(ct * LANES, (ct + 1) * LANES)
            a_scr[c0 // LANES + ct] = a[:, lanes]
            b_scr[c0 // LANES + ct] = bb[:, lanes]

    n_ct = width // LANES

    def scan_step(t, h):
        rows_t = pl.ds(t, nb, stride=ts)
        new = []
        for ct in range(n_ct):
            hc = a_scr[ct, rows_t, :] * h[ct] + b_scr[ct, rows_t, :]
            hs_scr[ct, rows_t, :] = hc
            new.append(hc)
        return tuple(new)

    h_fin = lax.fori_loop(0, ts, scan_step, tuple(h_scr[ct] for ct in range(n_ct)))
    gate = gate_ref[...].reshape(rows, width)
    for ct in range(n_ct):
        lanes = slice(ct * LANES, (ct + 1) * LANES)
        h_scr[ct] = h_fin[ct]
        y_ref[:, :, lanes] = (hs_scr[ct] * jax.nn.gelu(gate[:, lanes])).reshape(nb, ts, LANES)


def _rg_scan(u, gate, conv_w, conv_b, w_band, b_a, b_x, lam, block_w):
    b, s, width = u.shape
    assert b == SUBLANES
    ts = min(RG_TIME_TILE, s)
    rows = b * ts
    band_starts = tuple(_band_starts(width, block_w))
    tile = pl.BlockSpec((b, ts, width), lambda t: (0, t, 0))
    vec = pl.BlockSpec((1, width), lambda t: (0, 0))
    kern = functools.partial(_rg_scan_kernel, band_starts=band_starts, block_w=block_w)
    return pl.pallas_call(
        kern,
        grid=(s // ts,),
        in_specs=[
            tile, tile,
            pl.BlockSpec((CONV_WIDTH, width), lambda t: (0, 0)),
            vec,
            pl.BlockSpec(w_band.shape, lambda t: (0, 0, 0)),
            vec, vec, vec,
        ],
        out_specs=tile,
        out_shape=jax.ShapeDtypeStruct((b, s, width), F32),
        scratch_shapes=[
            pltpu.VMEM((b, ts + SUBLANES, width), F32),
            pltpu.VMEM((width // LANES, b, LANES), F32),
            pltpu.VMEM((width // LANES, rows, LANES), F32),
            pltpu.VMEM((width // LANES, rows, LANES), F32),
            pltpu.VMEM((width // LANES, rows, LANES), F32),
        ],
        compiler_params=_cparams(1),
        name="rg_scan",
    )(u, gate, conv_w, conv_b, w_band, b_a, b_x, lam)


def kernel(x, c, ada_w, ada_b, norm1_g, norm2_g, nsa_w_in, nsa_w_out, nsa_cmp_pos, nsa_cmp_w1, nsa_cmp_w2, nsa_q_gain, nsa_k_gain, rg_w_in, rg_conv_w, rg_conv_b, rg_w_a, rg_b_a, rg_w_x, rg_b_x, rg_lam, rg_w_out, ffn_w_in, ffn_w_out):
    depth, d, _ = ada_w.shape
    b = x.shape[0]
    mod = _ada_mod(c, ada_w, ada_b).reshape(depth, b, 6, 1, d)
    for layer in range(depth):
        sh1, sc1, g1, sh2, sc2, g2 = (mod[layer, :, k] for k in range(6))
        n1 = norm1_g[layer].reshape(1, d)
        n2 = norm2_g[layer].reshape(1, d)
        j = layer // N_MIXERS
        if layer % N_MIXERS == 0:
            w_in = nsa_w_in[j]
            n_real = w_in.shape[1]
            n_pad = -(-n_real // LANES) * LANES
            w_pad = jnp.pad(w_in, ((0, 0), (0, n_pad - n_real))).astype(BF16)
            q, kv, gates = _nsa_proj(x, n1, sc1, sh1, w_pad)
            kvc = _compress(kv, nsa_cmp_pos[j], nsa_cmp_w1[j], nsa_cmp_w2[j], nsa_k_gain[j])
            o = _attention(q, gates, kvc, kv, nsa_q_gain[j].reshape(1, HEAD_DIM), nsa_k_gain[j])
            x = _out_proj(o, x, g1, nsa_w_out[j].astype(BF16))
        else:
            width = rg_w_out.shape[1]
            block_w = rg_w_a.shape[-1]
            gate, u = _rg_proj(x, n1, sc1, sh1, rg_w_in[j].astype(BF16))
            y = _rg_scan(u, gate, rg_conv_w[j], rg_conv_b[j].reshape(1, width),
                         _band_weights(rg_w_a[j], rg_w_x[j]),
                         rg_b_a[j].reshape(1, width), rg_b_x[j].reshape(1, width),
                         rg_lam[j].reshape(1, width), block_w)
            x = _out_proj(y, x, g1, rg_w_out[j].astype(BF16))
        x = _ffn(x, n2, sc2, sh2, g2, ffn_w_in[layer].astype(BF16), ffn_w_out[layer].astype(BF16))
    return x
```

```python
import functools

import numpy as np
import jax
import jax.numpy as jnp
from jax import lax
from jax.experimental import pallas as pl
from jax.experimental.pallas import tpu as pltpu

F32 = jnp.float32
BF16 = jnp.bfloat16

EPS = 1e-6
N_MIXERS = 2

NSA_HEADS = 16
NSA_GROUPS = 4
NSA_HPG = NSA_HEADS // NSA_GROUPS
HEAD_DIM = 64
CMP_STRIDE = 16
CMP_BLOCK = 2 * CMP_STRIDE
SEL_BLOCK = 64
SEL_TOP_N = 8
WINDOW = 512
Q_BLOCK = 128
N_BRANCH = 3
Q_DIM = NSA_HEADS * HEAD_DIM
KV_DIM = NSA_GROUPS * HEAD_DIM
GATE_COLS = N_BRANCH * NSA_HEADS
GATE_PER_GROUP = N_BRANCH * NSA_HPG

RNN_BLOCKS = 16
CONV_WIDTH = 4
RG_C = 8.0

LANES = 128
SUBLANES = 8
MXU_WIDTH = 256
VMEM_LIMIT_BYTES = 56 * 1024 * 1024

NEG_BIG = -1e30
LOG2_E = 1.4426950408889634
SCORE_LANES = 512
SEL_CHUNK = 256
RG_TIME_TILE = 64
ROW_TILE = 512


def _cparams(n_axes, flags=None):
    return pltpu.CompilerParams(
        dimension_semantics=("arbitrary",) * n_axes,
        vmem_limit_bytes=VMEM_LIMIT_BYTES,
        flags=flags,
    )


def _norm_mod(x, gain, scale, shift):
    ms = jnp.mean(x * x, axis=-1, keepdims=True)
    y = x * lax.rsqrt(ms + EPS) * gain
    return y * (1.0 + scale) + shift


def _ada_kernel(c_ref, w_ref, b_ref, o_ref):
    cond = jax.nn.silu(c_ref[...])
    o_ref[0] = jnp.dot(cond, w_ref[0], preferred_element_type=F32,
                       precision=lax.Precision.HIGHEST) + b_ref[0]


def _ada_mod(c, ada_w, ada_b):
    depth, d, n = ada_w.shape
    b = c.shape[0]
    tn = n // 4
    return pl.pallas_call(
        _ada_kernel,
        grid=(depth, n // tn),
        in_specs=[
            pl.BlockSpec((b, d), lambda l, j: (0, 0)),
            pl.BlockSpec((1, d, tn), lambda l, j: (l, 0, j)),
            pl.BlockSpec((1, 1, tn), lambda l, j: (l, 0, j)),
        ],
        out_specs=pl.BlockSpec((1, b, tn), lambda l, j: (l, 0, j)),
        out_shape=jax.ShapeDtypeStruct((depth, b, n), F32),
        compiler_params=_cparams(2),
        name="ada_mod",
    )(c, ada_w, ada_b.reshape(depth, 1, n))


def _nsa_weight_layout(w_in):
    d = w_in.shape[0]
    kv = w_in[:, Q_DIM:Q_DIM + 6 * KV_DIM].reshape(d, N_BRANCH, 2, NSA_GROUPS, HEAD_DIM)
    kv = jnp.transpose(kv, (0, 1, 3, 2, 4)).reshape(d, 6 * KV_DIM)
    gates = jnp.pad(w_in[:, Q_DIM + 6 * KV_DIM:], ((0, 0), (0, LANES - GATE_COLS)))
    return jnp.concatenate([w_in[:, :Q_DIM], kv, gates], axis=1).astype(BF16)


def _nsa_proj_kernel(x_ref, g_ref, sc_ref, sh_ref, w_ref, q_ref, kv_ref, gate_ref):
    hh = _norm_mod(x_ref[0], g_ref[...], sc_ref[0], sh_ref[0])
    p = jnp.dot(hh.astype(BF16), w_ref[...], preferred_element_type=F32)
    q_ref[0] = p[:, :Q_DIM]
    for j in range(N_BRANCH):
        for g in range(NSA_GROUPS):
            lo = Q_DIM + (j * NSA_GROUPS + g) * LANES
            kv_ref[j, 0, g] = p[:, lo:lo + LANES]
    gate = jax.nn.sigmoid(p[:, Q_DIM + 6 * KV_DIM:])
    for g in range(NSA_GROUPS):
        shift = (LANES - GATE_PER_GROUP * g) % LANES
        gate_ref[0, g] = gate if shift == 0 else pltpu.roll(gate, shift, 1)


def _nsa_proj(x, gain, sc, sh, w):
    b, s, d = x.shape
    tm = min(ROW_TILE, s)
    n = w.shape[1]
    vec = pl.BlockSpec((1, 1, d), lambda bi, i: (bi, 0, 0))
    return pl.pallas_call(
        _nsa_proj_kernel,
        grid=(b, s // tm),
        in_specs=[
            pl.BlockSpec((1, tm, d), lambda bi, i: (bi, i, 0)),
            pl.BlockSpec((1, d), lambda bi, i: (0, 0)),
            vec, vec,
            pl.BlockSpec((d, n), lambda bi, i: (0, 0)),
        ],
        out_specs=[
            pl.BlockSpec((1, tm, Q_DIM), lambda bi, i: (bi, i, 0)),
            pl.BlockSpec((N_BRANCH, 1, NSA_GROUPS, tm, LANES), lambda bi, i: (0, bi, 0, i, 0)),
            pl.BlockSpec((1, NSA_GROUPS, tm, LANES), lambda bi, i: (bi, 0, i, 0)),
        ],
        out_shape=[
            jax.ShapeDtypeStruct((b, s, Q_DIM), F32),
            jax.ShapeDtypeStruct((N_BRANCH, b, NSA_GROUPS, s, LANES), F32),
            jax.ShapeDtypeStruct((b, NSA_GROUPS, s, LANES), F32),
        ],
        compiler_params=_cparams(2),
        name="nsa_proj",
    )(x, gain, sc, sh, w)


def _k_lane_norm(t, gain_ext):
    lane = lax.broadcasted_iota(jnp.int32, t.shape, 1)
    ms = jnp.sum(jnp.where(lane < HEAD_DIM, t * t, 0.0), axis=-1, keepdims=True) * (1.0 / HEAD_DIM)
    return t * lax.rsqrt(ms + EPS) * gain_ext


def _compress_weights(pos, w1, w2):
    hidden = w1.shape[-1]
    w1p = w1.reshape(2, 2, CMP_STRIDE, HEAD_DIM, hidden)
    z = jnp.zeros_like(w1p[0])
    wk = jnp.concatenate([w1p[0], z], axis=-1)
    wv = jnp.concatenate([z, w1p[1]], axis=-1)
    w1_blk = jnp.concatenate([wk, wv], axis=2).astype(BF16)
    pos_blk = jnp.concatenate([pos[0], pos[1]], axis=-1).reshape(2, CMP_STRIDE, 1, LANES)
    z2 = jnp.zeros_like(w2[0])
    w2_blk = jnp.concatenate([jnp.concatenate([w2[0], z2], axis=1),
                              jnp.concatenate([z2, w2[1]], axis=1)], axis=0).astype(BF16)
    return pos_blk, w1_blk, w2_blk


def _compress_kernel(kv_ref, pos_ref, w1_ref, w2_ref, kg_ref, o_ref):
    g, seq = kv_ref.shape[2:4]
    nch = seq // CMP_STRIDE
    rows = g * nch
    first = None
    second = None
    for p in range(CMP_STRIDE):
        tok = jnp.concatenate([kv_ref[0, 0, gi, pl.ds(p, nch, stride=CMP_STRIDE), :] for gi in range(g)], axis=0)
        fa = jnp.dot((tok + pos_ref[0, p]).astype(BF16), w1_ref[0, p], preferred_element_type=F32)
        fb = jnp.dot((tok + pos_ref[1, p]).astype(BF16), w1_ref[1, p], preferred_element_type=F32)
        first = fa if first is None else first + fa
        second = fb if second is None else second + fb
    h1 = first + pltpu.roll(second, rows - 1, 0)
    out = jnp.dot(jax.nn.silu(h1).astype(BF16), w2_ref[...], preferred_element_type=F32)
    lane = lax.broadcasted_iota(jnp.int32, out.shape, 1)
    out = jnp.where(lane < HEAD_DIM, _k_lane_norm(out, kg_ref[0:1]), out)
    r = lax.broadcasted_iota(jnp.int32, out.shape, 0)
    out = jnp.where((r & (nch - 1)) == nch - 1, 0.0, out)
    o_ref[0] = out.reshape(g, nch, LANES)


def _compress(kv, pos_blk, w1_blk, w2_blk, kg_ext):
    _, b, g, s, _ = kv.shape
    nch = s // CMP_STRIDE
    assert nch & (nch - 1) == 0
    return pl.pallas_call(
        _compress_kernel,
        grid=(b,),
        in_specs=[
            pl.BlockSpec((1, 1, g, s, LANES), lambda bi: (0, bi, 0, 0, 0)),
            pl.BlockSpec(pos_blk.shape, lambda bi: (0, 0, 0, 0)),
            pl.BlockSpec(w1_blk.shape, lambda bi: (0, 0, 0, 0)),
            pl.BlockSpec(w2_blk.shape, lambda bi: (0, 0)),
            pl.BlockSpec((N_BRANCH, LANES), lambda bi: (0, 0)),
        ],
        out_specs=pl.BlockSpec((1, g, nch, LANES), lambda bi: (bi, 0, 0, 0)),
        out_shape=jax.ShapeDtypeStruct((b, g, nch, LANES), F32),
        compiler_params=_cparams(1),
        name="nsa_compress",
    )(kv, pos_blk, w1_blk, w2_blk, kg_ext)


def _tile_scores(k_tile, q_t):
    return tuple(jnp.dot(k_tile, q_t[:, h:h + SCORE_LANES], preferred_element_type=F32)
                 for h in range(0, q_t.shape[1], SCORE_LANES))


def _tile_softmax(scores, bias, vt_tile):
    reps = SCORE_LANES // Q_BLOCK
    bias = jnp.concatenate([bias] * reps, axis=1)
    ms, ls, os_ = [], [], []
    for s in scores:
        s = s + bias
        m = jnp.max(s, axis=0, keepdims=True)
        p = jnp.exp2(s - m)
        ms.append(m)
        ls.append(jnp.sum(p, axis=0, keepdims=True))
        os_.append(jnp.dot(vt_tile, p.astype(BF16), preferred_element_type=F32))
    return jnp.concatenate(ms, axis=1), jnp.concatenate(ls, axis=1), jnp.concatenate(os_, axis=1)


def _merge_tiles(parts):
    m = parts[0][0]
    for mi, _, _ in parts[1:]:
        m = jnp.maximum(m, mi)
    l = None
    o = None
    for mi, li, oi in parts:
        w = jnp.exp2(mi - m)
        l = w * li if l is None else l + w * li
        o = w * oi if o is None else o + w * oi
    return o / l


def _attn_kernel(q_ref, gate_ref, kvc_ref, ks_ref, kw_ref, selmap_ref, qg_ref, kg_ref, o_ref,
                 ksn, kwn, vst, vwt, selb, qt_scr, slot_m, slot_l, slot_o):
    pair = pl.program_id(2)
    seq = ks_ref.shape[3]
    ncmp = kvc_ref.shape[2]
    nsel = selmap_ref.shape[0]
    nqb = seq // Q_BLOCK
    cols = NSA_HPG * Q_BLOCK
    win_tiles = WINDOW // LANES
    chunk_tiles = SEL_CHUNK // LANES
    blocks_per_chunk = SEL_CHUNK // SEL_BLOCK
    scale = HEAD_DIM ** -0.5 * LOG2_E

    @pl.when(pair == 0)
    def _prepare_kv():
        kwn[0:WINDOW, :] = jnp.zeros((WINDOW, LANES), BF16)
        vwt[0:win_tiles] = jnp.zeros((win_tiles, HEAD_DIM, LANES), BF16)

        def body(c, carry):
            r0 = pl.multiple_of(c * LANES, LANES)
            ts = ks_ref[0, 0, 0, pl.ds(r0, LANES), :]
            ksn[pl.ds(r0, LANES), :] = _k_lane_norm(ts, kg_ref[1:2]).astype(BF16)
            vst[c] = ts.T[HEAD_DIM:, :].astype(BF16)
            tw = kw_ref[0, 0, 0, pl.ds(r0, LANES), :]
            kwn[pl.ds(WINDOW + r0, LANES), :] = _k_lane_norm(tw, kg_ref[2:3]).astype(BF16)
            vwt[win_tiles + c] = tw.T[HEAD_DIM:, :].astype(BF16)
            return carry

        lax.fori_loop(0, seq // LANES, body, 0)

    kc = kvc_ref[0, 0]
    kc_b = kc.astype(BF16)
    vc_t = kc.T[HEAD_DIM:, :].astype(BF16)
    block_ids = (pair, nqb - 1 - pair)

    def block_branches(idx):
        i = block_ids[idx]
        t0 = i * Q_BLOCK
        row0 = pl.multiple_of(t0, Q_BLOCK)

        q = q_ref[0, pl.ds(row0, Q_BLOCK), :]
        heads = []
        for h in range(NSA_HPG // 2):
            qt = q[:, h * LANES:(h + 1) * LANES].T
            for a in range(2):
                x = qt[a * HEAD_DIM:(a + 1) * HEAD_DIM]
                ms = jnp.mean(x * x, axis=0, keepdims=True)
                heads.append(x * lax.rsqrt(ms + EPS) * qg_ref[...] * scale)
        q_t = jnp.concatenate(heads, axis=1)
        q_t = jnp.concatenate([q_t, jnp.zeros_like(q_t)], axis=0).astype(BF16)
        qt_scr[idx] = q_t

        s = jnp.dot(kc_b, q_t, preferred_element_type=F32)
        tq = t0 + (lax.broadcasted_iota(jnp.int32, (ncmp, cols), 1) & (Q_BLOCK - 1))
        cmp_end = lax.broadcasted_iota(jnp.int32, (ncmp, cols), 0) * CMP_STRIDE + (CMP_BLOCK - 1)
        visible = cmp_end <= tq
        s = jnp.where(visible, s, NEG_BIG)
        m = jnp.max(s, axis=0, keepdims=True)
        p = jnp.where(visible, jnp.exp2(s - m), 0.0)
        l = jnp.sum(p, axis=0, keepdims=True)
        p_c = p / jnp.maximum(l, jnp.finfo(F32).tiny)
        o_c = jnp.dot(vc_t, p_c.astype(BF16), preferred_element_type=F32)

        p_sum = p_c[:, 0:Q_BLOCK]
        for n in range(1, NSA_HPG):
            p_sum = p_sum + p_c[:, n * Q_BLOCK:(n + 1) * Q_BLOCK]
        imp = jnp.dot(selmap_ref[...], p_sum, preferred_element_type=F32, precision=lax.Precision.HIGHEST)
        blk = lax.broadcasted_iota(jnp.int32, (nsel, Q_BLOCK), 0)
        cur = lax.shift_right_logical(t0 + lax.broadcasted_iota(jnp.int32, (nsel, Q_BLOCK), 1), 6)
        valid = blk <= cur
        forced = (blk == 0) | (blk == cur) | (blk == cur - 1)
        val = jnp.where(valid, imp, -jnp.inf)
        val = jnp.where(forced, jnp.inf, val)
        rank = jnp.zeros((nsel, Q_BLOCK), jnp.int32)
        for j2 in range(nsel):
            row = val[j2:j2 + 1, :]
            beats = (row > val) | ((row == val) & (blk > j2))
            rank = rank + beats.astype(jnp.int32)
        selb[idx] = jnp.where((rank < SEL_TOP_N) & valid, 0.0, NEG_BIG)

        tiles = []
        off = 0
        span = WINDOW + Q_BLOCK
        while off < span:
            kc_w = min(SEL_CHUNK, span - off)

            def scores(off=off, kc_w=kc_w):
                k_tile = kwn[pl.ds(pl.multiple_of(t0 + off, LANES), kc_w), :]
                return _tile_scores(k_tile, q_t)

            def finish(s, off=off, kc_w=kc_w):
                r = off + lax.broadcasted_iota(jnp.int32, (kc_w, Q_BLOCK), 0)
                c = lax.broadcasted_iota(jnp.int32, (kc_w, Q_BLOCK), 1)
                in_win = (r - WINDOW <= c) & (r > c) & (t0 + r >= WINDOW)
                vt_tile = jnp.concatenate([vwt[i + off // LANES + cc] for cc in range(kc_w // LANES)], axis=1)
                return _tile_softmax(s, jnp.where(in_win, 0.0, NEG_BIG), vt_tile)

            tiles.append((scores, finish))
            off += kc_w
        gate_t = gate_ref[0, 0, pl.ds(row0, Q_BLOCK), :].T
        return o_c, gate_t, tiles

    per_block = [block_branches(idx) for idx in range(2)]
    n_win = len(per_block[0][2])

    n_slots = nqb // 2 + 1
    n0 = lax.shift_right_logical(block_ids[0] * Q_BLOCK + Q_BLOCK + SEL_CHUNK - 1, SEL_CHUNK.bit_length() - 1)
    n0_max = (nqb // 2 * Q_BLOCK + SEL_CHUNK - 1) // SEL_CHUNK
    key_minus_query = (lax.broadcasted_iota(jnp.int32, (SEL_CHUNK, Q_BLOCK), 0)
                       - lax.broadcasted_iota(jnp.int32, (SEL_CHUNK, Q_BLOCK), 1))
    sel_tiles = []
    for slot in range(n_slots):
        if slot == 0:
            which = 0
        elif slot >= n0_max:
            which = 1
        else:
            which = (slot >= n0).astype(jnp.int32)
        kb = slot - which * n0
        t0 = (block_ids[0] + which * (block_ids[1] - block_ids[0])) * Q_BLOCK
        k0 = pl.multiple_of(kb * SEL_CHUNK, SEL_CHUNK)

        def scores(which=which, k0=k0):
            return _tile_scores(ksn[pl.ds(k0, SEL_CHUNK), :], qt_scr[which])

        def finish(s, which=which, kb=kb, k0=k0, t0=t0):
            picked = jnp.concatenate(
                [jnp.broadcast_to(selb[which, pl.ds(kb * blocks_per_chunk + jj, 1), :], (SEL_BLOCK, Q_BLOCK))
                 for jj in range(blocks_per_chunk)], axis=0)
            bias = jnp.where(key_minus_query <= t0 - k0, picked, NEG_BIG)
            vt_tile = jnp.concatenate([vst[kb * chunk_tiles + c] for c in range(chunk_tiles)], axis=1)
            return _tile_softmax(s, bias, vt_tile)

        sel_tiles.append((scores, finish))

    all_tiles = per_block[0][2] + per_block[1][2] + sel_tiles
    results = []
    pending = all_tiles[0][0]()
    for t, (_, finish) in enumerate(all_tiles):
        s = pending
        if t + 1 < len(all_tiles):
            pending = all_tiles[t + 1][0]()
        results.append(finish(s))
    for slot in range(n_slots):
        m, l, o = results[2 * n_win + slot]
        slot_m[slot] = m
        slot_l[slot] = l
        slot_o[slot] = o

    for idx in range(2):
        o_c, gate_t, _ = per_block[idx]
        o_w = _merge_tiles(results[idx * n_win:(idx + 1) * n_win])
        slots = range(0, n0_max) if idx == 0 else range(1, n_slots)
        parts = []
        for slot in slots:
            if slot == 0 or slot >= n0_max:
                m = slot_m[slot]
            else:
                mine = (slot < n0) if idx == 0 else (slot >= n0)
                m = jnp.where(mine, slot_m[slot], NEG_BIG)
            parts.append((m, slot_l[slot], slot_o[slot]))
        o_s = _merge_tiles(parts)
        row0 = pl.multiple_of(block_ids[idx] * Q_BLOCK, Q_BLOCK)
        for hh in range(NSA_HPG // 2):
            tiles = []
            for a in range(2):
                n = 2 * hh + a
                lanes = slice(n * Q_BLOCK, (n + 1) * Q_BLOCK)
                c0 = N_BRANCH * n
                tiles.append(gate_t[c0:c0 + 1, :] * o_c[:, lanes]
                             + gate_t[c0 + 1:c0 + 2, :] * o_s[:, lanes]
                             + gate_t[c0 + 2:c0 + 3, :] * o_w[:, lanes])
            o_ref[0, pl.ds(row0, Q_BLOCK), hh * LANES:(hh + 1) * LANES] = jnp.concatenate(tiles, axis=0).T


def _sel_map_t(seq):
    n_c = seq // CMP_STRIDE - 1
    n_s = seq // SEL_BLOCK
    tok = np.arange(seq)
    start = np.arange(n_c) * CMP_STRIDE
    cover_c = (tok[None, :] >= start[:, None]) & (tok[None, :] < start[:, None] + CMP_BLOCK)
    cover_s = (tok[:, None] // SEL_BLOCK) == np.arange(n_s)[None, :]
    m = cover_c.astype(np.float32) @ cover_s.astype(np.float32) / np.float32(CMP_BLOCK)
    out = np.zeros((n_s, n_c + 1), np.float32)
    out[:, :n_c] = m.T
    return jnp.asarray(out)


def _attention(q, gates, kvc, kv, qg_t, kg_ext):
    b, s, _ = q.shape
    g = NSA_GROUPS
    nqb = s // Q_BLOCK
    ncmp = s // CMP_STRIDE
    nsel = s // SEL_BLOCK
    assert s % SEL_CHUNK == 0 and nsel % SUBLANES == 0 and nqb % 2 == 0
    gw = NSA_HPG * HEAD_DIM
    cols = NSA_HPG * Q_BLOCK
    n_slots = nqb // 2 + 1

    def kv_spec(j):
        return pl.BlockSpec((1, 1, 1, s, LANES), lambda bi, gi, i: (j, bi, gi, 0, 0))

    return pl.pallas_call(
        _attn_kernel,
        grid=(b, g, nqb // 2),
        in_specs=[
            pl.BlockSpec((1, s, gw), lambda bi, gi, i: (bi, 0, gi)),
            pl.BlockSpec((1, 1, s, LANES), lambda bi, gi, i: (bi, gi, 0, 0)),
            pl.BlockSpec((1, 1, ncmp, LANES), lambda bi, gi, i: (bi, gi, 0, 0)),
            kv_spec(1), kv_spec(2),
            pl.BlockSpec((nsel, ncmp), lambda bi, gi, i: (0, 0)),
            pl.BlockSpec((HEAD_DIM, Q_BLOCK), lambda bi, gi, i: (0, 0)),
            pl.BlockSpec((N_BRANCH, LANES), lambda bi, gi, i: (0, 0)),
        ],
        out_specs=pl.BlockSpec((1, s, gw), lambda bi, gi, i: (bi, 0, gi)),
        out_shape=jax.ShapeDtypeStruct((b, s, Q_DIM), F32),
        scratch_shapes=[
            pltpu.VMEM((s, LANES), BF16),
            pltpu.VMEM((WINDOW + s, LANES), BF16),
            pltpu.VMEM((s // LANES, HEAD_DIM, LANES), BF16),
            pltpu.VMEM(((WINDOW + s) // LANES, HEAD_DIM, LANES), BF16),
            pltpu.VMEM((2, nsel, Q_BLOCK), F32),
            pltpu.VMEM((2, LANES, cols), BF16),
            pltpu.VMEM((n_slots, 1, cols), F32),
            pltpu.VMEM((n_slots, 1, cols), F32),
            pltpu.VMEM((n_slots, HEAD_DIM, cols), F32),
        ],
        compiler_params=_cparams(3),
        name="nsa_attention",
    )(q, gates, kvc, kv, kv, _sel_map_t(s), qg_t, kg_ext)


def _out_proj_kernel(y_ref, x_ref, gate_ref, w_ref, o_ref):
    mix = jnp.dot(y_ref[0].astype(BF16), w_ref[...], preferred_element_type=F32)
    o_ref[0] = x_ref[0] + gate_ref[0] * mix


def _out_proj(y, x, gate, w):
    b, s, d = x.shape
    k = y.shape[-1]
    tm = min(ROW_TILE, s)
    return pl.pallas_call(
        _out_proj_kernel,
        grid=(b, s // tm),
        in_specs=[
            pl.BlockSpec((1, tm, k), lambda bi, i: (bi, i, 0)),
            pl.BlockSpec((1, tm, d), lambda bi, i: (bi, i, 0)),
            pl.BlockSpec((1, 1, d), lambda bi, i: (bi, 0, 0)),
            pl.BlockSpec((k, d), lambda bi, i: (0, 0)),
        ],
        out_specs=pl.BlockSpec((1, tm, d), lambda bi, i: (bi, i, 0)),
        out_shape=jax.ShapeDtypeStruct((b, s, d), F32),
        compiler_params=_cparams(2),
        name="out_proj",
    )(y, x, gate, w)


def _ffn_kernel(x_ref, g_ref, sc_ref, sh_ref, gate_ref, wg_ref, wu_ref, wo_ref, o_ref):
    x = x_ref[0]
    hh = _norm_mod(x, g_ref[...], sc_ref[0], sh_ref[0]).astype(BF16)
    a = jnp.dot(hh, wg_ref[...], preferred_element_type=F32)
    u = jnp.dot(hh, wu_ref[...], preferred_element_type=F32)
    act = (jax.nn.silu(a) * u).astype(BF16)
    o_ref[0] = x + gate_ref[0] * jnp.dot(act, wo_ref[...], preferred_element_type=F32)


def _ffn(x, gain, sc, sh, gate, w_in, w_out):
    b, s, d = x.shape
    hidden = w_out.shape[0]
    tm = min(ROW_TILE, s)
    vec = pl.BlockSpec((1, 1, d), lambda bi, i: (bi, 0, 0))
    return pl.pallas_call(
        _ffn_kernel,
        grid=(b, s // tm),
        in_specs=[
            pl.BlockSpec((1, tm, d), lambda bi, i: (bi, i, 0)),
            pl.BlockSpec((1, d), lambda bi, i: (0, 0)),
            vec, vec, vec,
            pl.BlockSpec((d, hidden), lambda bi, i: (0, 0)),
            pl.BlockSpec((d, hidden), lambda bi, i: (0, 1)),
            pl.BlockSpec((hidden, d), lambda bi, i: (0, 0)),
        ],
        out_specs=pl.BlockSpec((1, tm, d), lambda bi, i: (bi, i, 0)),
        out_shape=jax.ShapeDtypeStruct((b, s, d), F32),
        compiler_params=_cparams(2),
        name="ffn",
    )(x, gain, sc, sh, gate, w_in, w_in, w_out)


def _rg_proj_kernel(x_ref, g_ref, sc_ref, sh_ref, w_ref, gate_ref, u_ref):
    hh = _norm_mod(x_ref[0], g_ref[...], sc_ref[0], sh_ref[0])
    p = jnp.dot(hh.astype(BF16), w_ref[...], preferred_element_type=F32)
    r = gate_ref.shape[-1]
    gate_ref[0] = p[:, :r]
    u_ref[0] = p[:, r:]


def _rg_proj(x, gain, sc, sh, w):
    b, s, d = x.shape
    r = w.shape[1] // 2
    tm = min(ROW_TILE, s)
    vec = pl.BlockSpec((1, 1, d), lambda bi, i: (bi, 0, 0))
    out = pl.BlockSpec((1, tm, r), lambda bi, i: (bi, i, 0))
    return pl.pallas_call(
        _rg_proj_kernel,
        grid=(b, s // tm),
        in_specs=[
            pl.BlockSpec((1, tm, d), lambda bi, i: (bi, i, 0)),
            pl.BlockSpec((1, d), lambda bi, i: (0, 0)),
            vec, vec,
            pl.BlockSpec((d, 2 * r), lambda bi, i: (0, 0)),
        ],
        out_specs=[out, out],
        out_shape=[jax.ShapeDtypeStruct((b, s, r), F32)] * 2,
        compiler_params=_cparams(2),
        name="rg_proj",
    )(x, gain, sc, sh, w)


def _band_starts(width, block_w):
    band = 2 * MXU_WIDTH
    starts = []
    for c0 in range(0, width, MXU_WIDTH):
        c1 = min(c0 + MXU_WIDTH, width)
        lo = (c0 // block_w) * block_w
        hi = ((c1 - 1) // block_w + 1) * block_w
        k0 = min((lo // LANES) * LANES, width - band)
        assert k0 >= 0 and k0 <= lo and hi <= k0 + band
        starts.append(k0)
    return starts


def _band_weights(w_a, w_x):
    nb, bw, _ = w_a.shape
    width = nb * bw
    eye = jnp.eye(nb, dtype=w_a.dtype)
    dense = lambda w: jnp.einsum('nde,nm->ndme', w, eye).reshape(width, width)
    da, dx = dense(w_a), dense(w_x)
    band = 2 * MXU_WIDTH
    tiles = []
    for t, k0 in enumerate(_band_starts(width, bw)):
        c0 = t * MXU_WIDTH
        c1 = min(c0 + MXU_WIDTH, width)
        pad = ((0, 0), (0, MXU_WIDTH - (c1 - c0)))
        tiles.append(jnp.concatenate(
            [jnp.pad(da[k0:k0 + band, c0:c1], pad), jnp.pad(dx[k0:k0 + band, c0:c1], pad)], axis=1))
    return jnp.stack(tiles).astype(BF16)


def _rg_scan_kernel(u_ref, gate_ref, cw_ref, cb_ref, wb_ref, ba_ref, bx_ref, lam_ref, y_ref,
                    u_halo, h_scr, a_scr, b_scr, hs_scr, *, band_starts, block_w):
    step = pl.program_id(0)
    nb, ts, width = u_ref.shape
    rows = nb * ts
    halo = SUBLANES
    band = 2 * MXU_WIDTH

    @pl.when(step == 0)
    def _init():
        u_halo[:, 0:halo, :] = jnp.zeros((nb, halo, width), F32)
        h_scr[...] = jnp.zeros(h_scr.shape, F32)

    u_halo[:, halo:, :] = u_ref[...]
    uc = jnp.zeros((nb, ts, width), F32) + cb_ref[...]
    for k in range(CONV_WIDTH):
        off = halo - (CONV_WIDTH - 1) + k
        uc = uc + cw_ref[k:k + 1, :] * u_halo[:, off:off + ts, :]
    u_halo[:, 0:halo, :] = u_halo[:, ts:ts + halo, :]
    uc = uc.reshape(rows, width)
    ub = uc.astype(BF16)

    z = -lam_ref[...]
    log_a_unit = -RG_C * (jnp.maximum(z, 0.0) + jnp.log1p(jnp.exp(-jnp.abs(z))))

    for t, k0 in enumerate(band_starts):
        c0 = t * MXU_WIDTH
        c1 = min(c0 + MXU_WIDTH, width)
        n = c1 - c0
        zz = jnp.dot(ub[:, k0:k0 + band], wb_ref[t], preferred_element_type=F32)
        r = jax.nn.sigmoid(zz[:, :n] + ba_ref[:, c0:c1])
        ig = jax.nn.sigmoid(zz[:, MXU_WIDTH:MXU_WIDTH + n] + bx_ref[:, c0:c1])
        log_a = log_a_unit[:, c0:c1] * r
        a = jnp.exp(log_a)
        bb = jnp.sqrt(-jnp.tanh(log_a) * (a * a + 1.0)) * (ig * uc[:, c0:c1])
        for ct in range(n // LANES):
            lanes = slice(ct * LANES, (ct + 1) * LANES)
            a_scr[c0 // LANES + ct] = a[:, lanes]
            b_scr[c0 // LANES + ct] = bb[:, lanes]

    n_ct = width // LANES

    def scan_step(t, h):
        rows_t = pl.ds(t, nb, stride=ts)
        new = []
        for ct in range(n_ct):
            hc = a_scr[ct, rows_t, :] * h[ct] + b_scr[ct, rows_t, :]
            hs_scr[ct, rows_t, :] = hc
            new.append(hc)
        return tuple(new)

    h_fin = lax.fori_loop(0, ts, scan_step, tuple(h_scr[ct] for ct in range(n_ct)))
    gate = gate_ref[...].reshape(rows, width)
    for ct in range(n_ct):
        lanes = slice(ct * LANES, (ct + 1) * LANES)
        h_scr[ct] = h_fin[ct]
        y_ref[:, :, lanes] = (hs_scr[ct] * jax.nn.gelu(gate[:, lanes])).reshape(nb, ts, LANES)


def _rg_scan(u, gate, conv_w, conv_b, w_band, b_a, b_x, lam, block_w):
    b, s, width = u.shape
    assert b == SUBLANES
    ts = min(RG_TIME_TILE, s)
    rows = b * ts
    band_starts = tuple(_band_starts(width, block_w))
    tile = pl.BlockSpec((b, ts, width), lambda t: (0, t, 0))
    vec = pl.BlockSpec((1, width), lambda t: (0, 0))
    kern = functools.partial(_rg_scan_kernel, band_starts=band_starts, block_w=block_w)
    return pl.pallas_call(
        kern,
        grid=(s // ts,),
        in_specs=[
            tile, tile,
            pl.BlockSpec((CONV_WIDTH, width), lambda t: (0, 0)),
            vec,
            pl.BlockSpec(w_band.shape, lambda t: (0, 0, 0)),
            vec, vec, vec,
        ],
        out_specs=tile,
        out_shape=jax.ShapeDtypeStruct((b, s, width), F32),
        scratch_shapes=[
            pltpu.VMEM((b, ts + SUBLANES, width), F32),
            pltpu.VMEM((width // LANES, b, LANES), F32),
            pltpu.VMEM((width // LANES, rows, LANES), F32),
            pltpu.VMEM((width // LANES, rows, LANES), F32),
            pltpu.VMEM((width // LANES, rows, LANES), F32),
        ],
        compiler_params=_cparams(1),
        name="rg_scan",
    )(u, gate, conv_w, conv_b, w_band, b_a, b_x, lam)


def kernel(x, c, ada_w, ada_b, norm1_g, norm2_g, nsa_w_in, nsa_w_out, nsa_cmp_pos, nsa_cmp_w1, nsa_cmp_w2, nsa_q_gain, nsa_k_gain, rg_w_in, rg_conv_w, rg_conv_b, rg_w_a, rg_b_a, rg_w_x, rg_b_x, rg_lam, rg_w_out, ffn_w_in, ffn_w_out):
    depth, d, _ = ada_w.shape
    b = x.shape[0]
    mod = _ada_mod(c, ada_w, ada_b).reshape(depth, b, 6, 1, d)
    for layer in range(depth):
        sh1, sc1, g1, sh2, sc2, g2 = (mod[layer, :, k] for k in range(6))
        n1 = norm1_g[layer].reshape(1, d)
        n2 = norm2_g[layer].reshape(1, d)
        j = layer // N_MIXERS
        if layer % N_MIXERS == 0:
            kg_ext = jnp.concatenate([nsa_k_gain[j], jnp.zeros_like(nsa_k_gain[j])], axis=1)
            qg_t = jnp.broadcast_to(nsa_q_gain[j][:, None], (HEAD_DIM, Q_BLOCK))
            q, kv, gates = _nsa_proj(x, n1, sc1, sh1, _nsa_weight_layout(nsa_w_in[j]))
            kvc = _compress(kv, *_compress_weights(nsa_cmp_pos[j], nsa_cmp_w1[j], nsa_cmp_w2[j]), kg_ext)
            o = _attention(q, gates, kvc, kv, qg_t, kg_ext)
            x = _out_proj(o, x, g1, nsa_w_out[j].astype(BF16))
        else:
            width = rg_w_out.shape[1]
            block_w = rg_w_a.shape[-1]
            gate, u = _rg_proj(x, n1, sc1, sh1, rg_w_in[j].astype(BF16))
            y = _rg_scan(u, gate, rg_conv_w[j], rg_conv_b[j].reshape(1, width),
                         _band_weights(rg_w_a[j], rg_w_x[j]),
                         rg_b_a[j].reshape(1, width), rg_b_x[j].reshape(1, width),
                         rg_lam[j].reshape(1, width), block_w)
            x = _out_proj(y, x, g1, rg_w_out[j].astype(BF16))
        x = _ffn(x, n2, sc2, sh2, g2, ffn_w_in[layer].astype(BF16), ffn_w_out[layer].astype(BF16))
    return x
```

```python
import functools

import numpy as np
import jax
import jax.numpy as jnp
from jax import lax
from jax.experimental import pallas as pl
from jax.experimental.pallas import tpu as pltpu

F32 = jnp.float32
BF16 = jnp.bfloat16

EPS = 1e-6
N_MIXERS = 2

NSA_HEADS = 16
NSA_GROUPS = 4
NSA_HPG = NSA_HEADS // NSA_GROUPS
HEAD_DIM = 64
CMP_STRIDE = 16
CMP_BLOCK = 2 * CMP_STRIDE
SEL_BLOCK = 64
SEL_TOP_N = 8
WINDOW = 512
Q_BLOCK = 128
N_BRANCH = 3
Q_DIM = NSA_HEADS * HEAD_DIM
KV_DIM = NSA_GROUPS * HEAD_DIM
GATE_COLS = N_BRANCH * NSA_HEADS
GATE_PER_GROUP = N_BRANCH * NSA_HPG

RNN_BLOCKS = 16
CONV_WIDTH = 4
RG_C = 8.0

LANES = 128
SUBLANES = 8
MXU_WIDTH = 256
VMEM_LIMIT_BYTES = 56 * 1024 * 1024

NEG_BIG = -1e30
LOG2_E = 1.4426950408889634
SEL_CHUNK = 256
RG_TIME_TILE = 64
ROW_TILE = 512


def _cparams(n_axes, flags=None):
    return pltpu.CompilerParams(
        dimension_semantics=("arbitrary",) * n_axes,
        vmem_limit_bytes=VMEM_LIMIT_BYTES,
        flags=flags,
    )


def _norm_mod(x, gain, scale, shift):
    ms = jnp.mean(x * x, axis=-1, keepdims=True)
    y = x * lax.rsqrt(ms + EPS) * gain
    return y * (1.0 + scale) + shift


def _ada_kernel(c_ref, w_ref, b_ref, o_ref):
    cond = jax.nn.silu(c_ref[...])
    o_ref[0] = jnp.dot(cond, w_ref[0], preferred_element_type=F32,
                       precision=lax.Precision.HIGHEST) + b_ref[0]


def _ada_mod(c, ada_w, ada_b):
    depth, d, n = ada_w.shape
    b = c.shape[0]
    tn = n // 4
    return pl.pallas_call(
        _ada_kernel,
        grid=(depth, n // tn),
        in_specs=[
            pl.BlockSpec((b, d), lambda l, j: (0, 0)),
            pl.BlockSpec((1, d, tn), lambda l, j: (l, 0, j)),
            pl.BlockSpec((1, 1, tn), lambda l, j: (l, 0, j)),
        ],
        out_specs=pl.BlockSpec((1, b, tn), lambda l, j: (l, 0, j)),
        out_shape=jax.ShapeDtypeStruct((depth, b, n), F32),
        compiler_params=_cparams(2),
        name="ada_mod",
    )(c, ada_w, ada_b.reshape(depth, 1, n))


def _nsa_weight_layout(w_in):
    d = w_in.shape[0]
    kv = w_in[:, Q_DIM:Q_DIM + 6 * KV_DIM].reshape(d, N_BRANCH, 2, NSA_GROUPS, HEAD_DIM)
    kv = jnp.transpose(kv, (0, 1, 3, 2, 4)).reshape(d, 6 * KV_DIM)
    gates = jnp.pad(w_in[:, Q_DIM + 6 * KV_DIM:], ((0, 0), (0, LANES - GATE_COLS)))
    return jnp.concatenate([w_in[:, :Q_DIM], kv, gates], axis=1).astype(BF16)


def _nsa_proj_kernel(x_ref, g_ref, sc_ref, sh_ref, w_ref, q_ref, kv_ref, gate_ref):
    hh = _norm_mod(x_ref[0], g_ref[...], sc_ref[0], sh_ref[0])
    p = jnp.dot(hh.astype(BF16), w_ref[...], preferred_element_type=F32)
    q_ref[0] = p[:, :Q_DIM]
    for j in range(N_BRANCH):
        for g in range(NSA_GROUPS):
            lo = Q_DIM + (j * NSA_GROUPS + g) * LANES
            kv_ref[j, 0, g] = p[:, lo:lo + LANES]
    gate = jax.nn.sigmoid(p[:, Q_DIM + 6 * KV_DIM:])
    for g in range(NSA_GROUPS):
        shift = (LANES - GATE_PER_GROUP * g) % LANES
        gate_ref[0, g] = gate if shift == 0 else pltpu.roll(gate, shift, 1)


def _nsa_proj(x, gain, sc, sh, w):
    b, s, d = x.shape
    tm = min(ROW_TILE, s)
    n = w.shape[1]
    vec = pl.BlockSpec((1, 1, d), lambda bi, i: (bi, 0, 0))
    return pl.pallas_call(
        _nsa_proj_kernel,
        grid=(b, s // tm),
        in_specs=[
            pl.BlockSpec((1, tm, d), lambda bi, i: (bi, i, 0)),
            pl.BlockSpec((1, d), lambda bi, i: (0, 0)),
            vec, vec,
            pl.BlockSpec((d, n), lambda bi, i: (0, 0)),
        ],
        out_specs=[
            pl.BlockSpec((1, tm, Q_DIM), lambda bi, i: (bi, i, 0)),
            pl.BlockSpec((N_BRANCH, 1, NSA_GROUPS, tm, LANES), lambda bi, i: (0, bi, 0, i, 0)),
            pl.BlockSpec((1, NSA_GROUPS, tm, LANES), lambda bi, i: (bi, 0, i, 0)),
        ],
        out_shape=[
            jax.ShapeDtypeStruct((b, s, Q_DIM), F32),
            jax.ShapeDtypeStruct((N_BRANCH, b, NSA_GROUPS, s, LANES), F32),
            jax.ShapeDtypeStruct((b, NSA_GROUPS, s, LANES), F32),
        ],
        compiler_params=_cparams(2),
        name="nsa_proj",
    )(x, gain, sc, sh, w)


def _k_lane_norm(t, gain_ext):
    lane = lax.broadcasted_iota(jnp.int32, t.shape, 1)
    ms = jnp.sum(jnp.where(lane < HEAD_DIM, t * t, 0.0), axis=-1, keepdims=True) * (1.0 / HEAD_DIM)
    return t * lax.rsqrt(ms + EPS) * gain_ext


def _compress_weights(pos, w1, w2):
    hidden = w1.shape[-1]
    w1p = w1.reshape(2, 2, CMP_STRIDE, HEAD_DIM, hidden)
    z = jnp.zeros_like(w1p[0])
    wk = jnp.concatenate([w1p[0], z], axis=-1)
    wv = jnp.concatenate([z, w1p[1]], axis=-1)
    w1_blk = jnp.concatenate([wk, wv], axis=2).astype(BF16)
    pos_blk = jnp.concatenate([pos[0], pos[1]], axis=-1).reshape(2, CMP_STRIDE, 1, LANES)
    z2 = jnp.zeros_like(w2[0])
    w2_blk = jnp.concatenate([jnp.concatenate([w2[0], z2], axis=1),
                              jnp.concatenate([z2, w2[1]], axis=1)], axis=0).astype(BF16)
    return pos_blk, w1_blk, w2_blk


def _compress_kernel(kv_ref, pos_ref, w1_ref, w2_ref, kg_ref, o_ref):
    g, seq = kv_ref.shape[2:4]
    nch = seq // CMP_STRIDE
    rows = g * nch
    first = None
    second = None
    for p in range(CMP_STRIDE):
        tok = jnp.concatenate([kv_ref[0, 0, gi, pl.ds(p, nch, stride=CMP_STRIDE), :] for gi in range(g)], axis=0)
        fa = jnp.dot((tok + pos_ref[0, p]).astype(BF16), w1_ref[0, p], preferred_element_type=F32)
        fb = jnp.dot((tok + pos_ref[1, p]).astype(BF16), w1_ref[1, p], preferred_element_type=F32)
        first = fa if first is None else first + fa
        second = fb if second is None else second + fb
    h1 = first + pltpu.roll(second, rows - 1, 0)
    out = jnp.dot(jax.nn.silu(h1).astype(BF16), w2_ref[...], preferred_element_type=F32)
    lane = lax.broadcasted_iota(jnp.int32, out.shape, 1)
    out = jnp.where(lane < HEAD_DIM, _k_lane_norm(out, kg_ref[0:1]), out)
    r = lax.broadcasted_iota(jnp.int32, out.shape, 0)
    out = jnp.where((r & (nch - 1)) == nch - 1, 0.0, out)
    o_ref[0] = out.reshape(g, nch, LANES)


def _compress(kv, pos_blk, w1_blk, w2_blk, kg_ext):
    _, b, g, s, _ = kv.shape
    nch = s // CMP_STRIDE
    assert nch & (nch - 1) == 0
    return pl.pallas_call(
        _compress_kernel,
        grid=(b,),
        in_specs=[
            pl.BlockSpec((1, 1, g, s, LANES), lambda bi: (0, bi, 0, 0, 0)),
            pl.BlockSpec(pos_blk.shape, lambda bi: (0, 0, 0, 0)),
            pl.BlockSpec(w1_blk.shape, lambda bi: (0, 0, 0, 0)),
            pl.BlockSpec(w2_blk.shape, lambda bi: (0, 0)),
            pl.BlockSpec((N_BRANCH, LANES), lambda bi: (0, 0)),
        ],
        out_specs=pl.BlockSpec((1, g, nch, LANES), lambda bi: (bi, 0, 0, 0)),
        out_shape=jax.ShapeDtypeStruct((b, g, nch, LANES), F32),
        compiler_params=_cparams(1),
        name="nsa_compress",
    )(kv, pos_blk, w1_blk, w2_blk, kg_ext)


def _tile_scores(k_tile, q_t):
    return jnp.concatenate([jnp.dot(k_tile, q_t[:, h:h + MXU_WIDTH], preferred_element_type=F32)
                            for h in range(0, q_t.shape[1], MXU_WIDTH)], axis=1)


def _tile_softmax(s, bias, vt_tile):
    s = s + jnp.concatenate([bias] * (s.shape[1] // Q_BLOCK), axis=1)
    m = jnp.max(s, axis=0, keepdims=True)
    p = jnp.exp2(s - m)
    l = jnp.sum(p, axis=0, keepdims=True)
    return m, l, jnp.dot(vt_tile, p.astype(BF16), preferred_element_type=F32)


def _merge_tiles(parts):
    m = parts[0][0]
    for mi, _, _ in parts[1:]:
        m = jnp.maximum(m, mi)
    l = None
    o = None
    for mi, li, oi in parts:
        w = jnp.exp2(mi - m)
        l = w * li if l is None else l + w * li
        o = w * oi if o is None else o + w * oi
    return o / l


def _attn_kernel(q_ref, gate_ref, kvc_ref, ks_ref, kw_ref, selmap_ref, qg_ref, kg_ref, o_ref,
                 ksn, kwn, vst, vwt, selb, qt_scr, slot_m, slot_l, slot_o):
    pair = pl.program_id(2)
    seq = ks_ref.shape[3]
    ncmp = kvc_ref.shape[2]
    nsel = selmap_ref.shape[0]
    nqb = seq // Q_BLOCK
    cols = NSA_HPG * Q_BLOCK
    win_tiles = WINDOW // LANES
    chunk_tiles = SEL_CHUNK // LANES
    blocks_per_chunk = SEL_CHUNK // SEL_BLOCK
    scale = HEAD_DIM ** -0.5 * LOG2_E

    @pl.when(pair == 0)
    def _prepare_kv():
        kwn[0:WINDOW, :] = jnp.zeros((WINDOW, LANES), BF16)
        vwt[0:win_tiles] = jnp.zeros((win_tiles, HEAD_DIM, LANES), BF16)

        def body(c, carry):
            r0 = pl.multiple_of(c * LANES, LANES)
            ts = ks_ref[0, 0, 0, pl.ds(r0, LANES), :]
            ksn[pl.ds(r0, LANES), :] = _k_lane_norm(ts, kg_ref[1:2]).astype(BF16)
            vst[c] = ts.T[HEAD_DIM:, :].astype(BF16)
            tw = kw_ref[0, 0, 0, pl.ds(r0, LANES), :]
            kwn[pl.ds(WINDOW + r0, LANES), :] = _k_lane_norm(tw, kg_ref[2:3]).astype(BF16)
            vwt[win_tiles + c] = tw.T[HEAD_DIM:, :].astype(BF16)
            return carry

        lax.fori_loop(0, seq // LANES, body, 0)

    kc = kvc_ref[0, 0]
    kc_b = kc.astype(BF16)
    vc_t = kc.T[HEAD_DIM:, :].astype(BF16)
    block_ids = (pair, nqb - 1 - pair)

    def block_branches(idx):
        i = block_ids[idx]
        t0 = i * Q_BLOCK
        row0 = pl.multiple_of(t0, Q_BLOCK)

        q = q_ref[0, pl.ds(row0, Q_BLOCK), :]
        heads = []
        for h in range(NSA_HPG // 2):
            qt = q[:, h * LANES:(h + 1) * LANES].T
            for a in range(2):
                x = qt[a * HEAD_DIM:(a + 1) * HEAD_DIM]
                ms = jnp.mean(x * x, axis=0, keepdims=True)
                heads.append(x * lax.rsqrt(ms + EPS) * qg_ref[...] * scale)
        q_t = jnp.concatenate(heads, axis=1)
        q_t = jnp.concatenate([q_t, jnp.zeros_like(q_t)], axis=0).astype(BF16)
        qt_scr[idx] = q_t

        s = _tile_scores(kc_b, q_t)
        tq = t0 + (lax.broadcasted_iota(jnp.int32, (ncmp, cols), 1) & (Q_BLOCK - 1))
        cmp_end = lax.broadcasted_iota(jnp.int32, (ncmp, cols), 0) * CMP_STRIDE + (CMP_BLOCK - 1)
        visible = cmp_end <= tq
        s = jnp.where(visible, s, NEG_BIG)
        m = jnp.max(s, axis=0, keepdims=True)
        p = jnp.where(visible, jnp.exp2(s - m), 0.0)
        l = jnp.sum(p, axis=0, keepdims=True)
        p_c = p / jnp.maximum(l, jnp.finfo(F32).tiny)
        o_c = jnp.dot(vc_t, p_c.astype(BF16), preferred_element_type=F32)

        p_sum = p_c[:, 0:Q_BLOCK]
        for n in range(1, NSA_HPG):
            p_sum = p_sum + p_c[:, n * Q_BLOCK:(n + 1) * Q_BLOCK]
        imp = jnp.dot(selmap_ref[...], p_sum, preferred_element_type=F32, precision=lax.Precision.HIGHEST)
        blk = lax.broadcasted_iota(jnp.int32, (nsel, Q_BLOCK), 0)
        cur = lax.shift_right_logical(t0 + lax.broadcasted_iota(jnp.int32, (nsel, Q_BLOCK), 1), 6)
        valid = blk <= cur
        forced = (blk == 0) | (blk == cur) | (blk == cur - 1)
        val = jnp.where(valid, imp, -jnp.inf)
        val = jnp.where(forced, jnp.inf, val)
        rank = jnp.zeros((nsel, Q_BLOCK), jnp.int32)
        for j2 in range(nsel):
            row = val[j2:j2 + 1, :]
            beats = (row > val) | ((row == val) & (blk > j2))
            rank = rank + beats.astype(jnp.int32)
        selb[idx] = jnp.where((rank < SEL_TOP_N) & valid, 0.0, NEG_BIG)

        tiles = []
        off = 0
        span = WINDOW + Q_BLOCK
        while off < span:
            kc_w = min(SEL_CHUNK, span - off)

            def scores(off=off, kc_w=kc_w):
                k_tile = kwn[pl.ds(pl.multiple_of(t0 + off, LANES), kc_w), :]
                return _tile_scores(k_tile, q_t)

            def finish(s, off=off, kc_w=kc_w):
                r = off + lax.broadcasted_iota(jnp.int32, (kc_w, Q_BLOCK), 0)
                c = lax.broadcasted_iota(jnp.int32, (kc_w, Q_BLOCK), 1)
                in_win = (r - WINDOW <= c) & (r > c) & (t0 + r >= WINDOW)
                vt_tile = jnp.concatenate([vwt[i + off // LANES + cc] for cc in range(kc_w // LANES)], axis=1)
                return _tile_softmax(s, jnp.where(in_win, 0.0, NEG_BIG), vt_tile)

            tiles.append((scores, finish))
            off += kc_w
        gate_t = gate_ref[0, 0, pl.ds(row0, Q_BLOCK), :].T
        return o_c, gate_t, tiles

    per_block = [block_branches(idx) for idx in range(2)]
    n_win = len(per_block[0][2])

    n_slots = nqb // 2 + 1
    n0 = lax.shift_right_logical(block_ids[0] * Q_BLOCK + Q_BLOCK + SEL_CHUNK - 1, SEL_CHUNK.bit_length() - 1)
    n0_max = (nqb // 2 * Q_BLOCK + SEL_CHUNK - 1) // SEL_CHUNK
    key_minus_query = (lax.broadcasted_iota(jnp.int32, (SEL_CHUNK, Q_BLOCK), 0)
                       - lax.broadcasted_iota(jnp.int32, (SEL_CHUNK, Q_BLOCK), 1))
    sel_tiles = []
    for slot in range(n_slots):
        if slot == 0:
            which = 0
        elif slot >= n0_max:
            which = 1
        else:
            which = (slot >= n0).astype(jnp.int32)
        kb = slot - which * n0
        t0 = (block_ids[0] + which * (block_ids[1] - block_ids[0])) * Q_BLOCK
        k0 = pl.multiple_of(kb * SEL_CHUNK, SEL_CHUNK)

        def scores(which=which, k0=k0):
            return _tile_scores(ksn[pl.ds(k0, SEL_CHUNK), :], qt_scr[which])

        def finish(s, which=which, kb=kb, k0=k0, t0=t0):
            picked = jnp.concatenate(
                [jnp.broadcast_to(selb[which, pl.ds(kb * blocks_per_chunk + jj, 1), :], (SEL_BLOCK, Q_BLOCK))
                 for jj in range(blocks_per_chunk)], axis=0)
            bias = jnp.where(key_minus_query <= t0 - k0, picked, NEG_BIG)
            vt_tile = jnp.concatenate([vst[kb * chunk_tiles + c] for c in range(chunk_tiles)], axis=1)
            return _tile_softmax(s, bias, vt_tile)

        sel_tiles.append((scores, finish))

    all_tiles = per_block[0][2] + per_block[1][2] + sel_tiles
    results = []
    pending = all_tiles[0][0]()
    for t, (_, finish) in enumerate(all_tiles):
        s = pending
        if t + 1 < len(all_tiles):
            pending = all_tiles[t + 1][0]()
        results.append(finish(s))
    for slot in range(n_slots):
        m, l, o = results[2 * n_win + slot]
        slot_m[slot] = m
        slot_l[slot] = l
        slot_o[slot] = o

    for idx in range(2):
        o_c, gate_t, _ = per_block[idx]
        o_w = _merge_tiles(results[idx * n_win:(idx + 1) * n_win])
        slots = range(0, n0_max) if idx == 0 else range(1, n_slots)
        parts = []
        for slot in slots:
            if slot == 0 or slot >= n0_max:
                m = slot_m[slot]
            else:
                mine = (slot < n0) if idx == 0 else (slot >= n0)
                m = jnp.where(mine, slot_m[slot], NEG_BIG)
            parts.append((m, slot_l[slot], slot_o[slot]))
        o_s = _merge_tiles(parts)
        row0 = pl.multiple_of(block_ids[idx] * Q_BLOCK, Q_BLOCK)
        for hh in range(NSA_HPG // 2):
            tiles = []
            for a in range(2):
                n = 2 * hh + a
                lanes = slice(n * Q_BLOCK, (n + 1) * Q_BLOCK)
                c0 = N_BRANCH * n
                tiles.append(gate_t[c0:c0 + 1, :] * o_c[:, lanes]
                             + gate_t[c0 + 1:c0 + 2, :] * o_s[:, lanes]
                             + gate_t[c0 + 2:c0 + 3, :] * o_w[:, lanes])
            o_ref[0, pl.ds(row0, Q_BLOCK), hh * LANES:(hh + 1) * LANES] = jnp.concatenate(tiles, axis=0).T


def _sel_map_t(seq):
    n_c = seq // CMP_STRIDE - 1
    n_s = seq // SEL_BLOCK
    tok = np.arange(seq)
    start = np.arange(n_c) * CMP_STRIDE
    cover_c = (tok[None, :] >= start[:, None]) & (tok[None, :] < start[:, None] + CMP_BLOCK)
    cover_s = (tok[:, None] // SEL_BLOCK) == np.arange(n_s)[None, :]
    m = cover_c.astype(np.float32) @ cover_s.astype(np.float32) / np.float32(CMP_BLOCK)
    out = np.zeros((n_s, n_c + 1), np.float32)
    out[:, :n_c] = m.T
    return jnp.asarray(out)


def _attention(q, gates, kvc, kv, qg_t, kg_ext):
    b, s, _ = q.shape
    g = NSA_GROUPS
    nqb = s // Q_BLOCK
    ncmp = s // CMP_STRIDE
    nsel = s // SEL_BLOCK
    assert s % SEL_CHUNK == 0 and nsel % SUBLANES == 0 and nqb % 2 == 0
    gw = NSA_HPG * HEAD_DIM
    cols = NSA_HPG * Q_BLOCK
    n_slots = nqb // 2 + 1

    def kv_spec(j):
        return pl.BlockSpec((1, 1, 1, s, LANES), lambda bi, gi, i: (j, bi, gi, 0, 0))

    return pl.pallas_call(
        _attn_kernel,
        grid=(b, g, nqb // 2),
        in_specs=[
            pl.BlockSpec((1, s, gw), lambda bi, gi, i: (bi, 0, gi)),
            pl.BlockSpec((1, 1, s, LANES), lambda bi, gi, i: (bi, gi, 0, 0)),
            pl.BlockSpec((1, 1, ncmp, LANES), lambda bi, gi, i: (bi, gi, 0, 0)),
            kv_spec(1), kv_spec(2),
            pl.BlockSpec((nsel, ncmp), lambda bi, gi, i: (0, 0)),
            pl.BlockSpec((HEAD_DIM, Q_BLOCK), lambda bi, gi, i: (0, 0)),
            pl.BlockSpec((N_BRANCH, LANES), lambda bi, gi, i: (0, 0)),
        ],
        out_specs=pl.BlockSpec((1, s, gw), lambda bi, gi, i: (bi, 0, gi)),
        out_shape=jax.ShapeDtypeStruct((b, s, Q_DIM), F32),
        scratch_shapes=[
            pltpu.VMEM((s, LANES), BF16),
            pltpu.VMEM((WINDOW + s, LANES), BF16),
            pltpu.VMEM((s // LANES, HEAD_DIM, LANES), BF16),
            pltpu.VMEM(((WINDOW + s) // LANES, HEAD_DIM, LANES), BF16),
            pltpu.VMEM((2, nsel, Q_BLOCK), F32),
            pltpu.VMEM((2, LANES, cols), BF16),
            pltpu.VMEM((n_slots, 1, cols), F32),
            pltpu.VMEM((n_slots, 1, cols), F32),
            pltpu.VMEM((n_slots, HEAD_DIM, cols), F32),
        ],
        compiler_params=_cparams(3),
        name="nsa_attention",
    )(q, gates, kvc, kv, kv, _sel_map_t(s), qg_t, kg_ext)


def _out_proj_kernel(y_ref, x_ref, gate_ref, w_ref, o_ref):
    mix = jnp.dot(y_ref[0].astype(BF16), w_ref[...], preferred_element_type=F32)
    o_ref[0] = x_ref[0] + gate_ref[0] * mix


def _out_proj(y, x, gate, w):
    b, s, d = x.shape
    k = y.shape[-1]
    tm = min(ROW_TILE, s)
    return pl.pallas_call(
        _out_proj_kernel,
        grid=(b, s // tm),
        in_specs=[
            pl.BlockSpec((1, tm, k), lambda bi, i: (bi, i, 0)),
            pl.BlockSpec((1, tm, d), lambda bi, i: (bi, i, 0)),
            pl.BlockSpec((1, 1, d), lambda bi, i: (bi, 0, 0)),
            pl.BlockSpec((k, d), lambda bi, i: (0, 0)),
        ],
        out_specs=pl.BlockSpec((1, tm, d), lambda bi, i: (bi, i, 0)),
        out_shape=jax.ShapeDtypeStruct((b, s, d), F32),
        compiler_params=_cparams(2),
        name="out_proj",
    )(y, x, gate, w)


def _ffn_kernel(x_ref, g_ref, sc_ref, sh_ref, gate_ref, wg_ref, wu_ref, wo_ref, o_ref):
    x = x_ref[0]
    hh = _norm_mod(x, g_ref[...], sc_ref[0], sh_ref[0]).astype(BF16)
    a = jnp.dot(hh, wg_ref[...], preferred_element_type=F32)
    u = jnp.dot(hh, wu_ref[...], preferred_element_type=F32)
    act = (jax.nn.silu(a) * u).astype(BF16)
    o_ref[0] = x + gate_ref[0] * jnp.dot(act, wo_ref[...], preferred_element_type=F32)


def _ffn(x, gain, sc, sh, gate, w_in, w_out):
    b, s, d = x.shape
    hidden = w_out.shape[0]
    tm = min(ROW_TILE, s)
    vec = pl.BlockSpec((1, 1, d), lambda bi, i: (bi, 0, 0))
    return pl.pallas_call(
        _ffn_kernel,
        grid=(b, s // tm),
        in_specs=[
            pl.BlockSpec((1, tm, d), lambda bi, i: (bi, i, 0)),
            pl.BlockSpec((1, d), lambda bi, i: (0, 0)),
            vec, vec, vec,
            pl.BlockSpec((d, hidden), lambda bi, i: (0, 0)),
            pl.BlockSpec((d, hidden), lambda bi, i: (0, 1)),
            pl.BlockSpec((hidden, d), lambda bi, i: (0, 0)),
        ],
        out_specs=pl.BlockSpec((1, tm, d), lambda bi, i: (bi, i, 0)),
        out_shape=jax.ShapeDtypeStruct((b, s, d), F32),
        compiler_params=_cparams(2),
        name="ffn",
    )(x, gain, sc, sh, gate, w_in, w_in, w_out)


def _rg_proj_kernel(x_ref, g_ref, sc_ref, sh_ref, w_ref, gate_ref, u_ref):
    hh = _norm_mod(x_ref[0], g_ref[...], sc_ref[0], sh_ref[0])
    p = jnp.dot(hh.astype(BF16), w_ref[...], preferred_element_type=F32)
    r = gate_ref.shape[-1]
    gate_ref[0] = p[:, :r]
    u_ref[0] = p[:, r:]


def _rg_proj(x, gain, sc, sh, w):
    b, s, d = x.shape
    r = w.shape[1] // 2
    tm = min(ROW_TILE, s)
    vec = pl.BlockSpec((1, 1, d), lambda bi, i: (bi, 0, 0))
    out = pl.BlockSpec((1, tm, r), lambda bi, i: (bi, i, 0))
    return pl.pallas_call(
        _rg_proj_kernel,
        grid=(b, s // tm),
        in_specs=[
            pl.BlockSpec((1, tm, d), lambda bi, i: (bi, i, 0)),
            pl.BlockSpec((1, d), lambda bi, i: (0, 0)),
            vec, vec,
            pl.BlockSpec((d, 2 * r), lambda bi, i: (0, 0)),
        ],
        out_specs=[out, out],
        out_shape=[jax.ShapeDtypeStruct((b, s, r), F32)] * 2,
        compiler_params=_cparams(2),
        name="rg_proj",
    )(x, gain, sc, sh, w)


def _sigmoid(x):
    return 0.5 * jnp.tanh(0.5 * x) + 0.5


def _band_starts(width, block_w):
    band = 2 * MXU_WIDTH
    starts = []
    for c0 in range(0, width, MXU_WIDTH):
        c1 = min(c0 + MXU_WIDTH, width)
        lo = (c0 // block_w) * block_w
        hi = ((c1 - 1) // block_w + 1) * block_w
        k0 = min((lo // LANES) * LANES, width - band)
        assert k0 >= 0 and k0 <= lo and hi <= k0 + band
        starts.append(k0)
    return starts


def _band_weights(w_a, w_x):
    nb, bw, _ = w_a.shape
    width = nb * bw
    eye = jnp.eye(nb, dtype=w_a.dtype)
    dense = lambda w: jnp.einsum('nde,nm->ndme', w, eye).reshape(width, width)
    da, dx = dense(w_a), dense(w_x)
    band = 2 * MXU_WIDTH
    tiles = []
    for t, k0 in enumerate(_band_starts(width, bw)):
        c0 = t * MXU_WIDTH
        c1 = min(c0 + MXU_WIDTH, width)
        pad = ((0, 0), (0, MXU_WIDTH - (c1 - c0)))
        tiles.append(jnp.concatenate(
            [jnp.pad(da[k0:k0 + band, c0:c1], pad), jnp.pad(dx[k0:k0 + band, c0:c1], pad)], axis=1))
    return jnp.stack(tiles).astype(BF16)


def _rg_scan_kernel(u_ref, gate_ref, cw_ref, cb_ref, wb_ref, ba_ref, bx_ref, lam_ref, y_ref,
                    u_halo, h_scr, a_scr, b_scr, hs_scr, *, band_starts, block_w):
    step = pl.program_id(0)
    nb, ts, width = u_ref.shape
    rows = nb * ts
    halo = CONV_WIDTH - 1
    band = 2 * MXU_WIDTH

    @pl.when(step == 0)
    def _init():
        u_halo[0:halo] = jnp.zeros((halo, nb, width), F32)
        h_scr[...] = jnp.zeros(h_scr.shape, F32)

    u_halo[halo:] = jnp.swapaxes(u_ref[...], 0, 1)
    uc = jnp.zeros((ts, nb, width), F32) + cb_ref[...]
    for k in range(CONV_WIDTH):
        uc = uc + cw_ref[k:k + 1, :] * u_halo[k:k + ts]
    u_halo[0:halo] = u_halo[ts:ts + halo]
    uc = uc.reshape(rows, width)
    ub = uc.astype(BF16)

    z = -lam_ref[...]
    log_a_unit = -RG_C * (jnp.maximum(z, 0.0) + jnp.log1p(jnp.exp(-jnp.abs(z))))

    for t, k0 in enumerate(band_starts):
        c0 = t * MXU_WIDTH
        c1 = min(c0 + MXU_WIDTH, width)
        n = c1 - c0
        zz = jnp.dot(ub[:, k0:k0 + band], wb_ref[t], preferred_element_type=F32)
        r = _sigmoid(zz[:, :n] + ba_ref[:, c0:c1])
        ig = _sigmoid(zz[:, MXU_WIDTH:MXU_WIDTH + n] + bx_ref[:, c0:c1])
        log_a = log_a_unit[:, c0:c1] * r
        a = jnp.exp(log_a)
        bb = jnp.sqrt(-jnp.tanh(log_a) * (a * a + 1.0)) * (ig * uc[:, c0:c1])
        for ct in range(n // LANES):
            lanes = slice(ct * LANES, (ct + 1) * LANES)
            a_scr[c0 // LANES + ct] = a[:, lanes]
            b_scr[c0 // LANES + ct] = bb[:, lanes]

    n_ct = width // LANES

    def scan_step(t, h):
        rows_t = pl.ds(pl.multiple_of(t * nb, nb), nb)
        new = []
        for ct in range(n_ct):
            hc = a_scr[ct, rows_t, :] * h[ct] + b_scr[ct, rows_t, :]
            hs_scr[ct, rows_t, :] = hc
            new.append(hc)
        return tuple(new)

    h_fin = lax.fori_loop(0, ts, scan_step, tuple(h_scr[ct] for ct in range(n_ct)), unroll=SUBLANES)
    for ct in range(n_ct):
        lanes = slice(ct * LANES, (ct + 1) * LANES)
        h_scr[ct] = h_fin[ct]
        hs = jnp.swapaxes(hs_scr[ct].reshape(ts, nb, LANES), 0, 1)
        y_ref[:, :, lanes] = hs * jax.nn.gelu(gate_ref[:, :, lanes])


def _rg_scan(u, gate, conv_w, conv_b, w_band, b_a, b_x, lam, block_w):
    b, s, width = u.shape
    assert b == SUBLANES
    ts = min(RG_TIME_TILE, s)
    rows = b * ts
    band_starts = tuple(_band_starts(width, block_w))
    tile = pl.BlockSpec((b, ts, width), lambda t: (0, t, 0))
    vec = pl.BlockSpec((1, width), lambda t: (0, 0))
    kern = functools.partial(_rg_scan_kernel, band_starts=band_starts, block_w=block_w)
    return pl.pallas_call(
        kern,
        grid=(s // ts,),
        in_specs=[
            tile, tile,
            pl.BlockSpec((CONV_WIDTH, width), lambda t: (0, 0)),
            vec,
            pl.BlockSpec(w_band.shape, lambda t: (0, 0, 0)),
            vec, vec, vec,
        ],
        out_specs=tile,
        out_shape=jax.ShapeDtypeStruct((b, s, width), F32),
        scratch_shapes=[
            pltpu.VMEM((ts + CONV_WIDTH - 1, b, width), F32),
            pltpu.VMEM((width // LANES, b, LANES), F32),
            pltpu.VMEM((width // LANES, rows, LANES), F32),
            pltpu.VMEM((width // LANES, rows, LANES), F32),
            pltpu.VMEM((width // LANES, rows, LANES), F32),
        ],
        compiler_params=_cparams(1),
        name="rg_scan",
    )(u, gate, conv_w, conv_b, w_band, b_a, b_x, lam)


def kernel(x, c, ada_w, ada_b, norm1_g, norm2_g, nsa_w_in, nsa_w_out, nsa_cmp_pos, nsa_cmp_w1, nsa_cmp_w2, nsa_q_gain, nsa_k_gain, rg_w_in, rg_conv_w, rg_conv_b, rg_w_a, rg_b_a, rg_w_x, rg_b_x, rg_lam, rg_w_out, ffn_w_in, ffn_w_out):
    depth, d, _ = ada_w.shape
    b = x.shape[0]
    mod = _ada_mod(c, ada_w, ada_b).reshape(depth, b, 6, 1, d)
    for layer in range(depth):
        sh1, sc1, g1, sh2, sc2, g2 = (mod[layer, :, k] for k in range(6))
        n1 = norm1_g[layer].reshape(1, d)
        n2 = norm2_g[layer].reshape(1, d)
        j = layer // N_MIXERS
        if layer % N_MIXERS == 0:
            kg_ext = jnp.concatenate([nsa_k_gain[j], jnp.zeros_like(nsa_k_gain[j])], axis=1)
            qg_t = jnp.broadcast_to(nsa_q_gain[j][:, None], (HEAD_DIM, Q_BLOCK))
            q, kv, gates = _nsa_proj(x, n1, sc1, sh1, _nsa_weight_layout(nsa_w_in[j]))
            kvc = _compress(kv, *_compress_weights(nsa_cmp_pos[j], nsa_cmp_w1[j], nsa_cmp_w2[j]), kg_ext)
            o = _attention(q, gates, kvc, kv, qg_t, kg_ext)
            x = _out_proj(o, x, g1, nsa_w_out[j].astype(BF16))
        else:
            width = rg_w_out.shape[1]
            block_w = rg_w_a.shape[-1]
            gate, u = _rg_proj(x, n1, sc1, sh1, rg_w_in[j].astype(BF16))
            y = _rg_scan(u, gate, rg_conv_w[j], rg_conv_b[j].reshape(1, width),
                         _band_weights(rg_w_a[j], rg_w_x[j]),
                         rg_b_a[j].reshape(1, width), rg_b_x[j].reshape(1, width),
                         rg_lam[j].reshape(1, width), block_w)
            x = _out_proj(y, x, g1, rg_w_out[j].astype(BF16))
        x = _ffn(x, n2, sc2, sh2, g2, ffn_w_in[layer].astype(BF16), ffn_w_out[layer].astype(BF16))
    return x
```

```python
import functools

import numpy as np
import jax
import jax.numpy as jnp
from jax import lax
from jax.experimental import pallas as pl
from jax.experimental.pallas import tpu as pltpu

F32 = jnp.float32
BF16 = jnp.bfloat16

EPS = 1e-6
N_MIXERS = 2

NSA_HEADS = 16
NSA_GROUPS = 4
NSA_HPG = NSA_HEADS // NSA_GROUPS
HEAD_DIM = 64
CMP_STRIDE = 16
CMP_BLOCK = 2 * CMP_STRIDE
SEL_BLOCK = 64
SEL_TOP_N = 8
WINDOW = 512
Q_BLOCK = 128
N_BRANCH = 3
Q_DIM = NSA_HEADS * HEAD_DIM
KV_DIM = NSA_GROUPS * HEAD_DIM
GATE_COLS = N_BRANCH * NSA_HEADS
GATE_PER_GROUP = N_BRANCH * NSA_HPG

RNN_BLOCKS = 16
CONV_WIDTH = 4
RG_C = 8.0

LANES = 128
SUBLANES = 8
MXU_WIDTH = 256
VMEM_LIMIT_BYTES = 56 * 1024 * 1024

NEG_BIG = -1e30
LOG2_E = 1.4426950408889634
BLOCK_LANE0 = HEAD_DIM
PAD_LANE = LANES - 1
VT_ROWS = HEAD_DIM + 16
SEL_CHUNK = 512
RG_TIME_TILE = 64
ROW_TILE = 512


def _cparams(n_axes, flags=None):
    return pltpu.CompilerParams(
        dimension_semantics=("arbitrary",) * n_axes,
        vmem_limit_bytes=VMEM_LIMIT_BYTES,
        flags=flags,
    )


def _resident(shape):
    return pl.BlockSpec(shape, lambda *_: (0,) * len(shape), pipeline_mode=pl.Buffered(1))


def _norm_mod(x, gain, scale, shift):
    ms = jnp.mean(x * x, axis=-1, keepdims=True)
    y = x * lax.rsqrt(ms + EPS) * gain
    return y * (1.0 + scale) + shift


def _ada_kernel(c_ref, w_ref, b_ref, o_ref):
    cond = jax.nn.silu(c_ref[...])
    o_ref[0] = jnp.dot(cond, w_ref[0], preferred_element_type=F32,
                       precision=lax.Precision.HIGHEST) + b_ref[0]


def _ada_mod(c, ada_w, ada_b):
    depth, d, n = ada_w.shape
    b = c.shape[0]
    tn = n // 4
    return pl.pallas_call(
        _ada_kernel,
        grid=(depth, n // tn),
        in_specs=[
            pl.BlockSpec((b, d), lambda l, j: (0, 0)),
            pl.BlockSpec((1, d, tn), lambda l, j: (l, 0, j)),
            pl.BlockSpec((1, 1, tn), lambda l, j: (l, 0, j)),
        ],
        out_specs=pl.BlockSpec((1, b, tn), lambda l, j: (l, 0, j)),
        out_shape=jax.ShapeDtypeStruct((depth, b, n), F32),
        compiler_params=_cparams(2),
        name="ada_mod",
    )(c, ada_w, ada_b.reshape(depth, 1, n))


def _nsa_weight_layout(w_in):
    d = w_in.shape[0]
    kv = w_in[:, Q_DIM:Q_DIM + 6 * KV_DIM].reshape(d, N_BRANCH, 2, NSA_GROUPS, HEAD_DIM)
    kv = jnp.transpose(kv, (0, 1, 3, 2, 4)).reshape(d, 6 * KV_DIM)
    gates = jnp.pad(w_in[:, Q_DIM + 6 * KV_DIM:], ((0, 0), (0, LANES - GATE_COLS)))
    return jnp.concatenate([w_in[:, :Q_DIM], kv, gates], axis=1).astype(BF16)


def _nsa_proj_kernel(x_ref, g_ref, sc_ref, sh_ref, w_ref, q_ref, kv_ref, gate_ref):
    hh = _norm_mod(x_ref[0], g_ref[...], sc_ref[0], sh_ref[0])
    p = jnp.dot(hh.astype(BF16), w_ref[...], preferred_element_type=F32)
    q_ref[0] = p[:, :Q_DIM]
    for j in range(N_BRANCH):
        for g in range(NSA_GROUPS):
            lo = Q_DIM + (j * NSA_GROUPS + g) * LANES
            kv_ref[j, 0, g] = p[:, lo:lo + LANES]
    gate = jax.nn.sigmoid(p[:, Q_DIM + 6 * KV_DIM:])
    for g in range(NSA_GROUPS):
        shift = (LANES - GATE_PER_GROUP * g) % LANES
        gate_ref[0, g] = gate if shift == 0 else pltpu.roll(gate, shift, 1)


def _nsa_proj(x, gain, sc, sh, w):
    b, s, d = x.shape
    tm = min(ROW_TILE, s)
    n = w.shape[1]
    vec = pl.BlockSpec((1, 1, d), lambda bi, i: (bi, 0, 0))
    return pl.pallas_call(
        _nsa_proj_kernel,
        grid=(b, s // tm),
        in_specs=[
            pl.BlockSpec((1, tm, d), lambda bi, i: (bi, i, 0)),
            pl.BlockSpec((1, d), lambda bi, i: (0, 0)),
            vec, vec,
            pl.BlockSpec((d, n), lambda bi, i: (0, 0)),
        ],
        out_specs=[
            pl.BlockSpec((1, tm, Q_DIM), lambda bi, i: (bi, i, 0)),
            pl.BlockSpec((N_BRANCH, 1, NSA_GROUPS, tm, LANES), lambda bi, i: (0, bi, 0, i, 0)),
            pl.BlockSpec((1, NSA_GROUPS, tm, LANES), lambda bi, i: (bi, 0, i, 0)),
        ],
        out_shape=[
            jax.ShapeDtypeStruct((b, s, Q_DIM), F32),
            jax.ShapeDtypeStruct((N_BRANCH, b, NSA_GROUPS, s, LANES), F32),
            jax.ShapeDtypeStruct((b, NSA_GROUPS, s, LANES), F32),
        ],
        compiler_params=_cparams(2),
        name="nsa_proj",
    )(x, gain, sc, sh, w)


def _k_lane_norm(t, gain_ext):
    lane = lax.broadcasted_iota(jnp.int32, t.shape, 1)
    ms = jnp.sum(jnp.where(lane < HEAD_DIM, t * t, 0.0), axis=-1, keepdims=True) * (1.0 / HEAD_DIM)
    return t * lax.rsqrt(ms + EPS) * gain_ext


def _compress_weights(pos, w1, w2):
    hidden = w1.shape[-1]
    w1p = w1.reshape(2, 2, CMP_STRIDE, HEAD_DIM, hidden)
    z = jnp.zeros_like(w1p[0])
    wk = jnp.concatenate([w1p[0], z], axis=-1)
    wv = jnp.concatenate([z, w1p[1]], axis=-1)
    w1_blk = jnp.concatenate([wk, wv], axis=2).astype(BF16)
    pos_blk = jnp.concatenate([pos[0], pos[1]], axis=-1).reshape(2, CMP_STRIDE, 1, LANES)
    z2 = jnp.zeros_like(w2[0])
    w2_blk = jnp.concatenate([jnp.concatenate([w2[0], z2], axis=1),
                              jnp.concatenate([z2, w2[1]], axis=1)], axis=0).astype(BF16)
    return pos_blk, w1_blk, w2_blk


def _compress_kernel(kv_ref, pos_ref, w1_ref, w2_ref, kg_ref, o_ref):
    g, seq = kv_ref.shape[2:4]
    nch = seq // CMP_STRIDE
    rows = g * nch
    first = None
    second = None
    for p in range(CMP_STRIDE):
        tok = jnp.concatenate([kv_ref[0, 0, gi, pl.ds(p, nch, stride=CMP_STRIDE), :] for gi in range(g)], axis=0)
        fa = jnp.dot((tok + pos_ref[0, p]).astype(BF16), w1_ref[0, p], preferred_element_type=F32)
        fb = jnp.dot((tok + pos_ref[1, p]).astype(BF16), w1_ref[1, p], preferred_element_type=F32)
        first = fa if first is None else first + fa
        second = fb if second is None else second + fb
    h1 = first + pltpu.roll(second, rows - 1, 0)
    out = jnp.dot(jax.nn.silu(h1).astype(BF16), w2_ref[...], preferred_element_type=F32)
    lane = lax.broadcasted_iota(jnp.int32, out.shape, 1)
    out = jnp.where(lane < HEAD_DIM, _k_lane_norm(out, kg_ref[0:1]), out)
    r = lax.broadcasted_iota(jnp.int32, out.shape, 0)
    out = jnp.where((r & (nch - 1)) == nch - 1, 0.0, out)
    o_ref[0] = out.reshape(g, nch, LANES)


def _compress(kv, pos_blk, w1_blk, w2_blk, kg_ext):
    _, b, g, s, _ = kv.shape
    nch = s // CMP_STRIDE
    assert nch & (nch - 1) == 0
    return pl.pallas_call(
        _compress_kernel,
        grid=(b,),
        in_specs=[
            pl.BlockSpec((1, 1, g, s, LANES), lambda bi: (0, bi, 0, 0, 0)),
            pl.BlockSpec(pos_blk.shape, lambda bi: (0, 0, 0, 0)),
            pl.BlockSpec(w1_blk.shape, lambda bi: (0, 0, 0, 0)),
            pl.BlockSpec(w2_blk.shape, lambda bi: (0, 0)),
            pl.BlockSpec((N_BRANCH, LANES), lambda bi: (0, 0)),
        ],
        out_specs=pl.BlockSpec((1, g, nch, LANES), lambda bi: (bi, 0, 0, 0)),
        out_shape=jax.ShapeDtypeStruct((b, g, nch, LANES), F32),
        compiler_params=_cparams(1),
        name="nsa_compress",
    )(kv, pos_blk, w1_blk, w2_blk, kg_ext)


def _sel_slot_count(nqb):
    chunks = lambda i: -(-(i + 1) * Q_BLOCK // SEL_CHUNK)
    counts = {chunks(p) + chunks(nqb - 1 - p) for p in range(nqb // 2)}
    assert len(counts) == 1, counts
    return counts.pop()


def _tile_scores(k_tile, q_t):
    return jnp.concatenate([jnp.dot(k_tile, q_t[:, h:h + MXU_WIDTH], preferred_element_type=F32)
                            for h in range(0, q_t.shape[1], MXU_WIDTH)], axis=1)


def _tile_softmax(s, bias, vt_tile):
    if bias is not None:
        rows = bias.shape[0]
        biased = s[:rows] + jnp.concatenate([bias] * (s.shape[1] // Q_BLOCK), axis=1)
        s = biased if rows == s.shape[0] else jnp.concatenate([biased, s[rows:]], axis=0)
    m = jnp.max(s, axis=0, keepdims=True)
    p = jnp.exp2(s - m)
    ol = jnp.dot(vt_tile, p.astype(BF16), preferred_element_type=F32)
    return m, ol[HEAD_DIM:HEAD_DIM + 1], ol[:HEAD_DIM]


def _merge_tiles(parts):
    m = parts[0][0]
    for mi, _, _ in parts[1:]:
        m = jnp.maximum(m, mi)
    l = None
    o = None
    for mi, li, oi in parts:
        w = jnp.exp2(mi - m)
        l = w * li if l is None else l + w * li
        o = w * oi if o is None else o + w * oi
    return o / l


def _attn_kernel(q_ref, gate_ref, kvc_ref, ks_ref, kw_ref, selmap_ref, qg_ref, kg_ref, o_ref,
                 ksn, kwn, vst, vwt, qt_scr, slot_m, slot_l, slot_o):
    pair = pl.program_id(2)
    seq = ks_ref.shape[3]
    ncmp = kvc_ref.shape[2]
    nsel = selmap_ref.shape[0]
    nqb = seq // Q_BLOCK
    cols = NSA_HPG * Q_BLOCK
    win_tiles = WINDOW // LANES
    chunk_tiles = SEL_CHUNK // LANES
    scale = HEAD_DIM ** -0.5 * LOG2_E

    @pl.when(pair == 0)
    def _prepare_kv():
        lane = lax.broadcasted_iota(jnp.int32, (LANES, LANES), 1)
        row = lax.broadcasted_iota(jnp.int32, (LANES, LANES), 0)
        ones_row = (lax.broadcasted_iota(jnp.int32, (VT_ROWS - HEAD_DIM, LANES), 0) == 0).astype(BF16)
        pad_mark = (lane == PAD_LANE).astype(BF16)
        for w in range(win_tiles):
            kwn[w * LANES:(w + 1) * LANES, :] = pad_mark
        vwt[0:win_tiles] = jnp.zeros((win_tiles, VT_ROWS, LANES), BF16)

        def body(c, carry):
            r0 = pl.multiple_of(c * LANES, LANES)
            ts = ks_ref[0, 0, 0, pl.ds(r0, LANES), :]
            in_block = (lane - BLOCK_LANE0 == lax.shift_right_logical(r0 + row, SEL_BLOCK.bit_length() - 1))
            ksn[pl.ds(r0, LANES), :] = (_k_lane_norm(ts, kg_ref[1:2]) + in_block.astype(F32)).astype(BF16)
            vst[c] = jnp.concatenate([ts.T[HEAD_DIM:, :].astype(BF16), ones_row], axis=0)
            tw = kw_ref[0, 0, 0, pl.ds(r0, LANES), :]
            kwn[pl.ds(WINDOW + r0, LANES), :] = _k_lane_norm(tw, kg_ref[2:3]).astype(BF16)
            vwt[win_tiles + c] = jnp.concatenate([tw.T[HEAD_DIM:, :].astype(BF16), ones_row], axis=0)
            return carry

        lax.fori_loop(0, seq // LANES, body, 0)

    kc = kvc_ref[0, 0]
    kc_b = kc.astype(BF16)
    vc_t = kc.T[HEAD_DIM:, :].astype(BF16)
    block_ids = (pair, nqb - 1 - pair)

    def block_branches(idx):
        i = block_ids[idx]
        t0 = i * Q_BLOCK
        row0 = pl.multiple_of(t0, Q_BLOCK)

        q = q_ref[0, pl.ds(row0, Q_BLOCK), :]
        heads = []
        for h in range(NSA_HPG // 2):
            qt = q[:, h * LANES:(h + 1) * LANES].T
            for a in range(2):
                x = qt[a * HEAD_DIM:(a + 1) * HEAD_DIM]
                ms = jnp.mean(x * x, axis=0, keepdims=True)
                heads.append(x * lax.rsqrt(ms + EPS) * qg_ref[...] * scale)
        q_top = jnp.concatenate(heads, axis=1).astype(BF16)
        mask_rows = lax.broadcasted_iota(jnp.int32, (LANES - HEAD_DIM, cols), 0) + HEAD_DIM
        q_plain = jnp.concatenate([q_top, jnp.zeros((LANES - HEAD_DIM, cols), BF16)], axis=0)
        q_win = jnp.concatenate([q_top, jnp.where(mask_rows == PAD_LANE, NEG_BIG, 0.0).astype(BF16)], axis=0)

        s = _tile_scores(kc_b, q_plain)
        tq = t0 + (lax.broadcasted_iota(jnp.int32, (ncmp, cols), 1) & (Q_BLOCK - 1))
        cmp_end = lax.broadcasted_iota(jnp.int32, (ncmp, cols), 0) * CMP_STRIDE + (CMP_BLOCK - 1)
        visible = cmp_end <= tq
        s = jnp.where(visible, s, NEG_BIG)
        m = jnp.max(s, axis=0, keepdims=True)
        p = jnp.where(visible, jnp.exp2(s - m), 0.0)
        l = jnp.sum(p, axis=0, keepdims=True)
        p_c = p / jnp.maximum(l, jnp.finfo(F32).tiny)
        o_c = jnp.dot(vc_t, p_c.astype(BF16), preferred_element_type=F32)

        p_sum = p_c[:, 0:Q_BLOCK]
        for n in range(1, NSA_HPG):
            p_sum = p_sum + p_c[:, n * Q_BLOCK:(n + 1) * Q_BLOCK]
        imp = jnp.dot(selmap_ref[...], p_sum, preferred_element_type=F32, precision=lax.Precision.HIGHEST)
        blk = lax.broadcasted_iota(jnp.int32, (nsel, Q_BLOCK), 0)
        cur = lax.shift_right_logical(t0 + lax.broadcasted_iota(jnp.int32, (nsel, Q_BLOCK), 1), 6)
        valid = blk <= cur
        forced = (blk == 0) | (blk == cur) | (blk == cur - 1)
        val = jnp.where(valid, imp, -jnp.inf)
        val = jnp.where(forced, jnp.inf, val)
        rank = jnp.zeros((nsel, Q_BLOCK), jnp.int32)
        for j2 in range(nsel):
            row = val[j2:j2 + 1, :]
            beats = (row > val) | ((row == val) & (blk > j2))
            rank = rank + beats.astype(jnp.int32)
        picked = jnp.where((rank < SEL_TOP_N) & valid, 0.0, NEG_BIG).astype(BF16)
        qt_scr[idx] = jnp.concatenate(
            [q_top, jnp.concatenate([picked] * NSA_HPG, axis=1),
             jnp.zeros((LANES - HEAD_DIM - nsel, cols), BF16)], axis=0)

        tiles = []
        off = 0
        span = WINDOW + Q_BLOCK
        while off < span:
            kc_w = min(SEL_CHUNK, span - off)
            if off < Q_BLOCK:
                rows_b = min(kc_w, Q_BLOCK - off)
                newest = False
            elif off + kc_w > WINDOW:
                assert off >= WINDOW
                rows_b = kc_w
                newest = True
            else:
                rows_b = 0
                newest = False

            def scores(off=off, kc_w=kc_w):
                k_tile = kwn[pl.ds(pl.multiple_of(t0 + off, LANES), kc_w), :]
                return _tile_scores(k_tile, q_win)

            def finish(s, off=off, kc_w=kc_w, rows_b=rows_b, newest=newest):
                bias = None
                if rows_b:
                    r = off + lax.broadcasted_iota(jnp.int32, (rows_b, Q_BLOCK), 0)
                    c = lax.broadcasted_iota(jnp.int32, (rows_b, Q_BLOCK), 1)
                    bias = jnp.where((r - WINDOW <= c) if newest else (r > c), 0.0, NEG_BIG)
                vt_tile = jnp.concatenate([vwt[i + off // LANES + cc] for cc in range(kc_w // LANES)], axis=1)
                return _tile_softmax(s, bias, vt_tile)

            tiles.append((scores, finish))
            off += kc_w
        gate_t = gate_ref[0, 0, pl.ds(row0, Q_BLOCK), :].T
        return o_c, gate_t, tiles

    per_block = [block_branches(idx) for idx in range(2)]
    n_win = len(per_block[0][2])

    n_slots = _sel_slot_count(nqb)
    n0 = lax.shift_right_logical(block_ids[0] * Q_BLOCK + Q_BLOCK + SEL_CHUNK - 1, SEL_CHUNK.bit_length() - 1)
    n0_max = (nqb // 2 * Q_BLOCK + SEL_CHUNK - 1) // SEL_CHUNK
    key_minus_query = (lax.broadcasted_iota(jnp.int32, (SEL_CHUNK, Q_BLOCK), 0)
                       - lax.broadcasted_iota(jnp.int32, (SEL_CHUNK, Q_BLOCK), 1))
    sel_tiles = []
    for slot in range(n_slots):
        last_of_block = slot == 0 or slot == n_slots - 1
        if slot == 0:
            which, kb = 0, n0 - 1
        elif slot == n_slots - 1:
            which, kb = 1, n_slots - n0 - 1
        elif slot >= n0_max:
            which, kb = 1, slot - n0
        else:
            which = (slot >= n0).astype(jnp.int32)
            kb = slot - 1 + which * (1 - n0)
        k0 = pl.multiple_of(kb * SEL_CHUNK, SEL_CHUNK)

        def scores(which=which, k0=k0):
            return _tile_scores(ksn[pl.ds(k0, SEL_CHUNK), :], qt_scr[which])

        def finish(s, which=which, kb=kb, k0=k0, last_of_block=last_of_block):
            bias = None
            if last_of_block:
                t0 = block_ids[which] * Q_BLOCK
                bias = jnp.where(key_minus_query <= t0 - k0, 0.0, NEG_BIG)
            vt_tile = jnp.concatenate([vst[kb * chunk_tiles + c] for c in range(chunk_tiles)], axis=1)
            return _tile_softmax(s, bias, vt_tile)

        sel_tiles.append((scores, finish))

    all_tiles = per_block[0][2] + per_block[1][2] + sel_tiles
    results = []
    pending = all_tiles[0][0]()
    for t, (_, finish) in enumerate(all_tiles):
        s = pending
        if t + 1 < len(all_tiles):
            pending = all_tiles[t + 1][0]()
        results.append(finish(s))
    for slot in range(n_slots):
        m, l, o = results[2 * n_win + slot]
        slot_m[slot] = m
        slot_l[slot] = l
        slot_o[slot] = o

    for idx in range(2):
        o_c, gate_t, _ = per_block[idx]
        o_w = _merge_tiles(results[idx * n_win:(idx + 1) * n_win])
        slots = range(0, n0_max) if idx == 0 else range(1, n_slots)
        parts = []
        for slot in slots:
            if slot == 0 or slot >= n0_max:
                m = slot_m[slot]
            else:
                mine = (slot < n0) if idx == 0 else (slot >= n0)
                m = jnp.where(mine, slot_m[slot], NEG_BIG)
            parts.append((m, slot_l[slot], slot_o[slot]))
        o_s = _merge_tiles(parts)
        row0 = pl.multiple_of(block_ids[idx] * Q_BLOCK, Q_BLOCK)
        for hh in range(NSA_HPG // 2):
            tiles = []
            for a in range(2):
                n = 2 * hh + a
                lanes = slice(n * Q_BLOCK, (n + 1) * Q_BLOCK)
                c0 = N_BRANCH * n
                tiles.append(gate_t[c0:c0 + 1, :] * o_c[:, lanes]
                             + gate_t[c0 + 1:c0 + 2, :] * o_s[:, lanes]
                             + gate_t[c0 + 2:c0 + 3, :] * o_w[:, lanes])
            o_ref[0, pl.ds(row0, Q_BLOCK), hh * LANES:(hh + 1) * LANES] = jnp.concatenate(tiles, axis=0).T


def _sel_map_t(seq):
    n_c = seq // CMP_STRIDE - 1
    n_s = seq // SEL_BLOCK
    tok = np.arange(seq)
    start = np.arange(n_c) * CMP_STRIDE
    cover_c = (tok[None, :] >= start[:, None]) & (tok[None, :] < start[:, None] + CMP_BLOCK)
    cover_s = (tok[:, None] // SEL_BLOCK) == np.arange(n_s)[None, :]
    m = cover_c.astype(np.float32) @ cover_s.astype(np.float32) / np.float32(CMP_BLOCK)
    out = np.zeros((n_s, n_c + 1), np.float32)
    out[:, :n_c] = m.T
    return jnp.asarray(out)


def _attention(q, gates, kvc, kv, qg_t, kg_ext):
    b, s, _ = q.shape
    g = NSA_GROUPS
    nqb = s // Q_BLOCK
    ncmp = s // CMP_STRIDE
    nsel = s // SEL_BLOCK
    assert s % SEL_CHUNK == 0 and nsel % SUBLANES == 0 and nqb % 2 == 0
    gw = NSA_HPG * HEAD_DIM
    cols = NSA_HPG * Q_BLOCK
    n_slots = _sel_slot_count(nqb)

    def kv_spec(j):
        return pl.BlockSpec((1, 1, 1, s, LANES), lambda bi, gi, i: (j, bi, gi, 0, 0))

    return pl.pallas_call(
        _attn_kernel,
        grid=(b, g, nqb // 2),
        in_specs=[
            pl.BlockSpec((1, s, gw), lambda bi, gi, i: (bi, 0, gi)),
            pl.BlockSpec((1, 1, s, LANES), lambda bi, gi, i: (bi, gi, 0, 0)),
            pl.BlockSpec((1, 1, ncmp, LANES), lambda bi, gi, i: (bi, gi, 0, 0)),
            kv_spec(1), kv_spec(2),
            pl.BlockSpec((nsel, ncmp), lambda bi, gi, i: (0, 0)),
            pl.BlockSpec((HEAD_DIM, Q_BLOCK), lambda bi, gi, i: (0, 0)),
            pl.BlockSpec((N_BRANCH, LANES), lambda bi, gi, i: (0, 0)),
        ],
        out_specs=pl.BlockSpec((1, s, gw), lambda bi, gi, i: (bi, 0, gi)),
        out_shape=jax.ShapeDtypeStruct((b, s, Q_DIM), F32),
        scratch_shapes=[
            pltpu.VMEM((s, LANES), BF16),
            pltpu.VMEM((WINDOW + s, LANES), BF16),
            pltpu.VMEM((s // LANES, VT_ROWS, LANES), BF16),
            pltpu.VMEM(((WINDOW + s) // LANES, VT_ROWS, LANES), BF16),
            pltpu.VMEM((2, LANES, cols), BF16),
            pltpu.VMEM((n_slots, 1, cols), F32),
            pltpu.VMEM((n_slots, 1, cols), F32),
            pltpu.VMEM((n_slots, HEAD_DIM, cols), F32),
        ],
        compiler_params=_cparams(3),
        name="nsa_attention",
    )(q, gates, kvc, kv, kv, _sel_map_t(s), qg_t, kg_ext)


def _ffn_core(x, g_ref, sc_ref, sh_ref, gate_ref, wg_ref, wu_ref, wo_ref):
    hh = _norm_mod(x, g_ref[...], sc_ref[0], sh_ref[0]).astype(BF16)
    a = jnp.dot(hh, wg_ref[...], preferred_element_type=F32)
    u = jnp.dot(hh, wu_ref[...], preferred_element_type=F32)
    act = (jax.nn.silu(a) * u).astype(BF16)
    return x + gate_ref[0] * jnp.dot(act, wo_ref[...], preferred_element_type=F32)


def _ffn_kernel(x_ref, g_ref, sc_ref, sh_ref, gate_ref, wg_ref, wu_ref, wo_ref, o_ref):
    o_ref[0] = _ffn_core(x_ref[0], g_ref, sc_ref, sh_ref, gate_ref, wg_ref, wu_ref, wo_ref)


def _mix_ffn_kernel(y_ref, wmix_ref, mix_gate_ref, x_ref, g_ref, sc_ref, sh_ref, gate_ref, wg_ref, wu_ref, wo_ref,
                    o_ref):
    mix = jnp.dot(y_ref[0].astype(BF16), wmix_ref[...], preferred_element_type=F32)
    x = x_ref[0] + mix_gate_ref[0] * mix
    o_ref[0] = _ffn_core(x, g_ref, sc_ref, sh_ref, gate_ref, wg_ref, wu_ref, wo_ref)


def _ffn(x, gain, sc, sh, gate, w_in, w_out, mixer_out=None):
    b, s, d = x.shape
    hidden = w_out.shape[0]
    tm = min(ROW_TILE, s)
    vec = pl.BlockSpec((1, 1, d), lambda bi, i: (bi, 0, 0))
    rows = lambda width: pl.BlockSpec((1, tm, width), lambda bi, i: (bi, i, 0))
    in_specs = [
        rows(d), _resident((1, d)), vec, vec, vec,
        pl.BlockSpec((d, hidden), lambda bi, i: (0, 0), pipeline_mode=pl.Buffered(1)),
        pl.BlockSpec((d, hidden), lambda bi, i: (0, 1), pipeline_mode=pl.Buffered(1)),
        _resident((hidden, d)),
    ]
    args = (x, gain, sc, sh, gate, w_in, w_in, w_out)
    body = _ffn_kernel
    if mixer_out is not None:
        y, w_mix, mix_gate = mixer_out
        in_specs = [rows(y.shape[-1]), _resident(w_mix.shape), vec] + in_specs
        args = (y, w_mix, mix_gate) + args
        body = _mix_ffn_kernel
    return pl.pallas_call(
        body,
        grid=(b, s // tm),
        in_specs=in_specs,
        out_specs=rows(d),
        out_shape=jax.ShapeDtypeStruct((b, s, d), F32),
        compiler_params=_cparams(2),
        name="ffn",
    )(*args)


def _sigmoid(x):
    return 0.5 * jnp.tanh(0.5 * x) + 0.5


def _band_starts(width, block_w):
    band = 2 * MXU_WIDTH
    starts = []
    for c0 in range(0, width, MXU_WIDTH):
        c1 = min(c0 + MXU_WIDTH, width)
        lo = (c0 // block_w) * block_w
        hi = ((c1 - 1) // block_w + 1) * block_w
        k0 = min((lo // LANES) * LANES, width - band)
        assert k0 >= 0 and k0 <= lo and hi <= k0 + band
        starts.append(k0)
    return starts


def _band_weights(w_a, w_x):
    nb, bw, _ = w_a.shape
    width = nb * bw
    eye = jnp.eye(nb, dtype=w_a.dtype)
    dense = lambda w: jnp.einsum('nde,nm->ndme', w, eye).reshape(width, width)
    da, dx = dense(w_a), dense(w_x)
    band = 2 * MXU_WIDTH
    tiles = []
    for t, k0 in enumerate(_band_starts(width, bw)):
        c0 = t * MXU_WIDTH
        c1 = min(c0 + MXU_WIDTH, width)
        pad = ((0, 0), (0, MXU_WIDTH - (c1 - c0)))
        tiles.append(jnp.concatenate(
            [jnp.pad(da[k0:k0 + band, c0:c1], pad), jnp.pad(dx[k0:k0 + band, c0:c1], pad)], axis=1))
    return jnp.stack(tiles).astype(BF16)


def _rg_mixer_kernel(x_ref, g_ref, sc_ref, sh_ref, res_gate_ref, win_ref, cw_ref, cb_ref, wb_ref, ba_ref,
                     bx_ref, lam_ref, wout_ref, o_ref,
                     u_halo, h_scr, a_scr, b_scr, hs_scr, gate_scr, y_scr, *, band_starts):
    step = pl.program_id(0)
    nb, ts, d = x_ref.shape
    width = wout_ref.shape[0]
    rows = nb * ts
    halo = CONV_WIDTH - 1
    band = 2 * MXU_WIDTH

    @pl.when(step == 0)
    def _init():
        u_halo[0:halo] = jnp.zeros((halo, nb, width), F32)
        h_scr[...] = jnp.zeros(h_scr.shape, F32)

    hh = _norm_mod(x_ref[...], g_ref[...], sc_ref[...], sh_ref[...]).reshape(rows, d)
    proj = jnp.dot(hh.astype(BF16), win_ref[...], preferred_element_type=F32)
    gate_scr[...] = proj[:, :width]

    u_halo[halo:] = jnp.swapaxes(proj[:, width:].reshape(nb, ts, width), 0, 1)
    uc = jnp.zeros((ts, nb, width), F32) + cb_ref[...]
    for k in range(CONV_WIDTH):
        uc = uc + cw_ref[k:k + 1, :] * u_halo[k:k + ts]
    u_halo[0:halo] = u_halo[ts:ts + halo]
    uc = uc.reshape(rows, width)
    ub = uc.astype(BF16)

    z = -lam_ref[...]
    log_a_unit = -RG_C * (jnp.maximum(z, 0.0) + jnp.log1p(jnp.exp(-jnp.abs(z))))

    for t, k0 in enumerate(band_starts):
        c0 = t * MXU_WIDTH
        c1 = min(c0 + MXU_WIDTH, width)
        n = c1 - c0
        zz = jnp.dot(ub[:, k0:k0 + band], wb_ref[t], preferred_element_type=F32)
        r = _sigmoid(zz[:, :n] + ba_ref[:, c0:c1])
        ig = _sigmoid(zz[:, MXU_WIDTH:MXU_WIDTH + n] + bx_ref[:, c0:c1])
        log_a = log_a_unit[:, c0:c1] * r
        a = jnp.exp(log_a)
        bb = jnp.sqrt(-jnp.tanh(log_a) * (a * a + 1.0)) * (ig * uc[:, c0:c1])
        for ct in range(n // LANES):
            lanes = slice(ct * LANES, (ct + 1) * LANES)
            a_scr[c0 // LANES + ct] = a[:, lanes]
            b_scr[c0 // LANES + ct] = bb[:, lanes]

    n_ct = width // LANES

    def scan_step(t, h):
        rows_t = pl.ds(pl.multiple_of(t * nb, nb), nb)
        new = []
        for ct in range(n_ct):
            hc = a_scr[ct, rows_t, :] * h[ct] + b_scr[ct, rows_t, :]
            hs_scr[ct, rows_t, :] = hc
            new.append(hc)
        return tuple(new)

    h_fin = lax.fori_loop(0, ts, scan_step, tuple(h_scr[ct] for ct in range(n_ct)), unroll=SUBLANES)
    for ct in range(n_ct):
        lanes = slice(ct * LANES, (ct + 1) * LANES)
        h_scr[ct] = h_fin[ct]
        hs = jnp.swapaxes(hs_scr[ct].reshape(ts, nb, LANES), 0, 1)
        y_scr[:, lanes] = (hs.reshape(rows, LANES) * jax.nn.gelu(gate_scr[:, lanes])).astype(BF16)
    mix = jnp.dot(y_scr[...], wout_ref[...], preferred_element_type=F32)
    o_ref[...] = x_ref[...] + res_gate_ref[...] * mix.reshape(nb, ts, d)


def _rg_mixer(x, gain, sc, sh, res_gate, w_in, conv_w, conv_b, w_band, b_a, b_x, lam, w_out, block_w):
    b, s, d = x.shape
    width = w_out.shape[0]
    assert b == SUBLANES
    ts = min(RG_TIME_TILE, s)
    rows = b * ts
    band_starts = tuple(_band_starts(width, block_w))
    tile = pl.BlockSpec((b, ts, d), lambda t: (0, t, 0))
    mod = _resident((b, 1, d))
    vec = _resident((1, width))
    kern = functools.partial(_rg_mixer_kernel, band_starts=band_starts)
    return pl.pallas_call(
        kern,
        grid=(s // ts,),
        in_specs=[
            tile, _resident((1, d)), mod, mod, mod,
            _resident(w_in.shape),
            _resident((CONV_WIDTH, width)), vec,
            _resident(w_band.shape),
            vec, vec, vec,
            _resident(w_out.shape),
        ],
        out_specs=tile,
        out_shape=jax.ShapeDtypeStruct((b, s, d), F32),
        scratch_shapes=[
            pltpu.VMEM((ts + CONV_WIDTH - 1, b, width), F32),
            pltpu.VMEM((width // LANES, b, LANES), F32),
            pltpu.VMEM((width // LANES, rows, LANES), F32),
            pltpu.VMEM((width // LANES, rows, LANES), F32),
            pltpu.VMEM((width // LANES, rows, LANES), F32),
            pltpu.VMEM((rows, width), F32),
            pltpu.VMEM((rows, width), BF16),
        ],
        compiler_params=_cparams(1),
        name="rg_mixer",
    )(x, gain, sc, sh, res_gate, w_in, conv_w, conv_b, w_band, b_a, b_x, lam, w_out)


def kernel(x, c, ada_w, ada_b, norm1_g, norm2_g, nsa_w_in, nsa_w_out, nsa_cmp_pos, nsa_cmp_w1, nsa_cmp_w2, nsa_q_gain, nsa_k_gain, rg_w_in, rg_conv_w, rg_conv_b, rg_w_a, rg_b_a, rg_w_x, rg_b_x, rg_lam, rg_w_out, ffn_w_in, ffn_w_out):
    depth, d, _ = ada_w.shape
    b = x.shape[0]
    mod = _ada_mod(c, ada_w, ada_b).reshape(depth, b, 6, 1, d)
    for layer in range(depth):
        sh1, sc1, g1, sh2, sc2, g2 = (mod[layer, :, k] for k in range(6))
        n1 = norm1_g[layer].reshape(1, d)
        n2 = norm2_g[layer].reshape(1, d)
        j = layer // N_MIXERS
        if layer % N_MIXERS == 0:
            kg_ext = jnp.concatenate([nsa_k_gain[j], jnp.zeros_like(nsa_k_gain[j])], axis=1)
            qg_t = jnp.broadcast_to(nsa_q_gain[j][:, None], (HEAD_DIM, Q_BLOCK))
            q, kv, gates = _nsa_proj(x, n1, sc1, sh1, _nsa_weight_layout(nsa_w_in[j]))
            kvc = _compress(kv, *_compress_weights(nsa_cmp_pos[j], nsa_cmp_w1[j], nsa_cmp_w2[j]), kg_ext)
            o = _attention(q, gates, kvc, kv, qg_t, kg_ext)
            mixer_out = (o, nsa_w_out[j].astype(BF16), g1)
        else:
            width = rg_w_out.shape[1]
            block_w = rg_w_a.shape[-1]
            x = _rg_mixer(x, n1, sc1, sh1, g1, rg_w_in[j].astype(BF16),
                          rg_conv_w[j], rg_conv_b[j].reshape(1, width),
                          _band_weights(rg_w_a[j], rg_w_x[j]),
                          rg_b_a[j].reshape(1, width), rg_b_x[j].reshape(1, width),
                          rg_lam[j].reshape(1, width), rg_w_out[j].astype(BF16), block_w)
            mixer_out = None
        x = _ffn(x, n2, sc2, sh2, g2, ffn_w_in[layer].astype(BF16), ffn_w_out[layer].astype(BF16), mixer_out)
    return x
```

```python
import functools

import numpy as np
import jax
import jax.numpy as jnp
from jax import lax
from jax.experimental import pallas as pl
from jax.experimental.pallas import tpu as pltpu

F32 = jnp.float32
BF16 = jnp.bfloat16

EPS = 1e-6
N_MIXERS = 2

NSA_HEADS = 16
NSA_GROUPS = 4
NSA_HPG = NSA_HEADS // NSA_GROUPS
HEAD_DIM = 64
CMP_STRIDE = 16
CMP_BLOCK = 2 * CMP_STRIDE
SEL_BLOCK = 64
SEL_TOP_N = 8
WINDOW = 512
Q_BLOCK = 128
N_BRANCH = 3
Q_DIM = NSA_HEADS * HEAD_DIM
KV_DIM = NSA_GROUPS * HEAD_DIM
GATE_COLS = N_BRANCH * NSA_HEADS
GATE_PER_GROUP = N_BRANCH * NSA_HPG

RNN_BLOCKS = 16
CONV_WIDTH = 4
RG_C = 8.0

LANES = 128
SUBLANES = 8
MXU_WIDTH = 256
VMEM_LIMIT_BYTES = 56 * 1024 * 1024

NEG_BIG = -1e30
LOG2_E = 1.4426950408889634
BLOCK_LANE0 = HEAD_DIM
PAD_LANE = LANES - 1
VT_ROWS = HEAD_DIM + 16
SEL_CHUNK = 256
SCORE_LOOKAHEAD = 3
RG_TIME_TILE = 64
ROW_TILE = 512


def _cparams(n_axes, flags=None):
    return pltpu.CompilerParams(
        dimension_semantics=("arbitrary",) * n_axes,
        vmem_limit_bytes=VMEM_LIMIT_BYTES,
        flags=flags,
    )


def _resident(shape):
    return pl.BlockSpec(shape, lambda *_: (0,) * len(shape), pipeline_mode=pl.Buffered(1))


def _layer_plane(shape, layer, col_block=0):
    return pl.BlockSpec((None,) + tuple(shape), lambda *_: (layer, 0, col_block), pipeline_mode=pl.Buffered(1))


def _norm_mod(x, gain, scale, shift):
    ms = jnp.mean(x * x, axis=-1, keepdims=True)
    y = x * lax.rsqrt(ms + EPS) * gain
    return y * (1.0 + scale) + shift


def _ada_kernel(c_ref, w_ref, b_ref, o_ref):
    cond = jax.nn.silu(c_ref[...])
    o_ref[0] = jnp.dot(cond, w_ref[0], preferred_element_type=F32,
                       precision=lax.Precision.HIGHEST) + b_ref[0]


def _ada_mod(c, ada_w, ada_b):
    depth, d, n = ada_w.shape
    b = c.shape[0]
    tn = n // 4
    return pl.pallas_call(
        _ada_kernel,
        grid=(depth, n // tn),
        in_specs=[
            pl.BlockSpec((b, d), lambda l, j: (0, 0)),
            pl.BlockSpec((1, d, tn), lambda l, j: (l, 0, j)),
            pl.BlockSpec((1, 1, tn), lambda l, j: (l, 0, j)),
        ],
        out_specs=pl.BlockSpec((1, b, tn), lambda l, j: (l, 0, j)),
        out_shape=jax.ShapeDtypeStruct((depth, b, n), F32),
        compiler_params=_cparams(2),
        name="ada_mod",
    )(c, ada_w, ada_b.reshape(depth, 1, n))


def _nsa_weight_layout(w_in):
    d = w_in.shape[0]
    kv = w_in[:, Q_DIM:Q_DIM + 6 * KV_DIM].reshape(d, N_BRANCH, 2, NSA_GROUPS, HEAD_DIM)
    kv = jnp.transpose(kv, (0, 1, 3, 2, 4)).reshape(d, 6 * KV_DIM)
    gates = jnp.pad(w_in[:, Q_DIM + 6 * KV_DIM:], ((0, 0), (0, LANES - GATE_COLS)))
    return jnp.concatenate([w_in[:, :Q_DIM], kv, gates], axis=1).astype(BF16)


def _nsa_proj_kernel(x_ref, g_ref, sc_ref, sh_ref, w_ref, q_ref, kv_ref, gate_ref):
    hh = _norm_mod(x_ref[0], g_ref[...], sc_ref[0], sh_ref[0])
    p = jnp.dot(hh.astype(BF16), w_ref[...], preferred_element_type=F32)
    q_ref[0] = p[:, :Q_DIM]
    for j in range(N_BRANCH):
        for g in range(NSA_GROUPS):
            lo = Q_DIM + (j * NSA_GROUPS + g) * LANES
            kv_ref[j, 0, g] = p[:, lo:lo + LANES]
    gate = jax.nn.sigmoid(p[:, Q_DIM + 6 * KV_DIM:])
    for g in range(NSA_GROUPS):
        shift = (LANES - GATE_PER_GROUP * g) % LANES
        gate_ref[0, g] = gate if shift == 0 else pltpu.roll(gate, shift, 1)


def _nsa_proj(x, gain, sc, sh, w):
    b, s, d = x.shape
    tm = min(ROW_TILE, s)
    n = w.shape[1]
    vec = pl.BlockSpec((1, 1, d), lambda bi, i: (bi, 0, 0))
    return pl.pallas_call(
        _nsa_proj_kernel,
        grid=(b, s // tm),
        in_specs=[
            pl.BlockSpec((1, tm, d), lambda bi, i: (bi, i, 0)),
            pl.BlockSpec((1, d), lambda bi, i: (0, 0)),
            vec, vec,
            pl.BlockSpec((d, n), lambda bi, i: (0, 0)),
        ],
        out_specs=[
            pl.BlockSpec((1, tm, Q_DIM), lambda bi, i: (bi, i, 0)),
            pl.BlockSpec((N_BRANCH, 1, NSA_GROUPS, tm, LANES), lambda bi, i: (0, bi, 0, i, 0)),
            pl.BlockSpec((1, NSA_GROUPS, tm, LANES), lambda bi, i: (bi, 0, i, 0)),
        ],
        out_shape=[
            jax.ShapeDtypeStruct((b, s, Q_DIM), F32),
            jax.ShapeDtypeStruct((N_BRANCH, b, NSA_GROUPS, s, LANES), F32),
            jax.ShapeDtypeStruct((b, NSA_GROUPS, s, LANES), F32),
        ],
        compiler_params=_cparams(2),
        name="nsa_proj",
    )(x, gain, sc, sh, w)


def _k_lane_norm(t, gain_ext):
    lane = lax.broadcasted_iota(jnp.int32, t.shape, 1)
    ms = jnp.sum(jnp.where(lane < HEAD_DIM, t * t, 0.0), axis=-1, keepdims=True) * (1.0 / HEAD_DIM)
    return t * lax.rsqrt(ms + EPS) * gain_ext


def _compress_weights(pos, w1, w2):
    hidden = w1.shape[-1]
    w1p = w1.reshape(2, 2, CMP_STRIDE, HEAD_DIM, hidden)
    z = jnp.zeros_like(w1p[0])
    wk = jnp.concatenate([w1p[0], z], axis=-1)
    wv = jnp.concatenate([z, w1p[1]], axis=-1)
    w1_blk = jnp.concatenate([wk, wv], axis=2).astype(BF16)
    pos_blk = jnp.concatenate([pos[0], pos[1]], axis=-1).reshape(2, CMP_STRIDE, 1, LANES)
    z2 = jnp.zeros_like(w2[0])
    w2_blk = jnp.concatenate([jnp.concatenate([w2[0], z2], axis=1),
                              jnp.concatenate([z2, w2[1]], axis=1)], axis=0).astype(BF16)
    return pos_blk, w1_blk, w2_blk


def _compress_kernel(kv_ref, pos_ref, w1_ref, w2_ref, kg_ref, o_ref):
    g, seq = kv_ref.shape[2:4]
    nch = seq // CMP_STRIDE
    rows = g * nch
    first = None
    second = None
    for p in range(CMP_STRIDE):
        tok = jnp.concatenate([kv_ref[0, 0, gi, pl.ds(p, nch, stride=CMP_STRIDE), :] for gi in range(g)], axis=0)
        fa = jnp.dot((tok + pos_ref[0, p]).astype(BF16), w1_ref[0, p], preferred_element_type=F32)
        fb = jnp.dot((tok + pos_ref[1, p]).astype(BF16), w1_ref[1, p], preferred_element_type=F32)
        first = fa if first is None else first + fa
        second = fb if second is None else second + fb
    h1 = first + pltpu.roll(second, rows - 1, 0)
    out = jnp.dot(jax.nn.silu(h1).astype(BF16), w2_ref[...], preferred_element_type=F32)
    lane = lax.broadcasted_iota(jnp.int32, out.shape, 1)
    out = jnp.where(lane < HEAD_DIM, _k_lane_norm(out, kg_ref[0:1]), out)
    r = lax.broadcasted_iota(jnp.int32, out.shape, 0)
    out = jnp.where((r & (nch - 1)) == nch - 1, 0.0, out)
    o_ref[0] = out.reshape(g, nch, LANES)


def _compress(kv, pos_blk, w1_blk, w2_blk, kg_ext):
    _, b, g, s, _ = kv.shape
    nch = s // CMP_STRIDE
    assert nch & (nch - 1) == 0
    return pl.pallas_call(
        _compress_kernel,
        grid=(b,),
        in_specs=[
            pl.BlockSpec((1, 1, g, s, LANES), lambda bi: (0, bi, 0, 0, 0)),
            pl.BlockSpec(pos_blk.shape, lambda bi: (0, 0, 0, 0)),
            pl.BlockSpec(w1_blk.shape, lambda bi: (0, 0, 0, 0)),
            pl.BlockSpec(w2_blk.shape, lambda bi: (0, 0)),
            pl.BlockSpec((N_BRANCH, LANES), lambda bi: (0, 0)),
        ],
        out_specs=pl.BlockSpec((1, g, nch, LANES), lambda bi: (bi, 0, 0, 0)),
        out_shape=jax.ShapeDtypeStruct((b, g, nch, LANES), F32),
        compiler_params=_cparams(1),
        name="nsa_compress",
    )(kv, pos_blk, w1_blk, w2_blk, kg_ext)


def _sel_slot_count(nqb):
    chunks = lambda i: -(-(i + 1) * Q_BLOCK // SEL_CHUNK)
    counts = {chunks(p) + chunks(nqb - 1 - p) for p in range(nqb // 2)}
    assert len(counts) == 1, counts
    return counts.pop()


def _tile_scores(k_tile, q_t):
    return jnp.concatenate([jnp.dot(k_tile, q_t[:, h:h + MXU_WIDTH], preferred_element_type=F32)
                            for h in range(0, q_t.shape[1], MXU_WIDTH)], axis=1)


def _tile_softmax(s, bias, vt_tile):
    if bias is not None:
        rows = bias.shape[0]
        biased = s[:rows] + jnp.concatenate([bias] * (s.shape[1] // Q_BLOCK), axis=1)
        s = biased if rows == s.shape[0] else jnp.concatenate([biased, s[rows:]], axis=0)
    m = jnp.max(s, axis=0, keepdims=True)
    p = jnp.exp2(s - m)
    ol = jnp.dot(vt_tile, p.astype(BF16), preferred_element_type=F32)
    return m, ol[HEAD_DIM:HEAD_DIM + 1], ol[:HEAD_DIM]


def _merge_tiles(parts):
    m = parts[0][0]
    for mi, _, _ in parts[1:]:
        m = jnp.maximum(m, mi)
    l = None
    o = None
    for mi, li, oi in parts:
        w = jnp.exp2(mi - m)
        l = w * li if l is None else l + w * li
        o = w * oi if o is None else o + w * oi
    return o / l


def _attn_kernel(q_ref, gate_ref, kvc_ref, ks_ref, kw_ref, selmap_ref, qg_ref, kg_ref, o_ref,
                 ksn, kwn, vst, vwt, qt_scr, slot_m, slot_l, slot_o):
    pair = pl.program_id(2)
    seq = ks_ref.shape[3]
    ncmp = kvc_ref.shape[2]
    nsel = selmap_ref.shape[0]
    nqb = seq // Q_BLOCK
    cols = NSA_HPG * Q_BLOCK
    win_tiles = WINDOW // LANES
    chunk_tiles = SEL_CHUNK // LANES
    scale = HEAD_DIM ** -0.5 * LOG2_E

    @pl.when(pair == 0)
    def _prepare_kv():
        lane = lax.broadcasted_iota(jnp.int32, (LANES, LANES), 1)
        row = lax.broadcasted_iota(jnp.int32, (LANES, LANES), 0)
        ones_row = (lax.broadcasted_iota(jnp.int32, (VT_ROWS - HEAD_DIM, LANES), 0) == 0).astype(BF16)
        pad_mark = (lane == PAD_LANE).astype(BF16)
        for w in range(win_tiles):
            kwn[w * LANES:(w + 1) * LANES, :] = pad_mark
        vwt[0:win_tiles] = jnp.zeros((win_tiles, VT_ROWS, LANES), BF16)

        def body(c, carry):
            r0 = pl.multiple_of(c * LANES, LANES)
            ts = ks_ref[0, 0, 0, pl.ds(r0, LANES), :]
            in_block = (lane - BLOCK_LANE0 == lax.shift_right_logical(r0 + row, SEL_BLOCK.bit_length() - 1))
            ksn[pl.ds(r0, LANES), :] = (_k_lane_norm(ts, kg_ref[1:2]) + in_block.astype(F32)).astype(BF16)
            vst[c] = jnp.concatenate([ts.T[HEAD_DIM:, :].astype(BF16), ones_row], axis=0)
            tw = kw_ref[0, 0, 0, pl.ds(r0, LANES), :]
            kwn[pl.ds(WINDOW + r0, LANES), :] = _k_lane_norm(tw, kg_ref[2:3]).astype(BF16)
            vwt[win_tiles + c] = jnp.concatenate([tw.T[HEAD_DIM:, :].astype(BF16), ones_row], axis=0)
            return carry

        lax.fori_loop(0, seq // LANES, body, 0)

    kc = kvc_ref[0, 0]
    kc_b = kc.astype(BF16)
    vc_t = kc.T[HEAD_DIM:, :].astype(BF16)
    block_ids = (pair, nqb - 1 - pair)

    def block_branches(idx):
        i = block_ids[idx]
        t0 = i * Q_BLOCK
        row0 = pl.multiple_of(t0, Q_BLOCK)

        q = q_ref[0, pl.ds(row0, Q_BLOCK), :]
        heads = []
        for h in range(NSA_HPG // 2):
            qt = q[:, h * LANES:(h + 1) * LANES].T
            for a in range(2):
                x = qt[a * HEAD_DIM:(a + 1) * HEAD_DIM]
                ms = jnp.mean(x * x, axis=0, keepdims=True)
                heads.append(x * lax.rsqrt(ms + EPS) * qg_ref[...] * scale)
        q_top = jnp.concatenate(heads, axis=1).astype(BF16)
        mask_rows = lax.broadcasted_iota(jnp.int32, (LANES - HEAD_DIM, cols), 0) + HEAD_DIM
        q_plain = jnp.concatenate([q_top, jnp.zeros((LANES - HEAD_DIM, cols), BF16)], axis=0)
        q_win = jnp.concatenate([q_top, jnp.where(mask_rows == PAD_LANE, NEG_BIG, 0.0).astype(BF16)], axis=0)

        s = _tile_scores(kc_b, q_plain)
        tq = t0 + (lax.broadcasted_iota(jnp.int32, (ncmp, cols), 1) & (Q_BLOCK - 1))
        cmp_end = lax.broadcasted_iota(jnp.int32, (ncmp, cols), 0) * CMP_STRIDE + (CMP_BLOCK - 1)
        visible = cmp_end <= tq
        s = jnp.where(visible, s, NEG_BIG)
        m = jnp.max(s, axis=0, keepdims=True)
        p = jnp.where(visible, jnp.exp2(s - m), 0.0)
        l = jnp.sum(p, axis=0, keepdims=True)
        p_c = p / jnp.maximum(l, jnp.finfo(F32).tiny)
        o_c = jnp.dot(vc_t, p_c.astype(BF16), preferred_element_type=F32)

        p_sum = p_c[:, 0:Q_BLOCK]
        for n in range(1, NSA_HPG):
            p_sum = p_sum + p_c[:, n * Q_BLOCK:(n + 1) * Q_BLOCK]
        imp = jnp.dot(selmap_ref[...], p_sum, preferred_element_type=F32, precision=lax.Precision.HIGHEST)
        blk = lax.broadcasted_iota(jnp.int32, (nsel, Q_BLOCK), 0)
        cur = lax.shift_right_logical(t0 + lax.broadcasted_iota(jnp.int32, (nsel, Q_BLOCK), 1), 6)
        valid = blk <= cur
        forced = (blk == 0) | (blk == cur) | (blk == cur - 1)
        val = jnp.where(valid, imp, -jnp.inf)
        val = jnp.where(forced, jnp.inf, val)
        rank = jnp.zeros((nsel, Q_BLOCK), jnp.int32)
        for j2 in range(nsel):
            row = val[j2:j2 + 1, :]
            beats = (row > val) | ((row == val) & (blk > j2))
            rank = rank + beats.astype(jnp.int32)
        picked = jnp.where((rank < SEL_TOP_N) & valid, 0.0, NEG_BIG).astype(BF16)
        qt_scr[idx] = jnp.concatenate(
            [q_top, jnp.concatenate([picked] * NSA_HPG, axis=1),
             jnp.zeros((LANES - HEAD_DIM - nsel, cols), BF16)], axis=0)

        tiles = []
        off = 0
        span = WINDOW + Q_BLOCK
        while off < span:
            kc_w = min(SEL_CHUNK, span - off)
            if off < Q_BLOCK:
                rows_b = min(kc_w, Q_BLOCK - off)
                newest = False
            elif off + kc_w > WINDOW:
                assert off >= WINDOW
                rows_b = kc_w
                newest = True
            else:
                rows_b = 0
                newest = False

            def scores(off=off, kc_w=kc_w):
                k_tile = kwn[pl.ds(pl.multiple_of(t0 + off, LANES), kc_w), :]
                return _tile_scores(k_tile, q_win)

            def finish(s, off=off, kc_w=kc_w, rows_b=rows_b, newest=newest):
                bias = None
                if rows_b:
                    r = off + lax.broadcasted_iota(jnp.int32, (rows_b, Q_BLOCK), 0)
                    c = lax.broadcasted_iota(jnp.int32, (rows_b, Q_BLOCK), 1)
                    bias = jnp.where((r - WINDOW <= c) if newest else (r > c), 0.0, NEG_BIG)
                vt_tile = jnp.concatenate([vwt[i + off // LANES + cc] for cc in range(kc_w // LANES)], axis=1)
                return _tile_softmax(s, bias, vt_tile)

            tiles.append((scores, finish))
            off += kc_w
        gate_t = gate_ref[0, 0, pl.ds(row0, Q_BLOCK), :].T
        return o_c, gate_t, tiles

    per_block = [block_branches(idx) for idx in range(2)]
    n_win = len(per_block[0][2])

    n_slots = _sel_slot_count(nqb)
    n0 = lax.shift_right_logical(block_ids[0] * Q_BLOCK + Q_BLOCK + SEL_CHUNK - 1, SEL_CHUNK.bit_length() - 1)
    n0_max = (nqb // 2 * Q_BLOCK + SEL_CHUNK - 1) // SEL_CHUNK
    key_minus_query = (lax.broadcasted_iota(jnp.int32, (SEL_CHUNK, Q_BLOCK), 0)
                       - lax.broadcasted_iota(jnp.int32, (SEL_CHUNK, Q_BLOCK), 1))
    sel_tiles = []
    for slot in range(n_slots):
        last_of_block = slot == 0 or slot == n_slots - 1
        if slot == 0:
            which, kb = 0, n0 - 1
        elif slot == n_slots - 1:
            which, kb = 1, n_slots - n0 - 1
        elif slot >= n0_max:
            which, kb = 1, slot - n0
        else:
            which = (slot >= n0).astype(jnp.int32)
            kb = slot - 1 + which * (1 - n0)
        k0 = pl.multiple_of(kb * SEL_CHUNK, SEL_CHUNK)

        def scores(which=which, k0=k0):
            return _tile_scores(ksn[pl.ds(k0, SEL_CHUNK), :], qt_scr[which])

        def finish(s, which=which, kb=kb, k0=k0, last_of_block=last_of_block):
            bias = None
            if last_of_block:
                t0 = block_ids[which] * Q_BLOCK
                bias = jnp.where(key_minus_query <= t0 - k0, 0.0, NEG_BIG)
            vt_tile = jnp.concatenate([vst[kb * chunk_tiles + c] for c in range(chunk_tiles)], axis=1)
            return _tile_softmax(s, bias, vt_tile)

        sel_tiles.append((scores, finish))

    all_tiles = per_block[0][2] + per_block[1][2] + sel_tiles
    results = []
    pending = [tile[0]() for tile in all_tiles[:SCORE_LOOKAHEAD]]
    for t, (_, finish) in enumerate(all_tiles):
        if t + SCORE_LOOKAHEAD < len(all_tiles):
            pending.append(all_tiles[t + SCORE_LOOKAHEAD][0]())
        results.append(finish(pending.pop(0)))
    for slot in range(n_slots):
        m, l, o = results[2 * n_win + slot]
        slot_m[slot] = m
        slot_l[slot] = l
        slot_o[slot] = o

    for idx in range(2):
        o_c, gate_t, _ = per_block[idx]
        o_w = _merge_tiles(results[idx * n_win:(idx + 1) * n_win])
        slots = range(0, n0_max) if idx == 0 else range(1, n_slots)
        parts = []
        for slot in slots:
            if slot == 0 or slot >= n0_max:
                m = slot_m[slot]
            else:
                mine = (slot < n0) if idx == 0 else (slot >= n0)
                m = jnp.where(mine, slot_m[slot], NEG_BIG)
            parts.append((m, slot_l[slot], slot_o[slot]))
        o_s = _merge_tiles(parts)
        row0 = pl.multiple_of(block_ids[idx] * Q_BLOCK, Q_BLOCK)
        for hh in range(NSA_HPG // 2):
            tiles = []
            for a in range(2):
                n = 2 * hh + a
                lanes = slice(n * Q_BLOCK, (n + 1) * Q_BLOCK)
                c0 = N_BRANCH * n
                tiles.append(gate_t[c0:c0 + 1, :] * o_c[:, lanes]
                             + gate_t[c0 + 1:c0 + 2, :] * o_s[:, lanes]
                             + gate_t[c0 + 2:c0 + 3, :] * o_w[:, lanes])
            o_ref[0, pl.ds(row0, Q_BLOCK), hh * LANES:(hh + 1) * LANES] = jnp.concatenate(tiles, axis=0).T


def _sel_map_t(seq):
    n_c = seq // CMP_STRIDE - 1
    n_s = seq // SEL_BLOCK
    tok = np.arange(seq)
    start = np.arange(n_c) * CMP_STRIDE
    cover_c = (tok[None, :] >= start[:, None]) & (tok[None, :] < start[:, None] + CMP_BLOCK)
    cover_s = (tok[:, None] // SEL_BLOCK) == np.arange(n_s)[None, :]
    m = cover_c.astype(np.float32) @ cover_s.astype(np.float32) / np.float32(CMP_BLOCK)
    out = np.zeros((n_s, n_c + 1), np.float32)
    out[:, :n_c] = m.T
    return jnp.asarray(out)


def _attention(q, gates, kvc, kv, qg_t, kg_ext):
    b, s, _ = q.shape
    g = NSA_GROUPS
    nqb = s // Q_BLOCK
    ncmp = s // CMP_STRIDE
    nsel = s // SEL_BLOCK
    assert s % SEL_CHUNK == 0 and nsel % SUBLANES == 0 and nqb % 2 == 0
    gw = NSA_HPG * HEAD_DIM
    cols = NSA_HPG * Q_BLOCK
    n_slots = _sel_slot_count(nqb)

    def kv_spec(j):
        return pl.BlockSpec((1, 1, 1, s, LANES), lambda bi, gi, i: (j, bi, gi, 0, 0))

    return pl.pallas_call(
        _attn_kernel,
        grid=(b, g, nqb // 2),
        in_specs=[
            pl.BlockSpec((1, s, gw), lambda bi, gi, i: (bi, 0, gi)),
            pl.BlockSpec((1, 1, s, LANES), lambda bi, gi, i: (bi, gi, 0, 0)),
            pl.BlockSpec((1, 1, ncmp, LANES), lambda bi, gi, i: (bi, gi, 0, 0)),
            kv_spec(1), kv_spec(2),
            pl.BlockSpec((nsel, ncmp), lambda bi, gi, i: (0, 0)),
            pl.BlockSpec((HEAD_DIM, Q_BLOCK), lambda bi, gi, i: (0, 0)),
            pl.BlockSpec((N_BRANCH, LANES), lambda bi, gi, i: (0, 0)),
        ],
        out_specs=pl.BlockSpec((1, s, gw), lambda bi, gi, i: (bi, 0, gi)),
        out_shape=jax.ShapeDtypeStruct((b, s, Q_DIM), F32),
        scratch_shapes=[
            pltpu.VMEM((s, LANES), BF16),
            pltpu.VMEM((WINDOW + s, LANES), BF16),
            pltpu.VMEM((s // LANES, VT_ROWS, LANES), BF16),
            pltpu.VMEM(((WINDOW + s) // LANES, VT_ROWS, LANES), BF16),
            pltpu.VMEM((2, LANES, cols), BF16),
            pltpu.VMEM((n_slots, 1, cols), F32),
            pltpu.VMEM((n_slots, 1, cols), F32),
            pltpu.VMEM((n_slots, HEAD_DIM, cols), F32),
        ],
        compiler_params=_cparams(3),
        name="nsa_attention",
    )(q, gates, kvc, kv, kv, _sel_map_t(s), qg_t, kg_ext)


def _ffn_core(x, g_ref, sc_ref, sh_ref, gate_ref, wg_ref, wu_ref, wo_ref):
    hh = _norm_mod(x, g_ref[...], sc_ref[0], sh_ref[0]).astype(BF16)
    a = jnp.dot(hh, wg_ref[...], preferred_element_type=F32)
    u = jnp.dot(hh, wu_ref[...], preferred_element_type=F32)
    act = (jax.nn.silu(a) * u).astype(BF16)
    return x + gate_ref[0] * jnp.dot(act, wo_ref[...], preferred_element_type=F32)


def _ffn_kernel(x_ref, g_ref, sc_ref, sh_ref, gate_ref, wg_ref, wu_ref, wo_ref, o_ref):
    o_ref[0] = _ffn_core(x_ref[0], g_ref, sc_ref, sh_ref, gate_ref, wg_ref, wu_ref, wo_ref)


def _mix_ffn_kernel(y_ref, wmix_ref, mix_gate_ref, x_ref, g_ref, sc_ref, sh_ref, gate_ref, wg_ref, wu_ref, wo_ref,
                    o_ref):
    mix = jnp.dot(y_ref[0].astype(BF16), wmix_ref[...], preferred_element_type=F32)
    x = x_ref[0] + mix_gate_ref[0] * mix
    o_ref[0] = _ffn_core(x, g_ref, sc_ref, sh_ref, gate_ref, wg_ref, wu_ref, wo_ref)


def _ffn(x, gain, sc, sh, gate, w_in, w_out, layer, mixer_out=None):
    b, s, d = x.shape
    hidden = w_out.shape[1]
    tm = min(ROW_TILE, s)
    vec = pl.BlockSpec((1, 1, d), lambda bi, i: (bi, 0, 0))
    rows = lambda width: pl.BlockSpec((1, tm, width), lambda bi, i: (bi, i, 0))
    in_specs = [
        rows(d), _resident((1, d)), vec, vec, vec,
        _layer_plane((d, hidden), layer, 0), _layer_plane((d, hidden), layer, 1),
        _layer_plane((hidden, d), layer),
    ]
    args = (x, gain, sc, sh, gate, w_in, w_in, w_out)
    body = _ffn_kernel
    if mixer_out is not None:
        y, w_mix, j, mix_gate = mixer_out
        in_specs = [rows(y.shape[-1]), _layer_plane(w_mix.shape[1:], j), vec] + in_specs
        args = (y, w_mix, mix_gate) + args
        body = _mix_ffn_kernel
    return pl.pallas_call(
        body,
        grid=(b, s // tm),
        in_specs=in_specs,
        out_specs=rows(d),
        out_shape=jax.ShapeDtypeStruct((b, s, d), F32),
        compiler_params=_cparams(2),
        name="ffn",
    )(*args)


def _sigmoid(x):
    return 0.5 * jnp.tanh(0.5 * x) + 0.5


def _band_starts(width, block_w):
    band = 2 * MXU_WIDTH
    starts = []
    for c0 in range(0, width, MXU_WIDTH):
        c1 = min(c0 + MXU_WIDTH, width)
        lo = (c0 // block_w) * block_w
        hi = ((c1 - 1) // block_w + 1) * block_w
        k0 = min((lo // LANES) * LANES, width - band)
        assert k0 >= 0 and k0 <= lo and hi <= k0 + band
        starts.append(k0)
    return starts


def _band_weights(w_a, w_x):
    nb, bw, _ = w_a.shape
    width = nb * bw
    eye = jnp.eye(nb, dtype=w_a.dtype)
    dense = lambda w: jnp.einsum('nde,nm->ndme', w, eye).reshape(width, width)
    da, dx = dense(w_a), dense(w_x)
    band = 2 * MXU_WIDTH
    tiles = []
    for t, k0 in enumerate(_band_starts(width, bw)):
        c0 = t * MXU_WIDTH
        c1 = min(c0 + MXU_WIDTH, width)
        pad = ((0, 0), (0, MXU_WIDTH - (c1 - c0)))
        tiles.append(jnp.concatenate(
            [jnp.pad(da[k0:k0 + band, c0:c1], pad), jnp.pad(dx[k0:k0 + band, c0:c1], pad)], axis=1))
    return jnp.stack(tiles).astype(BF16)


def _rg_mixer_kernel(x_ref, g_ref, sc_ref, sh_ref, res_gate_ref, win_ref, cw_ref, cb_ref, wb_ref, ba_ref,
                     bx_ref, lam_ref, wout_ref, o_ref,
                     u_halo, h_scr, a_scr, b_scr, hs_scr, gate_scr, y_scr, *, band_starts):
    step = pl.program_id(0)
    nb, ts, d = x_ref.shape
    width = wout_ref.shape[0]
    rows = nb * ts
    halo = CONV_WIDTH - 1
    band = 2 * MXU_WIDTH

    @pl.when(step == 0)
    def _init():
        u_halo[0:halo] = jnp.zeros((halo, nb, width), F32)
        h_scr[...] = jnp.zeros(h_scr.shape, F32)

    hh = _norm_mod(x_ref[...], g_ref[...], sc_ref[...], sh_ref[...]).reshape(rows, d)
    proj = jnp.dot(hh.astype(BF16), win_ref[...], preferred_element_type=F32)
    gate_scr[...] = proj[:, :width]

    u_halo[halo:] = jnp.swapaxes(proj[:, width:].reshape(nb, ts, width), 0, 1)
    uc = jnp.zeros((ts, nb, width), F32) + cb_ref[...]
    for k in range(CONV_WIDTH):
        uc = uc + cw_ref[k:k + 1, :] * u_halo[k:k + ts]
    u_halo[0:halo] = u_halo[ts:ts + halo]
    uc = uc.reshape(rows, width)
    ub = uc.astype(BF16)

    z = -lam_ref[...]
    log_a_unit = -RG_C * (jnp.maximum(z, 0.0) + jnp.log1p(jnp.exp(-jnp.abs(z))))

    for t, k0 in enumerate(band_starts):
        c0 = t * MXU_WIDTH
        c1 = min(c0 + MXU_WIDTH, width)
        n = c1 - c0
        zz = jnp.dot(ub[:, k0:k0 + band], wb_ref[t], preferred_element_type=F32)
        r = _sigmoid(zz[:, :n] + ba_ref[:, c0:c1])
        ig = _sigmoid(zz[:, MXU_WIDTH:MXU_WIDTH + n] + bx_ref[:, c0:c1])
        log_a = log_a_unit[:, c0:c1] * r
        a = jnp.exp(log_a)
        bb = jnp.sqrt(-jnp.tanh(log_a) * (a * a + 1.0)) * (ig * uc[:, c0:c1])
        for ct in range(n // LANES):
            lanes = slice(ct * LANES, (ct + 1) * LANES)
            a_scr[c0 // LANES + ct] = a[:, lanes]
            b_scr[c0 // LANES + ct] = bb[:, lanes]

    n_ct = width // LANES

    def scan_step(t, h):
        rows_t = pl.ds(pl.multiple_of(t * nb, nb), nb)
        new = []
        for ct in range(n_ct):
            hc = a_scr[ct, rows_t, :] * h[ct] + b_scr[ct, rows_t, :]
            hs_scr[ct, rows_t, :] = hc
            new.append(hc)
        return tuple(new)

    h_fin = lax.fori_loop(0, ts, scan_step, tuple(h_scr[ct] for ct in range(n_ct)), unroll=SUBLANES)
    for ct in range(n_ct):
        lanes = slice(ct * LANES, (ct + 1) * LANES)
        h_scr[ct] = h_fin[ct]
        hs = jnp.swapaxes(hs_scr[ct].reshape(ts, nb, LANES), 0, 1)
        y_scr[:, lanes] = (hs.reshape(rows, LANES) * jax.nn.gelu(gate_scr[:, lanes])).astype(BF16)
    mix = jnp.dot(y_scr[...], wout_ref[...], preferred_element_type=F32)
    o_ref[...] = x_ref[...] + res_gate_ref[...] * mix.reshape(nb, ts, d)


def _rg_mixer(x, gain, sc, sh, res_gate, w_in, conv_w, conv_b, w_band, b_a, b_x, lam, w_out, j, block_w):
    b, s, d = x.shape
    width = w_out.shape[1]
    assert b == SUBLANES
    ts = min(RG_TIME_TILE, s)
    rows = b * ts
    band_starts = tuple(_band_starts(width, block_w))
    tile = pl.BlockSpec((b, ts, d), lambda t: (0, t, 0))
    mod = _resident((b, 1, d))
    vec = _resident((1, width))
    kern = functools.partial(_rg_mixer_kernel, band_starts=band_starts)
    return pl.pallas_call(
        kern,
        grid=(s // ts,),
        in_specs=[
            tile, _resident((1, d)), mod, mod, mod,
            _layer_plane(w_in.shape[1:], j),
            _resident((CONV_WIDTH, width)), vec,
            _resident(w_band.shape),
            vec, vec, vec,
            _layer_plane(w_out.shape[1:], j),
        ],
        out_specs=tile,
        out_shape=jax.ShapeDtypeStruct((b, s, d), F32),
        scratch_shapes=[
            pltpu.VMEM((ts + CONV_WIDTH - 1, b, width), F32),
            pltpu.VMEM((width // LANES, b, LANES), F32),
            pltpu.VMEM((width // LANES, rows, LANES), F32),
            pltpu.VMEM((width // LANES, rows, LANES), F32),
            pltpu.VMEM((width // LANES, rows, LANES), F32),
            pltpu.VMEM((rows, width), F32),
            pltpu.VMEM((rows, width), BF16),
        ],
        compiler_params=_cparams(1),
        name="rg_mixer",
    )(x, gain, sc, sh, res_gate, w_in, conv_w, conv_b, w_band, b_a, b_x, lam, w_out)


def kernel(x, c, ada_w, ada_b, norm1_g, norm2_g, nsa_w_in, nsa_w_out, nsa_cmp_pos, nsa_cmp_w1, nsa_cmp_w2, nsa_q_gain, nsa_k_gain, rg_w_in, rg_conv_w, rg_conv_b, rg_w_a, rg_b_a, rg_w_x, rg_b_x, rg_lam, rg_w_out, ffn_w_in, ffn_w_out):
    depth, d, _ = ada_w.shape
    b = x.shape[0]
    mod = _ada_mod(c, ada_w, ada_b).reshape(depth, b, 6, 1, d)
    ffn_w_in, ffn_w_out, nsa_w_out, rg_w_in, rg_w_out = (
        w.astype(BF16) for w in (ffn_w_in, ffn_w_out, nsa_w_out, rg_w_in, rg_w_out))
    for layer in range(depth):
        sh1, sc1, g1, sh2, sc2, g2 = (mod[layer, :, k] for k in range(6))
        n1 = norm1_g[layer].reshape(1, d)
        n2 = norm2_g[layer].reshape(1, d)
        j = layer // N_MIXERS
        if layer % N_MIXERS == 0:
            kg_ext = jnp.concatenate([nsa_k_gain[j], jnp.zeros_like(nsa_k_gain[j])], axis=1)
            qg_t = jnp.broadcast_to(nsa_q_gain[j][:, None], (HEAD_DIM, Q_BLOCK))
            q, kv, gates = _nsa_proj(x, n1, sc1, sh1, _nsa_weight_layout(nsa_w_in[j]))
            kvc = _compress(kv, *_compress_weights(nsa_cmp_pos[j], nsa_cmp_w1[j], nsa_cmp_w2[j]), kg_ext)
            o = _attention(q, gates, kvc, kv, qg_t, kg_ext)
            mixer_out = (o, nsa_w_out, j, g1)
        else:
            width = rg_w_out.shape[1]
            block_w = rg_w_a.shape[-1]
            x = _rg_mixer(x, n1, sc1, sh1, g1, rg_w_in,
                          rg_conv_w[j], rg_conv_b[j].reshape(1, width),
                          _band_weights(rg_w_a[j], rg_w_x[j]),
                          rg_b_a[j].reshape(1, width), rg_b_x[j].reshape(1, width),
                          rg_lam[j].reshape(1, width), rg_w_out, j, block_w)
            mixer_out = None
        x = _ffn(x, n2, sc2, sh2, g2, ffn_w_in, ffn_w_out, layer, mixer_out)
    return x
```

```python
import functools

import numpy as np
import jax
import jax.numpy as jnp
from jax import lax
from jax.experimental import pallas as pl
from jax.experimental.pallas import tpu as pltpu

F32 = jnp.float32
BF16 = jnp.bfloat16

EPS = 1e-6
N_MIXERS = 2

NSA_HEADS = 16
NSA_GROUPS = 4
NSA_HPG = NSA_HEADS // NSA_GROUPS
HEAD_DIM = 64
CMP_STRIDE = 16
CMP_BLOCK = 2 * CMP_STRIDE
SEL_BLOCK = 64
SEL_TOP_N = 8
WINDOW = 512
Q_BLOCK = 128
N_BRANCH = 3
Q_DIM = NSA_HEADS * HEAD_DIM
KV_DIM = NSA_GROUPS * HEAD_DIM
GATE_COLS = N_BRANCH * NSA_HEADS
GATE_PER_GROUP = N_BRANCH * NSA_HPG

RNN_BLOCKS = 16
CONV_WIDTH = 4
RG_C = 8.0

LANES = 128
SUBLANES = 8
MXU_WIDTH = 256
VMEM_LIMIT_BYTES = 56 * 1024 * 1024

NEG_BIG = -1e30
LOG2_E = 1.4426950408889634
BLOCK_LANE0 = HEAD_DIM
PAD_LANE = LANES - 1
SHIFT_LANE = LANES - 2
MAX_SHIFTED_SCORE_RANGE = 100.0
VT_ROWS = HEAD_DIM + 16
SEL_CHUNK = 256
SCORE_LOOKAHEAD = 3
RG_TIME_TILE = 64
ROW_TILE = 512


def _cparams(n_axes, flags=None):
    return pltpu.CompilerParams(
        dimension_semantics=("arbitrary",) * n_axes,
        vmem_limit_bytes=VMEM_LIMIT_BYTES,
        flags=flags,
    )


def _resident(shape):
    return pl.BlockSpec(shape, lambda *_: (0,) * len(shape), pipeline_mode=pl.Buffered(1))


def _layer_plane(shape, layer, col_block=0):
    return pl.BlockSpec((None,) + tuple(shape), lambda *_: (layer, 0, col_block), pipeline_mode=pl.Buffered(1))


def _norm_mod(x, gain, scale, shift):
    ms = jnp.mean(x * x, axis=-1, keepdims=True)
    y = x * lax.rsqrt(ms + EPS) * gain
    return y * (1.0 + scale) + shift


def _ada_kernel(c_ref, w_ref, b_ref, o_ref):
    cond = jax.nn.silu(c_ref[...])
    o_ref[0] = jnp.dot(cond, w_ref[0], preferred_element_type=F32,
                       precision=lax.Precision.HIGHEST) + b_ref[0]


def _ada_mod(c, ada_w, ada_b):
    depth, d, n = ada_w.shape
    b = c.shape[0]
    tn = n // 4
    return pl.pallas_call(
        _ada_kernel,
        grid=(depth, n // tn),
        in_specs=[
            pl.BlockSpec((b, d), lambda l, j: (0, 0)),
            pl.BlockSpec((1, d, tn), lambda l, j: (l, 0, j)),
            pl.BlockSpec((1, 1, tn), lambda l, j: (l, 0, j)),
        ],
        out_specs=pl.BlockSpec((1, b, tn), lambda l, j: (l, 0, j)),
        out_shape=jax.ShapeDtypeStruct((depth, b, n), F32),
        compiler_params=_cparams(2),
        name="ada_mod",
    )(c, ada_w, ada_b.reshape(depth, 1, n))


def _nsa_weight_layout(w_in):
    d = w_in.shape[0]
    kv = w_in[:, Q_DIM:Q_DIM + 6 * KV_DIM].reshape(d, N_BRANCH, 2, NSA_GROUPS, HEAD_DIM)
    kv = jnp.transpose(kv, (0, 1, 3, 2, 4)).reshape(d, 6 * KV_DIM)
    gates = jnp.pad(w_in[:, Q_DIM + 6 * KV_DIM:], ((0, 0), (0, LANES - GATE_COLS)))
    return jnp.concatenate([w_in[:, :Q_DIM], kv, gates], axis=1).astype(BF16)


def _nsa_proj_kernel(x_ref, g_ref, sc_ref, sh_ref, w_ref, q_ref, kv_ref, gate_ref):
    hh = _norm_mod(x_ref[0], g_ref[...], sc_ref[0], sh_ref[0])
    p = jnp.dot(hh.astype(BF16), w_ref[...], preferred_element_type=F32)
    q_ref[0] = p[:, :Q_DIM]
    for j in range(N_BRANCH):
        for g in range(NSA_GROUPS):
            lo = Q_DIM + (j * NSA_GROUPS + g) * LANES
            kv_ref[j, 0, g] = p[:, lo:lo + LANES]
    gate = jax.nn.sigmoid(p[:, Q_DIM + 6 * KV_DIM:])
    for g in range(NSA_GROUPS):
        shift = (LANES - GATE_PER_GROUP * g) % LANES
        gate_ref[0, g] = gate if shift == 0 else pltpu.roll(gate, shift, 1)


def _nsa_proj(x, gain, sc, sh, w):
    b, s, d = x.shape
    tm = min(ROW_TILE, s)
    n = w.shape[1]
    vec = pl.BlockSpec((1, 1, d), lambda bi, i: (bi, 0, 0))
    return pl.pallas_call(
        _nsa_proj_kernel,
        grid=(b, s // tm),
        in_specs=[
            pl.BlockSpec((1, tm, d), lambda bi, i: (bi, i, 0)),
            pl.BlockSpec((1, d), lambda bi, i: (0, 0)),
            vec, vec,
            pl.BlockSpec((d, n), lambda bi, i: (0, 0)),
        ],
        out_specs=[
            pl.BlockSpec((1, tm, Q_DIM), lambda bi, i: (bi, i, 0)),
            pl.BlockSpec((N_BRANCH, 1, NSA_GROUPS, tm, LANES), lambda bi, i: (0, bi, 0, i, 0)),
            pl.BlockSpec((1, NSA_GROUPS, tm, LANES), lambda bi, i: (bi, 0, i, 0)),
        ],
        out_shape=[
            jax.ShapeDtypeStruct((b, s, Q_DIM), F32),
            jax.ShapeDtypeStruct((N_BRANCH, b, NSA_GROUPS, s, LANES), F32),
            jax.ShapeDtypeStruct((b, NSA_GROUPS, s, LANES), F32),
        ],
        compiler_params=_cparams(2),
        name="nsa_proj",
    )(x, gain, sc, sh, w)


def _k_lane_norm(t, gain_ext):
    lane = lax.broadcasted_iota(jnp.int32, t.shape, 1)
    ms = jnp.sum(jnp.where(lane < HEAD_DIM, t * t, 0.0), axis=-1, keepdims=True) * (1.0 / HEAD_DIM)
    return t * lax.rsqrt(ms + EPS) * gain_ext


def _compress_weights(pos, w1, w2):
    hidden = w1.shape[-1]
    w1p = w1.reshape(2, 2, CMP_STRIDE, HEAD_DIM, hidden)
    z = jnp.zeros_like(w1p[0])
    wk = jnp.concatenate([w1p[0], z], axis=-1)
    wv = jnp.concatenate([z, w1p[1]], axis=-1)
    w1_blk = jnp.concatenate([wk, wv], axis=2).astype(BF16)
    pos_blk = jnp.concatenate([pos[0], pos[1]], axis=-1).reshape(2, CMP_STRIDE, 1, LANES)
    z2 = jnp.zeros_like(w2[0])
    w2_blk = jnp.concatenate([jnp.concatenate([w2[0], z2], axis=1),
                              jnp.concatenate([z2, w2[1]], axis=1)], axis=0).astype(BF16)
    return pos_blk, w1_blk, w2_blk


def _compress_kernel(kv_ref, pos_ref, w1_ref, w2_ref, kg_ref, o_ref):
    g, seq = kv_ref.shape[2:4]
    nch = seq // CMP_STRIDE
    rows = g * nch
    first = None
    second = None
    for p in range(CMP_STRIDE):
        tok = jnp.concatenate([kv_ref[0, 0, gi, pl.ds(p, nch, stride=CMP_STRIDE), :] for gi in range(g)], axis=0)
        fa = jnp.dot((tok + pos_ref[0, p]).astype(BF16), w1_ref[0, p], preferred_element_type=F32)
        fb = jnp.dot((tok + pos_ref[1, p]).astype(BF16), w1_ref[1, p], preferred_element_type=F32)
        first = fa if first is None else first + fa
        second = fb if second is None else second + fb
    h1 = first + pltpu.roll(second, rows - 1, 0)
    out = jnp.dot(jax.nn.silu(h1).astype(BF16), w2_ref[...], preferred_element_type=F32)
    lane = lax.broadcasted_iota(jnp.int32, out.shape, 1)
    out = jnp.where(lane < HEAD_DIM, _k_lane_norm(out, kg_ref[0:1]), out)
    r = lax.broadcasted_iota(jnp.int32, out.shape, 0)
    out = jnp.where((r & (nch - 1)) == nch - 1, 0.0, out)
    o_ref[0] = out.reshape(g, nch, LANES)


def _compress(kv, pos_blk, w1_blk, w2_blk, kg_ext):
    _, b, g, s, _ = kv.shape
    nch = s // CMP_STRIDE
    assert nch & (nch - 1) == 0
    return pl.pallas_call(
        _compress_kernel,
        grid=(b,),
        in_specs=[
            pl.BlockSpec((1, 1, g, s, LANES), lambda bi: (0, bi, 0, 0, 0)),
            pl.BlockSpec(pos_blk.shape, lambda bi: (0, 0, 0, 0)),
            pl.BlockSpec(w1_blk.shape, lambda bi: (0, 0, 0, 0)),
            pl.BlockSpec(w2_blk.shape, lambda bi: (0, 0)),
            pl.BlockSpec((N_BRANCH, LANES), lambda bi: (0, 0)),
        ],
        out_specs=pl.BlockSpec((1, g, nch, LANES), lambda bi: (bi, 0, 0, 0)),
        out_shape=jax.ShapeDtypeStruct((b, g, nch, LANES), F32),
        compiler_params=_cparams(1),
        name="nsa_compress",
    )(kv, pos_blk, w1_blk, w2_blk, kg_ext)


def _sel_slot_count(nqb):
    chunks = lambda i: -(-(i + 1) * Q_BLOCK // SEL_CHUNK)
    counts = {chunks(p) + chunks(nqb - 1 - p) for p in range(nqb // 2)}
    assert len(counts) == 1, counts
    return counts.pop()


def _tile_scores(k_tile, q_t):
    return jnp.concatenate([jnp.dot(k_tile, q_t[:, h:h + MXU_WIDTH], preferred_element_type=F32)
                            for h in range(0, q_t.shape[1], MXU_WIDTH)], axis=1)


def _tile_softmax(s, bias, vt_tile, shifted):
    if bias is not None:
        rows = bias.shape[0]
        biased = s[:rows] + jnp.concatenate([bias] * (s.shape[1] // Q_BLOCK), axis=1)
        s = biased if rows == s.shape[0] else jnp.concatenate([biased, s[rows:]], axis=0)
    m = None if shifted else jnp.max(s, axis=0, keepdims=True)
    p = jnp.exp2(s if shifted else s - m)
    ol = jnp.dot(vt_tile, p.astype(BF16), preferred_element_type=F32)
    return m, ol[HEAD_DIM:HEAD_DIM + 1], ol[:HEAD_DIM]


def _merge_tiles(parts):
    if parts[0][0] is None:
        weights = [None if mine is None else jnp.where(mine, 1.0, 0.0) for _, _, _, mine in parts]
    else:
        ms = [mi if mine is None else jnp.where(mine, mi, NEG_BIG) for mi, _, _, mine in parts]
        m = functools.reduce(jnp.maximum, ms)
        weights = [jnp.exp2(mi - m) for mi in ms]
    l = None
    o = None
    for w, (_, li, oi, _) in zip(weights, parts):
        if w is not None:
            li, oi = w * li, w * oi
        l = li if l is None else l + li
        o = oi if o is None else o + oi
    return o / l


def _attn_kernel(q_ref, gate_ref, kvc_ref, ks_ref, kw_ref, selmap_ref, qg_ref, kg_ref, shift_ref, o_ref,
                 ksn, kwn, vst, vwt, qt_scr, slot_m, slot_l, slot_o, *, shifted):
    pair = pl.program_id(2)
    seq = ks_ref.shape[3]
    ncmp = kvc_ref.shape[2]
    nsel = selmap_ref.shape[0]
    nqb = seq // Q_BLOCK
    cols = NSA_HPG * Q_BLOCK
    win_tiles = WINDOW // LANES
    chunk_tiles = SEL_CHUNK // LANES
    scale = HEAD_DIM ** -0.5 * LOG2_E

    @pl.when(pair == 0)
    def _prepare_kv():
        lane = lax.broadcasted_iota(jnp.int32, (LANES, LANES), 1)
        row = lax.broadcasted_iota(jnp.int32, (LANES, LANES), 0)
        ones_row = (lax.broadcasted_iota(jnp.int32, (VT_ROWS - HEAD_DIM, LANES), 0) == 0).astype(BF16)
        pad_mark = (lane == PAD_LANE).astype(BF16)
        for w in range(win_tiles):
            kwn[w * LANES:(w + 1) * LANES, :] = pad_mark
        vwt[0:win_tiles] = jnp.zeros((win_tiles, VT_ROWS, LANES), BF16)

        def body(c, carry):
            r0 = pl.multiple_of(c * LANES, LANES)
            ts = ks_ref[0, 0, 0, pl.ds(r0, LANES), :]
            in_block = (lane - BLOCK_LANE0 == lax.shift_right_logical(r0 + row, SEL_BLOCK.bit_length() - 1))
            marks = (in_block | (lane == SHIFT_LANE)).astype(F32)
            ksn[pl.ds(r0, LANES), :] = (_k_lane_norm(ts, kg_ref[1:2]) + marks).astype(BF16)
            vst[c] = jnp.concatenate([ts.T[HEAD_DIM:, :].astype(BF16), ones_row], axis=0)
            tw = kw_ref[0, 0, 0, pl.ds(r0, LANES), :]
            kwn[pl.ds(WINDOW + r0, LANES), :] = (
                _k_lane_norm(tw, kg_ref[2:3]) + (lane == SHIFT_LANE).astype(F32)).astype(BF16)
            vwt[win_tiles + c] = jnp.concatenate([tw.T[HEAD_DIM:, :].astype(BF16), ones_row], axis=0)
            return carry

        lax.fori_loop(0, seq // LANES, body, 0)

    kc = kvc_ref[0, 0]
    kc_b = kc.astype(BF16)
    vc_t = kc.T[HEAD_DIM:, :].astype(BF16)
    block_ids = (pair, nqb - 1 - pair)

    def block_branches(idx):
        i = block_ids[idx]
        t0 = i * Q_BLOCK
        row0 = pl.multiple_of(t0, Q_BLOCK)

        q = q_ref[0, pl.ds(row0, Q_BLOCK), :]
        heads = []
        for h in range(NSA_HPG // 2):
            qt = q[:, h * LANES:(h + 1) * LANES].T
            for a in range(2):
                x = qt[a * HEAD_DIM:(a + 1) * HEAD_DIM]
                ms = jnp.mean(x * x, axis=0, keepdims=True)
                heads.append(x * lax.rsqrt(ms + EPS) * qg_ref[...] * scale)
        q_top = jnp.concatenate(heads, axis=1).astype(BF16)
        mask_rows = lax.broadcasted_iota(jnp.int32, (LANES - HEAD_DIM, cols), 0) + HEAD_DIM
        q_plain = jnp.concatenate([q_top, jnp.zeros((LANES - HEAD_DIM, cols), BF16)], axis=0)
        shift_rows = lambda br: jnp.where(
            mask_rows == SHIFT_LANE, -jnp.concatenate([shift_ref[br:br + 1, :]] * NSA_HPG, axis=1), 0.0)
        q_win = jnp.concatenate(
            [q_top, jnp.where(mask_rows == PAD_LANE, NEG_BIG, shift_rows(2)).astype(BF16)], axis=0)

        s = _tile_scores(kc_b, q_plain)
        tq = t0 + (lax.broadcasted_iota(jnp.int32, (ncmp, cols), 1) & (Q_BLOCK - 1))
        cmp_end = lax.broadcasted_iota(jnp.int32, (ncmp, cols), 0) * CMP_STRIDE + (CMP_BLOCK - 1)
        visible = cmp_end <= tq
        s = jnp.where(visible, s, NEG_BIG)
        m = jnp.max(s, axis=0, keepdims=True)
        p = jnp.where(visible, jnp.exp2(s - m), 0.0)
        l = jnp.sum(p, axis=0, keepdims=True)
        p_c = p / jnp.maximum(l, jnp.finfo(F32).tiny)
        o_c = jnp.dot(vc_t, p_c.astype(BF16), preferred_element_type=F32)

        p_sum = p_c[:, 0:Q_BLOCK]
        for n in range(1, NSA_HPG):
            p_sum = p_sum + p_c[:, n * Q_BLOCK:(n + 1) * Q_BLOCK]
        imp = jnp.dot(selmap_ref[...], p_sum, preferred_element_type=F32, precision=lax.Precision.HIGHEST)
        blk = lax.broadcasted_iota(jnp.int32, (nsel, Q_BLOCK), 0)
        cur = lax.shift_right_logical(t0 + lax.broadcasted_iota(jnp.int32, (nsel, Q_BLOCK), 1), 6)
        valid = blk <= cur
        forced = (blk == 0) | (blk == cur) | (blk == cur - 1)
        val = jnp.where(valid, imp, -jnp.inf)
        val = jnp.where(forced, jnp.inf, val)
        rank = jnp.zeros((nsel, Q_BLOCK), jnp.int32)
        for j2 in range(nsel):
            row = val[j2:j2 + 1, :]
            beats = (row > val) | ((row == val) & (blk > j2))
            rank = rank + beats.astype(jnp.int32)
        picked = jnp.where((rank < SEL_TOP_N) & valid, 0.0, NEG_BIG).astype(BF16)
        qt_scr[idx] = jnp.concatenate(
            [q_top, jnp.concatenate([picked] * NSA_HPG, axis=1),
             shift_rows(1)[nsel:].astype(BF16)], axis=0)

        tiles = []
        off = 0
        span = WINDOW + Q_BLOCK
        while off < span:
            kc_w = min(SEL_CHUNK, span - off)
            if off < Q_BLOCK:
                rows_b = min(kc_w, Q_BLOCK - off)
                newest = False
            elif off + kc_w > WINDOW:
                assert off >= WINDOW
                rows_b = kc_w
                newest = True
            else:
                rows_b = 0
                newest = False

            def scores(off=off, kc_w=kc_w):
                k_tile = kwn[pl.ds(pl.multiple_of(t0 + off, LANES), kc_w), :]
                return _tile_scores(k_tile, q_win)

            def finish(s, off=off, kc_w=kc_w, rows_b=rows_b, newest=newest):
                bias = None
                if rows_b:
                    r = off + lax.broadcasted_iota(jnp.int32, (rows_b, Q_BLOCK), 0)
                    c = lax.broadcasted_iota(jnp.int32, (rows_b, Q_BLOCK), 1)
                    bias = jnp.where((r - WINDOW <= c) if newest else (r > c), 0.0, NEG_BIG)
                vt_tile = jnp.concatenate([vwt[i + off // LANES + cc] for cc in range(kc_w // LANES)], axis=1)
                return _tile_softmax(s, bias, vt_tile, shifted)

            tiles.append((scores, finish))
            off += kc_w
        gate_t = gate_ref[0, 0, pl.ds(row0, Q_BLOCK), :].T
        return o_c, gate_t, tiles

    per_block = [block_branches(idx) for idx in range(2)]
    n_win = len(per_block[0][2])

    n_slots = _sel_slot_count(nqb)
    n0 = lax.shift_right_logical(block_ids[0] * Q_BLOCK + Q_BLOCK + SEL_CHUNK - 1, SEL_CHUNK.bit_length() - 1)
    n0_max = (nqb // 2 * Q_BLOCK + SEL_CHUNK - 1) // SEL_CHUNK
    key_minus_query = (lax.broadcasted_iota(jnp.int32, (SEL_CHUNK, Q_BLOCK), 0)
                       - lax.broadcasted_iota(jnp.int32, (SEL_CHUNK, Q_BLOCK), 1))
    sel_tiles = []
    for slot in range(n_slots):
        last_of_block = slot == 0 or slot == n_slots - 1
        if slot == 0:
            which, kb = 0, n0 - 1
        elif slot == n_slots - 1:
            which, kb = 1, n_slots - n0 - 1
        elif slot >= n0_max:
            which, kb = 1, slot - n0
        else:
            which = (slot >= n0).astype(jnp.int32)
            kb = slot - 1 + which * (1 - n0)
        k0 = pl.multiple_of(kb * SEL_CHUNK, SEL_CHUNK)

        def scores(which=which, k0=k0):
            return _tile_scores(ksn[pl.ds(k0, SEL_CHUNK), :], qt_scr[which])

        def finish(s, which=which, kb=kb, k0=k0, last_of_block=last_of_block):
            bias = None
            if last_of_block:
                t0 = block_ids[which] * Q_BLOCK
                bias = jnp.where(key_minus_query <= t0 - k0, 0.0, NEG_BIG)
            vt_tile = jnp.concatenate([vst[kb * chunk_tiles + c] for c in range(chunk_tiles)], axis=1)
            return _tile_softmax(s, bias, vt_tile, shifted)

        sel_tiles.append((scores, finish))

    all_tiles = per_block[0][2] + per_block[1][2] + sel_tiles
    results = []
    pending = [tile[0]() for tile in all_tiles[:SCORE_LOOKAHEAD]]
    for t, (_, finish) in enumerate(all_tiles):
        if t + SCORE_LOOKAHEAD < len(all_tiles):
            pending.append(all_tiles[t + SCORE_LOOKAHEAD][0]())
        results.append(finish(pending.pop(0)))
    for slot in range(n_slots):
        m, l, o = results[2 * n_win + slot]
        if not shifted:
            slot_m[slot] = m
        slot_l[slot] = l
        slot_o[slot] = o

    for idx in range(2):
        o_c, gate_t, _ = per_block[idx]
        o_w = _merge_tiles([part + (None,) for part in results[idx * n_win:(idx + 1) * n_win]])
        slots = range(0, n0_max) if idx == 0 else range(1, n_slots)
        parts = []
        for slot in slots:
            if slot == 0 or slot >= n0_max:
                mine = None
            else:
                mine = (slot < n0) if idx == 0 else (slot >= n0)
            parts.append((None if shifted else slot_m[slot], slot_l[slot], slot_o[slot], mine))
        o_s = _merge_tiles(parts)
        row0 = pl.multiple_of(block_ids[idx] * Q_BLOCK, Q_BLOCK)
        for hh in range(NSA_HPG // 2):
            tiles = []
            for a in range(2):
                n = 2 * hh + a
                lanes = slice(n * Q_BLOCK, (n + 1) * Q_BLOCK)
                c0 = N_BRANCH * n
                tiles.append(gate_t[c0:c0 + 1, :] * o_c[:, lanes]
                             + gate_t[c0 + 1:c0 + 2, :] * o_s[:, lanes]
                             + gate_t[c0 + 2:c0 + 3, :] * o_w[:, lanes])
            o_ref[0, pl.ds(row0, Q_BLOCK), hh * LANES:(hh + 1) * LANES] = jnp.concatenate(tiles, axis=0).T


def _sel_map_t(seq):
    n_c = seq // CMP_STRIDE - 1
    n_s = seq // SEL_BLOCK
    tok = np.arange(seq)
    start = np.arange(n_c) * CMP_STRIDE
    cover_c = (tok[None, :] >= start[:, None]) & (tok[None, :] < start[:, None] + CMP_BLOCK)
    cover_s = (tok[:, None] // SEL_BLOCK) == np.arange(n_s)[None, :]
    m = cover_c.astype(np.float32) @ cover_s.astype(np.float32) / np.float32(CMP_BLOCK)
    out = np.zeros((n_s, n_c + 1), np.float32)
    out[:, :n_c] = m.T
    return jnp.asarray(out)


def _score_bounds(q_gain, k_gain):
    bound = 1.01 * HEAD_DIM ** 0.5 * LOG2_E * jnp.max(jnp.abs(q_gain)) * jnp.max(jnp.abs(k_gain), axis=-1)
    return jnp.broadcast_to(bound[:, None], (N_BRANCH, LANES)).astype(F32)


def _attention(q, gates, kvc, kv, qg_t, kg_ext, shift, shifted):
    b, s, _ = q.shape
    g = NSA_GROUPS
    nqb = s // Q_BLOCK
    ncmp = s // CMP_STRIDE
    nsel = s // SEL_BLOCK
    assert s % SEL_CHUNK == 0 and nsel % SUBLANES == 0 and nqb % 2 == 0
    gw = NSA_HPG * HEAD_DIM
    cols = NSA_HPG * Q_BLOCK
    n_slots = _sel_slot_count(nqb)

    def kv_spec(j):
        return pl.BlockSpec((1, 1, 1, s, LANES), lambda bi, gi, i: (j, bi, gi, 0, 0))

    return pl.pallas_call(
        functools.partial(_attn_kernel, shifted=shifted),
        grid=(b, g, nqb // 2),
        in_specs=[
            pl.BlockSpec((1, s, gw), lambda bi, gi, i: (bi, 0, gi)),
            pl.BlockSpec((1, 1, s, LANES), lambda bi, gi, i: (bi, gi, 0, 0)),
            pl.BlockSpec((1, 1, ncmp, LANES), lambda bi, gi, i: (bi, gi, 0, 0)),
            kv_spec(1), kv_spec(2),
            pl.BlockSpec((nsel, ncmp), lambda bi, gi, i: (0, 0)),
            pl.BlockSpec((HEAD_DIM, Q_BLOCK), lambda bi, gi, i: (0, 0)),
            pl.BlockSpec((N_BRANCH, LANES), lambda bi, gi, i: (0, 0)),
            pl.BlockSpec((N_BRANCH, LANES), lambda bi, gi, i: (0, 0)),
        ],
        out_specs=pl.BlockSpec((1, s, gw), lambda bi, gi, i: (bi, 0, gi)),
        out_shape=jax.ShapeDtypeStruct((b, s, Q_DIM), F32),
        scratch_shapes=[
            pltpu.VMEM((s, LANES), BF16),
            pltpu.VMEM((WINDOW + s, LANES), BF16),
            pltpu.VMEM((s // LANES, VT_ROWS, LANES), BF16),
            pltpu.VMEM(((WINDOW + s) // LANES, VT_ROWS, LANES), BF16),
            pltpu.VMEM((2, LANES, cols), BF16),
            pltpu.VMEM((n_slots, 1, cols), F32),
            pltpu.VMEM((n_slots, 1, cols), F32),
            pltpu.VMEM((n_slots, HEAD_DIM, cols), F32),
        ],
        compiler_params=_cparams(3),
        name="nsa_attention",
    )(q, gates, kvc, kv, kv, _sel_map_t(s), qg_t, kg_ext, shift)


def _ffn_core(x, g_ref, sc_ref, sh_ref, gate_ref, wg_ref, wu_ref, wo_ref):
    hh = _norm_mod(x, g_ref[...], sc_ref[0], sh_ref[0]).astype(BF16)
    a = jnp.dot(hh, wg_ref[...], preferred_element_type=F32)
    u = jnp.dot(hh, wu_ref[...], preferred_element_type=F32)
    act = (jax.nn.silu(a) * u).astype(BF16)
    return x + gate_ref[0] * jnp.dot(act, wo_ref[...], preferred_element_type=F32)


def _ffn_kernel(x_ref, g_ref, sc_ref, sh_ref, gate_ref, wg_ref, wu_ref, wo_ref, o_ref):
    o_ref[0] = _ffn_core(x_ref[0], g_ref, sc_ref, sh_ref, gate_ref, wg_ref, wu_ref, wo_ref)


def _mix_ffn_kernel(y_ref, wmix_ref, mix_gate_ref, x_ref, g_ref, sc_ref, sh_ref, gate_ref, wg_ref, wu_ref, wo_ref,
                    o_ref):
    mix = jnp.dot(y_ref[0].astype(BF16), wmix_ref[...], preferred_element_type=F32)
    x = x_ref[0] + mix_gate_ref[0] * mix
    o_ref[0] = _ffn_core(x, g_ref, sc_ref, sh_ref, gate_ref, wg_ref, wu_ref, wo_ref)


def _ffn(x, gain, sc, sh, gate, w_in, w_out, layer, mixer_out=None):
    b, s, d = x.shape
    hidden = w_out.shape[1]
    tm = min(ROW_TILE, s)
    vec = pl.BlockSpec((1, 1, d), lambda bi, i: (bi, 0, 0))
    rows = lambda width: pl.BlockSpec((1, tm, width), lambda bi, i: (bi, i, 0))
    in_specs = [
        rows(d), _resident((1, d)), vec, vec, vec,
        _layer_plane((d, hidden), layer, 0), _layer_plane((d, hidden), layer, 1),
        _layer_plane((hidden, d), layer),
    ]
    args = (x, gain, sc, sh, gate, w_in, w_in, w_out)
    body = _ffn_kernel
    if mixer_out is not None:
        y, w_mix, j, mix_gate = mixer_out
        in_specs = [rows(y.shape[-1]), _layer_plane(w_mix.shape[1:], j), vec] + in_specs
        args = (y, w_mix, mix_gate) + args
        body = _mix_ffn_kernel
    return pl.pallas_call(
        body,
        grid=(b, s // tm),
        in_specs=in_specs,
        out_specs=rows(d),
        out_shape=jax.ShapeDtypeStruct((b, s, d), F32),
        compiler_params=_cparams(2),
        name="ffn",
    )(*args)


def _sigmoid(x):
    return 0.5 * jnp.tanh(0.5 * x) + 0.5


def _band_starts(width, block_w):
    band = 2 * MXU_WIDTH
    starts = []
    for c0 in range(0, width, MXU_WIDTH):
        c1 = min(c0 + MXU_WIDTH, width)
        lo = (c0 // block_w) * block_w
        hi = ((c1 - 1) // block_w + 1) * block_w
        k0 = min((lo // LANES) * LANES, width - band)
        assert k0 >= 0 and k0 <= lo and hi <= k0 + band
        starts.append(k0)
    return starts


def _band_weights(w_a, w_x):
    nb, bw, _ = w_a.shape
    width = nb * bw
    spread = jnp.asarray(np.tile(np.eye(bw, dtype=np.float32), (1, nb)), dtype=BF16)
    blk = np.arange(width) // bw
    on_diag = jnp.asarray(blk[:, None] == blk[None, :])
    dense = lambda w: jnp.where(
        on_diag, jnp.dot(w.reshape(width, bw).astype(BF16), spread, preferred_element_type=BF16), 0)
    da, dx = dense(w_a), dense(w_x)
    band = 2 * MXU_WIDTH
    tiles = []
    for t, k0 in enumerate(_band_starts(width, bw)):
        c0 = t * MXU_WIDTH
        c1 = min(c0 + MXU_WIDTH, width)
        pad = ((0, 0), (0, MXU_WIDTH - (c1 - c0)))
        tiles.append(jnp.concatenate(
            [jnp.pad(da[k0:k0 + band, c0:c1], pad), jnp.pad(dx[k0:k0 + band, c0:c1], pad)], axis=1))
    return jnp.stack(tiles).astype(BF16)


def _rg_mixer_kernel(x_ref, g_ref, sc_ref, sh_ref, res_gate_ref, win_ref, cw_ref, cb_ref, wb_ref, ba_ref,
                     bx_ref, lam_ref, wout_ref, o_ref,
                     u_halo, h_scr, a_scr, b_scr, hs_scr, gate_scr, y_scr, *, band_starts):
    step = pl.program_id(0)
    nb, ts, d = x_ref.shape
    width = wout_ref.shape[0]
    rows = nb * ts
    halo = CONV_WIDTH - 1
    band = 2 * MXU_WIDTH

    @pl.when(step == 0)
    def _init():
        u_halo[0:halo] = jnp.zeros((halo, nb, width), F32)
        h_scr[...] = jnp.zeros(h_scr.shape, F32)

    hh = _norm_mod(x_ref[...], g_ref[...], sc_ref[...], sh_ref[...]).reshape(rows, d)
    proj = jnp.dot(hh.astype(BF16), win_ref[...], preferred_element_type=F32)
    gate_scr[...] = proj[:, :width]

    u_halo[halo:] = jnp.swapaxes(proj[:, width:].reshape(nb, ts, width), 0, 1)
    uc = jnp.zeros((ts, nb, width), F32) + cb_ref[...]
    for k in range(CONV_WIDTH):
        uc = uc + cw_ref[k:k + 1, :] * u_halo[k:k + ts]
    u_halo[0:halo] = u_halo[ts:ts + halo]
    uc = uc.reshape(rows, width)
    ub = uc.astype(BF16)

    z = -lam_ref[...]
    log_a_unit = -RG_C * (jnp.maximum(z, 0.0) + jnp.log1p(jnp.exp(-jnp.abs(z))))

    for t, k0 in enumerate(band_starts):
        c0 = t * MXU_WIDTH
        c1 = min(c0 + MXU_WIDTH, width)
        n = c1 - c0
        zz = jnp.dot(ub[:, k0:k0 + band], wb_ref[t], preferred_element_type=F32)
        r = _sigmoid(zz[:, :n] + ba_ref[:, c0:c1])
        ig = _sigmoid(zz[:, MXU_WIDTH:MXU_WIDTH + n] + bx_ref[:, c0:c1])
        log_a = log_a_unit[:, c0:c1] * r
        a = jnp.exp(log_a)
        bb = jnp.sqrt(-jnp.tanh(log_a) * (a * a + 1.0)) * (ig * uc[:, c0:c1])
        for ct in range(n // LANES):
            lanes = slice(ct * LANES, (ct + 1) * LANES)
            a_scr[c0 // LANES + ct] = a[:, lanes]
            b_scr[c0 // LANES + ct] = bb[:, lanes]

    n_ct = width // LANES

    def scan_step(t, h):
        rows_t = pl.ds(pl.multiple_of(t * nb, nb), nb)
        new = []
        for ct in range(n_ct):
            hc = a_scr[ct, rows_t, :] * h[ct] + b_scr[ct, rows_t, :]
            hs_scr[ct, rows_t, :] = hc
            new.append(hc)
        return tuple(new)

    h_fin = lax.fori_loop(0, ts, scan_step, tuple(h_scr[ct] for ct in range(n_ct)), unroll=SUBLANES)
    for ct in range(n_ct):
        lanes = slice(ct * LANES, (ct + 1) * LANES)
        h_scr[ct] = h_fin[ct]
        hs = jnp.swapaxes(hs_scr[ct].reshape(ts, nb, LANES), 0, 1)
        y_scr[:, lanes] = (hs.reshape(rows, LANES) * jax.nn.gelu(gate_scr[:, lanes])).astype(BF16)
    mix = jnp.dot(y_scr[...], wout_ref[...], preferred_element_type=F32)
    o_ref[...] = x_ref[...] + res_gate_ref[...] * mix.reshape(nb, ts, d)


def _rg_mixer(x, gain, sc, sh, res_gate, w_in, conv_w, conv_b, w_band, b_a, b_x, lam, w_out, j, block_w):
    b, s, d = x.shape
    width = w_out.shape[1]
    assert b == SUBLANES
    ts = min(RG_TIME_TILE, s)
    rows = b * ts
    band_starts = tuple(_band_starts(width, block_w))
    tile = pl.BlockSpec((b, ts, d), lambda t: (0, t, 0))
    mod = _resident((b, 1, d))
    vec = _resident((1, width))
    kern = functools.partial(_rg_mixer_kernel, band_starts=band_starts)
    return pl.pallas_call(
        kern,
        grid=(s // ts,),
        in_specs=[
            tile, _resident((1, d)), mod, mod, mod,
            _layer_plane(w_in.shape[1:], j),
            _resident((CONV_WIDTH, width)), vec,
            _resident(w_band.shape),
            vec, vec, vec,
            _layer_plane(w_out.shape[1:], j),
        ],
        out_specs=tile,
        out_shape=jax.ShapeDtypeStruct((b, s, d), F32),
        scratch_shapes=[
            pltpu.VMEM((ts + CONV_WIDTH - 1, b, width), F32),
            pltpu.VMEM((width // LANES, b, LANES), F32),
            pltpu.VMEM((width // LANES, rows, LANES), F32),
            pltpu.VMEM((width // LANES, rows, LANES), F32),
            pltpu.VMEM((width // LANES, rows, LANES), F32),
            pltpu.VMEM((rows, width), F32),
            pltpu.VMEM((rows, width), BF16),
        ],
        compiler_params=_cparams(1),
        name="rg_mixer",
    )(x, gain, sc, sh, res_gate, w_in, conv_w, conv_b, w_band, b_a, b_x, lam, w_out)


def kernel(x, c, ada_w, ada_b, norm1_g, norm2_g, nsa_w_in, nsa_w_out, nsa_cmp_pos, nsa_cmp_w1, nsa_cmp_w2, nsa_q_gain, nsa_k_gain, rg_w_in, rg_conv_w, rg_conv_b, rg_w_a, rg_b_a, rg_w_x, rg_b_x, rg_lam, rg_w_out, ffn_w_in, ffn_w_out):
    depth, d, _ = ada_w.shape
    b = x.shape[0]
    mod = _ada_mod(c, ada_w, ada_b).reshape(depth, b, 6, 1, d)
    ffn_w_in, ffn_w_out, nsa_w_out, rg_w_in, rg_w_out = (
        w.astype(BF16) for w in (ffn_w_in, ffn_w_out, nsa_w_out, rg_w_in, rg_w_out))
    for layer in range(depth):
        sh1, sc1, g1, sh2, sc2, g2 = (mod[layer, :, k] for k in range(6))
        n1 = norm1_g[layer].reshape(1, d)
        n2 = norm2_g[layer].reshape(1, d)
        j = layer // N_MIXERS
        if layer % N_MIXERS == 0:
            kg_ext = jnp.concatenate([nsa_k_gain[j], jnp.zeros_like(nsa_k_gain[j])], axis=1)
            qg_t = jnp.broadcast_to(nsa_q_gain[j][:, None], (HEAD_DIM, Q_BLOCK))
            q, kv, gates = _nsa_proj(x, n1, sc1, sh1, _nsa_weight_layout(nsa_w_in[j]))
            kvc = _compress(kv, *_compress_weights(nsa_cmp_pos[j], nsa_cmp_w1[j], nsa_cmp_w2[j]), kg_ext)
            bounds = _score_bounds(nsa_q_gain[j], nsa_k_gain[j])
            o = lax.cond(
                2.0 * jnp.max(bounds) <= MAX_SHIFTED_SCORE_RANGE,
                lambda *a: _attention(*a, bounds, True),
                lambda *a: _attention(*a, jnp.zeros_like(bounds), False),
                q, gates, kvc, kv, qg_t, kg_ext)
            mixer_out = (o, nsa_w_out, j, g1)
        else:
            width = rg_w_out.shape[1]
            block_w = rg_w_a.shape[-1]
            x = _rg_mixer(x, n1, sc1, sh1, g1, rg_w_in,
                          rg_conv_w[j], rg_conv_b[j].reshape(1, width),
                          _band_weights(rg_w_a[j], rg_w_x[j]),
                          rg_b_a[j].reshape(1, width), rg_b_x[j].reshape(1, width),
                          rg_lam[j].reshape(1, width), rg_w_out, j, block_w)
            mixer_out = None
        x = _ffn(x, n2, sc2, sh2, g2, ffn_w_in, ffn_w_out, layer, mixer_out)
    return x
```

```python
import functools

import numpy as np
import jax
import jax.numpy as jnp
from jax import lax
from jax.experimental import pallas as pl
from jax.experimental.pallas import tpu as pltpu

F32 = jnp.float32
BF16 = jnp.bfloat16

EPS = 1e-6
N_MIXERS = 2

NSA_HEADS = 16
NSA_GROUPS = 4
NSA_HPG = NSA_HEADS // NSA_GROUPS
HEAD_DIM = 64
CMP_STRIDE = 16
CMP_BLOCK = 2 * CMP_STRIDE
SEL_BLOCK = 64
SEL_TOP_N = 8
WINDOW = 512
Q_BLOCK = 128
N_BRANCH = 3
Q_DIM = NSA_HEADS * HEAD_DIM
KV_DIM = NSA_GROUPS * HEAD_DIM
GATE_COLS = N_BRANCH * NSA_HEADS
GATE_PER_GROUP = N_BRANCH * NSA_HPG

RNN_BLOCKS = 16
CONV_WIDTH = 4
RG_C = 8.0

LANES = 128
SUBLANES = 8
MXU_WIDTH = 256
VMEM_LIMIT_BYTES = 56 * 1024 * 1024

NEG_BIG = -1e30
LOG2_E = 1.4426950408889634
BLOCK_LANE0 = HEAD_DIM
PAD_LANE = LANES - 1
SHIFT_LANE = LANES - 2
MAX_SHIFTED_SCORE_RANGE = 100.0
VT_ROWS = HEAD_DIM + 16
SEL_CHUNK = 256
SCORE_LOOKAHEAD = 3
RG_TIME_TILE = 64
ROW_TILE = 512


def _cparams(n_axes, flags=None):
    return pltpu.CompilerParams(
        dimension_semantics=("arbitrary",) * n_axes,
        vmem_limit_bytes=VMEM_LIMIT_BYTES,
        flags=flags,
    )


def _resident(shape):
    return pl.BlockSpec(shape, lambda *_: (0,) * len(shape), pipeline_mode=pl.Buffered(1))


def _layer_plane(shape, layer, col_block=0):
    return pl.BlockSpec((None,) + tuple(shape), lambda *_: (layer, 0, col_block), pipeline_mode=pl.Buffered(1))


def _norm_mod(x, gain, scale, shift):
    ms = jnp.mean(x * x, axis=-1, keepdims=True)
    y = x * lax.rsqrt(ms + EPS) * gain
    return y * (1.0 + scale) + shift


def _ada_kernel(c_ref, w_ref, b_ref, o_ref):
    cond = jax.nn.silu(c_ref[...])
    o_ref[0] = jnp.dot(cond, w_ref[0], preferred_element_type=F32,
                       precision=lax.Precision.HIGHEST) + b_ref[0]


def _ada_mod(c, ada_w, ada_b):
    depth, d, n = ada_w.shape
    b = c.shape[0]
    tn = n // 4
    return pl.pallas_call(
        _ada_kernel,
        grid=(depth, n // tn),
        in_specs=[
            pl.BlockSpec((b, d), lambda l, j: (0, 0)),
            pl.BlockSpec((1, d, tn), lambda l, j: (l, 0, j)),
            pl.BlockSpec((1, 1, tn), lambda l, j: (l, 0, j)),
        ],
        out_specs=pl.BlockSpec((1, b, tn), lambda l, j: (l, 0, j)),
        out_shape=jax.ShapeDtypeStruct((depth, b, n), F32),
        compiler_params=_cparams(2),
        name="ada_mod",
    )(c, ada_w, ada_b.reshape(depth, 1, n))


def _nsa_weight_layout(w_in):
    d = w_in.shape[0]
    kv = w_in[:, Q_DIM:Q_DIM + 6 * KV_DIM].reshape(d, N_BRANCH, 2, NSA_GROUPS, HEAD_DIM)
    kv = jnp.transpose(kv, (0, 1, 3, 2, 4)).reshape(d, 6 * KV_DIM)
    gates = jnp.pad(w_in[:, Q_DIM + 6 * KV_DIM:], ((0, 0), (0, LANES - GATE_COLS)))
    return jnp.concatenate([w_in[:, :Q_DIM], kv, gates], axis=1).astype(BF16)


def _nsa_proj_kernel(x_ref, g_ref, sc_ref, sh_ref, w_ref, q_ref, kv_ref, gate_ref):
    hh = _norm_mod(x_ref[0], g_ref[...], sc_ref[0], sh_ref[0])
    p = jnp.dot(hh.astype(BF16), w_ref[...], preferred_element_type=F32)
    q_ref[0] = p[:, :Q_DIM]
    for j in range(N_BRANCH):
        for g in range(NSA_GROUPS):
            lo = Q_DIM + (j * NSA_GROUPS + g) * LANES
            kv_ref[j, 0, g] = p[:, lo:lo + LANES]
    gate = jax.nn.sigmoid(p[:, Q_DIM + 6 * KV_DIM:])
    for g in range(NSA_GROUPS):
        shift = (LANES - GATE_PER_GROUP * g) % LANES
        gate_ref[0, g] = gate if shift == 0 else pltpu.roll(gate, shift, 1)


def _nsa_proj(x, gain, sc, sh, w):
    b, s, d = x.shape
    tm = min(ROW_TILE, s)
    n = w.shape[1]
    vec = pl.BlockSpec((1, 1, d), lambda bi, i: (bi, 0, 0))
    return pl.pallas_call(
        _nsa_proj_kernel,
        grid=(b, s // tm),
        in_specs=[
            pl.BlockSpec((1, tm, d), lambda bi, i: (bi, i, 0)),
            pl.BlockSpec((1, d), lambda bi, i: (0, 0)),
            vec, vec,
            pl.BlockSpec((d, n), lambda bi, i: (0, 0)),
        ],
        out_specs=[
            pl.BlockSpec((1, tm, Q_DIM), lambda bi, i: (bi, i, 0)),
            pl.BlockSpec((N_BRANCH, 1, NSA_GROUPS, tm, LANES), lambda bi, i: (0, bi, 0, i, 0)),
            pl.BlockSpec((1, NSA_GROUPS, tm, LANES), lambda bi, i: (bi, 0, i, 0)),
        ],
        out_shape=[
            jax.ShapeDtypeStruct((b, s, Q_DIM), F32),
            jax.ShapeDtypeStruct((N_BRANCH, b, NSA_GROUPS, s, LANES), F32),
            jax.ShapeDtypeStruct((b, NSA_GROUPS, s, LANES), F32),
        ],
        compiler_params=_cparams(2),
        name="nsa_proj",
    )(x, gain, sc, sh, w)


def _k_lane_norm(t, gain_ext):
    lane = lax.broadcasted_iota(jnp.int32, t.shape, 1)
    ms = jnp.sum(jnp.where(lane < HEAD_DIM, t * t, 0.0), axis=-1, keepdims=True) * (1.0 / HEAD_DIM)
    return t * lax.rsqrt(ms + EPS) * gain_ext


def _compress_weights(pos, w1, w2):
    hidden = w1.shape[-1]
    w1p = w1.reshape(2, 2, CMP_STRIDE, HEAD_DIM, hidden)
    z = jnp.zeros_like(w1p[0])
    wk = jnp.concatenate([w1p[0], z], axis=-1)
    wv = jnp.concatenate([z, w1p[1]], axis=-1)
    w1_blk = jnp.concatenate([wk, wv], axis=2).astype(BF16)
    pos_blk = jnp.concatenate([pos[0], pos[1]], axis=-1).reshape(2, CMP_STRIDE, 1, LANES)
    z2 = jnp.zeros_like(w2[0])
    w2_blk = jnp.concatenate([jnp.concatenate([w2[0], z2], axis=1),
                              jnp.concatenate([z2, w2[1]], axis=1)], axis=0).astype(BF16)
    return pos_blk, w1_blk, w2_blk


def _compress_kernel(kv_ref, pos_ref, w1_ref, w2_ref, kg_ref, o_ref):
    g, seq = kv_ref.shape[2:4]
    nch = seq // CMP_STRIDE
    rows = g * nch
    first = None
    second = None
    for p in range(CMP_STRIDE):
        tok = jnp.concatenate([kv_ref[0, 0, gi, pl.ds(p, nch, stride=CMP_STRIDE), :] for gi in range(g)], axis=0)
        fa = jnp.dot((tok + pos_ref[0, p]).astype(BF16), w1_ref[0, p], preferred_element_type=F32)
        fb = jnp.dot((tok + pos_ref[1, p]).astype(BF16), w1_ref[1, p], preferred_element_type=F32)
        first = fa if first is None else first + fa
        second = fb if second is None else second + fb
    h1 = first + pltpu.roll(second, rows - 1, 0)
    out = jnp.dot(jax.nn.silu(h1).astype(BF16), w2_ref[...], preferred_element_type=F32)
    lane = lax.broadcasted_iota(jnp.int32, out.shape, 1)
    out = jnp.where(lane < HEAD_DIM, _k_lane_norm(out, kg_ref[0:1]), out)
    r = lax.broadcasted_iota(jnp.int32, out.shape, 0)
    out = jnp.where((r & (nch - 1)) == nch - 1, 0.0, out)
    o_ref[0] = out.reshape(g, nch, LANES)


def _compress(kv, pos_blk, w1_blk, w2_blk, kg_ext):
    _, b, g, s, _ = kv.shape
    nch = s // CMP_STRIDE
    assert nch & (nch - 1) == 0
    return pl.pallas_call(
        _compress_kernel,
        grid=(b,),
        in_specs=[
            pl.BlockSpec((1, 1, g, s, LANES), lambda bi: (0, bi, 0, 0, 0)),
            pl.BlockSpec(pos_blk.shape, lambda bi: (0, 0, 0, 0)),
            pl.BlockSpec(w1_blk.shape, lambda bi: (0, 0, 0, 0)),
            pl.BlockSpec(w2_blk.shape, lambda bi: (0, 0)),
            pl.BlockSpec((N_BRANCH, LANES), lambda bi: (0, 0)),
        ],
        out_specs=pl.BlockSpec((1, g, nch, LANES), lambda bi: (bi, 0, 0, 0)),
        out_shape=jax.ShapeDtypeStruct((b, g, nch, LANES), F32),
        compiler_params=_cparams(1),
        name="nsa_compress",
    )(kv, pos_blk, w1_blk, w2_blk, kg_ext)


def _sel_slot_count(nqb):
    chunks = lambda i: -(-(i + 1) * Q_BLOCK // SEL_CHUNK)
    counts = {chunks(p) + chunks(nqb - 1 - p) for p in range(nqb // 2)}
    assert len(counts) == 1, counts
    return counts.pop()


def _tile_scores(k_tile, q_t):
    return jnp.concatenate([jnp.dot(k_tile, q_t[:, h:h + MXU_WIDTH], preferred_element_type=F32)
                            for h in range(0, q_t.shape[1], MXU_WIDTH)], axis=1)


def _tile_softmax(s, bias, vt_tile, shifted):
    if bias is not None:
        rows = bias.shape[0]
        biased = s[:rows] + jnp.concatenate([bias] * (s.shape[1] // Q_BLOCK), axis=1)
        s = biased if rows == s.shape[0] else jnp.concatenate([biased, s[rows:]], axis=0)
    m = None if shifted else jnp.max(s, axis=0, keepdims=True)
    p = jnp.exp2(s if shifted else s - m)
    ol = jnp.dot(vt_tile, p.astype(BF16), preferred_element_type=F32)
    return m, ol[HEAD_DIM:HEAD_DIM + 1], ol[:HEAD_DIM]


def _merge_tiles(parts):
    if parts[0][0] is None:
        weights = [None if mine is None else jnp.where(mine, 1.0, 0.0) for _, _, _, mine in parts]
    else:
        ms = [mi if mine is None else jnp.where(mine, mi, NEG_BIG) for mi, _, _, mine in parts]
        m = functools.reduce(jnp.maximum, ms)
        weights = [jnp.exp2(mi - m) for mi in ms]
    l = None
    o = None
    for w, (_, li, oi, _) in zip(weights, parts):
        if w is not None:
            li, oi = w * li, w * oi
        l = li if l is None else l + li
        o = oi if o is None else o + oi
    return o / l


def _attn_kernel(q_ref, gate_ref, kvc_ref, ks_ref, kw_ref, selmap_ref, qg_ref, kg_ref, shift_ref, o_ref,
                 ksn, kwn, vst, vwt, qt_scr, slot_m, slot_l, slot_o, *, shifted):
    pair = pl.program_id(2)
    seq = ks_ref.shape[3]
    ncmp = kvc_ref.shape[2]
    nsel = selmap_ref.shape[0]
    nqb = seq // Q_BLOCK
    cols = NSA_HPG * Q_BLOCK
    win_tiles = WINDOW // LANES
    chunk_tiles = SEL_CHUNK // LANES
    scale = HEAD_DIM ** -0.5 * LOG2_E

    @pl.when(pair == 0)
    def _prepare_kv():
        lane = lax.broadcasted_iota(jnp.int32, (LANES, LANES), 1)
        row = lax.broadcasted_iota(jnp.int32, (LANES, LANES), 0)
        vrow = lax.broadcasted_iota(jnp.int32, (VT_ROWS, LANES), 0)
        vlane = lax.broadcasted_iota(jnp.int32, (VT_ROWS, LANES), 1)
        pick_v = ((vlane == vrow + HEAD_DIM) & (vrow < HEAD_DIM)).astype(BF16)
        ones_row = (vrow == HEAD_DIM).astype(F32)
        v_plane = lambda tile: (lax.dot_general(pick_v, tile.astype(BF16), (((1,), (1,)), ((), ())),
                                                preferred_element_type=F32) + ones_row).astype(BF16)
        pad_mark = (lane == PAD_LANE).astype(BF16)
        for w in range(win_tiles):
            kwn[w * LANES:(w + 1) * LANES, :] = pad_mark
        vwt[0:win_tiles] = jnp.zeros((win_tiles, VT_ROWS, LANES), BF16)

        def body(c, carry):
            r0 = pl.multiple_of(c * LANES, LANES)
            ts = ks_ref[0, 0, 0, pl.ds(r0, LANES), :]
            in_block = (lane - BLOCK_LANE0 == lax.shift_right_logical(r0 + row, SEL_BLOCK.bit_length() - 1))
            marks = (in_block | (lane == SHIFT_LANE)).astype(F32)
            ksn[pl.ds(r0, LANES), :] = (_k_lane_norm(ts, kg_ref[1:2]) + marks).astype(BF16)
            vst[c] = v_plane(ts)
            tw = kw_ref[0, 0, 0, pl.ds(r0, LANES), :]
            kwn[pl.ds(WINDOW + r0, LANES), :] = (
                _k_lane_norm(tw, kg_ref[2:3]) + (lane == SHIFT_LANE).astype(F32)).astype(BF16)
            vwt[win_tiles + c] = v_plane(tw)
            return carry

        lax.fori_loop(0, seq // LANES, body, 0, unroll=4)

    kc = kvc_ref[0, 0]
    kc_b = kc.astype(BF16)
    vc_t = kc.T[HEAD_DIM:, :].astype(BF16)
    block_ids = (pair, nqb - 1 - pair)

    def block_branches(idx):
        i = block_ids[idx]
        t0 = i * Q_BLOCK
        row0 = pl.multiple_of(t0, Q_BLOCK)

        q = q_ref[0, pl.ds(row0, Q_BLOCK), :]
        heads = []
        for h in range(NSA_HPG // 2):
            qt = q[:, h * LANES:(h + 1) * LANES].T
            for a in range(2):
                x = qt[a * HEAD_DIM:(a + 1) * HEAD_DIM]
                ms = jnp.mean(x * x, axis=0, keepdims=True)
                heads.append(x * lax.rsqrt(ms + EPS) * qg_ref[...] * scale)
        q_top = jnp.concatenate(heads, axis=1).astype(BF16)
        mask_rows = lax.broadcasted_iota(jnp.int32, (LANES - HEAD_DIM, cols), 0) + HEAD_DIM
        q_plain = jnp.concatenate([q_top, jnp.zeros((LANES - HEAD_DIM, cols), BF16)], axis=0)
        shift_rows = lambda br: jnp.where(
            mask_rows == SHIFT_LANE, -jnp.concatenate([shift_ref[br:br + 1, :]] * NSA_HPG, axis=1), 0.0)
        q_win = jnp.concatenate(
            [q_top, jnp.where(mask_rows == PAD_LANE, NEG_BIG, shift_rows(2)).astype(BF16)], axis=0)

        tiles = []
        off = 0
        span = WINDOW + Q_BLOCK
        while off < span:
            kc_w = min(SEL_CHUNK, span - off)
            if off < Q_BLOCK:
                rows_b = min(kc_w, Q_BLOCK - off)
                newest = False
            elif off + kc_w > WINDOW:
                assert off >= WINDOW
                rows_b = kc_w
                newest = True
            else:
                rows_b = 0
                newest = False

            def scores(off=off, kc_w=kc_w):
                k_tile = kwn[pl.ds(pl.multiple_of(t0 + off, LANES), kc_w), :]
                return _tile_scores(k_tile, q_win)

            def finish(s, off=off, kc_w=kc_w, rows_b=rows_b, newest=newest):
                bias = None
                if rows_b:
                    r = off + lax.broadcasted_iota(jnp.int32, (rows_b, Q_BLOCK), 0)
                    c = lax.broadcasted_iota(jnp.int32, (rows_b, Q_BLOCK), 1)
                    bias = jnp.where((r - WINDOW <= c) if newest else (r > c), 0.0, NEG_BIG)
                vt_tile = jnp.concatenate([vwt[i + off // LANES + cc] for cc in range(kc_w // LANES)], axis=1)
                return _tile_softmax(s, bias, vt_tile, shifted)

            tiles.append((scores, finish))
            off += kc_w
        yield tiles

        s = _tile_scores(kc_b, q_plain)
        yield
        tq = t0 + (lax.broadcasted_iota(jnp.int32, (ncmp, cols), 1) & (Q_BLOCK - 1))
        cmp_end = lax.broadcasted_iota(jnp.int32, (ncmp, cols), 0) * CMP_STRIDE + (CMP_BLOCK - 1)
        visible = cmp_end <= tq
        s = jnp.where(visible, s, NEG_BIG)
        m = jnp.max(s, axis=0, keepdims=True)
        p = jnp.where(visible, jnp.exp2(s - m), 0.0)
        l = jnp.sum(p, axis=0, keepdims=True)
        p_c = p / jnp.maximum(l, jnp.finfo(F32).tiny)
        o_c = jnp.dot(vc_t, p_c.astype(BF16), preferred_element_type=F32)

        yield
        p_sum = p_c[:, 0:Q_BLOCK]
        for n in range(1, NSA_HPG):
            p_sum = p_sum + p_c[:, n * Q_BLOCK:(n + 1) * Q_BLOCK]
        imp = jnp.dot(selmap_ref[...], p_sum, preferred_element_type=F32, precision=lax.Precision.HIGHEST)
        blk = lax.broadcasted_iota(jnp.int32, (nsel, Q_BLOCK), 0)
        cur = lax.shift_right_logical(t0 + lax.broadcasted_iota(jnp.int32, (nsel, Q_BLOCK), 1), 6)
        valid = blk <= cur
        forced = (blk == 0) | (blk == cur) | (blk == cur - 1)
        val = jnp.where(valid, imp, -jnp.inf)
        val = jnp.where(forced, jnp.inf, val)
        rank = jnp.zeros((nsel, Q_BLOCK), jnp.int32)
        for j2 in range(nsel):
            row = val[j2:j2 + 1, :]
            beats = (row > val) | ((row == val) & (blk > j2))
            rank = rank + beats.astype(jnp.int32)
            yield
        picked = jnp.where((rank < SEL_TOP_N) & valid, 0.0, NEG_BIG).astype(BF16)
        qt_scr[idx] = jnp.concatenate(
            [q_top, jnp.concatenate([picked] * NSA_HPG, axis=1),
             shift_rows(1)[nsel:].astype(BF16)], axis=0)
        gate_t = gate_ref[0, 0, pl.ds(row0, Q_BLOCK), :].T
        return o_c, gate_t

    all_tiles = []
    pending = []
    results = []

    def pump():
        while len(results) + len(pending) < min(len(all_tiles), len(results) + SCORE_LOOKAHEAD + 1):
            pending.append(all_tiles[len(results) + len(pending)][0]())
        results.append(all_tiles[len(results)][1](pending.pop(0)))

    per_block = [None, None]
    running = {idx: block_branches(idx) for idx in range(2)}
    for idx in range(2):
        all_tiles += next(running[idx])
    n_win = len(all_tiles) // 2
    while running:
        for idx, stages in list(running.items()):
            try:
                next(stages)
            except StopIteration as done:
                per_block[idx] = done.value
                del running[idx]

    n_slots = _sel_slot_count(nqb)
    n0 = lax.shift_right_logical(block_ids[0] * Q_BLOCK + Q_BLOCK + SEL_CHUNK - 1, SEL_CHUNK.bit_length() - 1)
    n0_max = (nqb // 2 * Q_BLOCK + SEL_CHUNK - 1) // SEL_CHUNK
    key_minus_query = (lax.broadcasted_iota(jnp.int32, (SEL_CHUNK, Q_BLOCK), 0)
                       - lax.broadcasted_iota(jnp.int32, (SEL_CHUNK, Q_BLOCK), 1))
    sel_tiles = []
    for slot in range(n_slots):
        last_of_block = slot == 0 or slot == n_slots - 1
        if slot == 0:
            which, kb = 0, n0 - 1
        elif slot == n_slots - 1:
            which, kb = 1, n_slots - n0 - 1
        elif slot >= n0_max:
            which, kb = 1, slot - n0
        else:
            which = (slot >= n0).astype(jnp.int32)
            kb = slot - 1 + which * (1 - n0)
        k0 = pl.multiple_of(kb * SEL_CHUNK, SEL_CHUNK)

        def scores(which=which, k0=k0):
            return _tile_scores(ksn[pl.ds(k0, SEL_CHUNK), :], qt_scr[which])

        def finish(s, which=which, kb=kb, k0=k0, last_of_block=last_of_block):
            bias = None
            if last_of_block:
                t0 = block_ids[which] * Q_BLOCK
                bias = jnp.where(key_minus_query <= t0 - k0, 0.0, NEG_BIG)
            vt_tile = jnp.concatenate([vst[kb * chunk_tiles + c] for c in range(chunk_tiles)], axis=1)
            return _tile_softmax(s, bias, vt_tile, shifted)

        sel_tiles.append((scores, finish))

    all_tiles += sel_tiles
    while len(results) < len(all_tiles):
        pump()
    for slot in range(n_slots):
        m, l, o = results[2 * n_win + slot]
        if not shifted:
            slot_m[slot] = m
        slot_l[slot] = l
        slot_o[slot] = o

    for idx in range(2):
        o_c, gate_t = per_block[idx]
        o_w = _merge_tiles([part + (None,) for part in results[idx * n_win:(idx + 1) * n_win]])
        slots = range(0, n0_max) if idx == 0 else range(1, n_slots)
        parts = []
        for slot in slots:
            if slot == 0 or slot >= n0_max:
                mine = None
            else:
                mine = (slot < n0) if idx == 0 else (slot >= n0)
            parts.append((None if shifted else slot_m[slot], slot_l[slot], slot_o[slot], mine))
        o_s = _merge_tiles(parts)
        row0 = pl.multiple_of(block_ids[idx] * Q_BLOCK, Q_BLOCK)
        for hh in range(NSA_HPG // 2):
            tiles = []
            for a in range(2):
                n = 2 * hh + a
                lanes = slice(n * Q_BLOCK, (n + 1) * Q_BLOCK)
                c0 = N_BRANCH * n
                tiles.append(gate_t[c0:c0 + 1, :] * o_c[:, lanes]
                             + gate_t[c0 + 1:c0 + 2, :] * o_s[:, lanes]
                             + gate_t[c0 + 2:c0 + 3, :] * o_w[:, lanes])
            o_ref[0, pl.ds(row0, Q_BLOCK), hh * LANES:(hh + 1) * LANES] = jnp.concatenate(tiles, axis=0).T


def _sel_map_t(seq):
    n_c = seq // CMP_STRIDE - 1
    n_s = seq // SEL_BLOCK
    tok = np.arange(seq)
    start = np.arange(n_c) * CMP_STRIDE
    cover_c = (tok[None, :] >= start[:, None]) & (tok[None, :] < start[:, None] + CMP_BLOCK)
    cover_s = (tok[:, None] // SEL_BLOCK) == np.arange(n_s)[None, :]
    m = cover_c.astype(np.float32) @ cover_s.astype(np.float32) / np.float32(CMP_BLOCK)
    out = np.zeros((n_s, n_c + 1), np.float32)
    out[:, :n_c] = m.T
    return jnp.asarray(out)


def _score_bounds(q_gain, k_gain):
    bound = 1.01 * HEAD_DIM ** 0.5 * LOG2_E * jnp.max(jnp.abs(q_gain)) * jnp.max(jnp.abs(k_gain), axis=-1)
    return jnp.broadcast_to(bound[:, None], (N_BRANCH, LANES)).astype(F32)


def _attention(q, gates, kvc, kv, qg_t, kg_ext, shift, shifted):
    b, s, _ = q.shape
    g = NSA_GROUPS
    nqb = s // Q_BLOCK
    ncmp = s // CMP_STRIDE
    nsel = s // SEL_BLOCK
    assert s % SEL_CHUNK == 0 and nsel % SUBLANES == 0 and nqb % 2 == 0
    gw = NSA_HPG * HEAD_DIM
    cols = NSA_HPG * Q_BLOCK
    n_slots = _sel_slot_count(nqb)

    def kv_spec(j):
        return pl.BlockSpec((1, 1, 1, s, LANES), lambda bi, gi, i: (j, bi, gi, 0, 0))

    return pl.pallas_call(
        functools.partial(_attn_kernel, shifted=shifted),
        grid=(b, g, nqb // 2),
        in_specs=[
            pl.BlockSpec((1, s, gw), lambda bi, gi, i: (bi, 0, gi)),
            pl.BlockSpec((1, 1, s, LANES), lambda bi, gi, i: (bi, gi, 0, 0)),
            pl.BlockSpec((1, 1, ncmp, LANES), lambda bi, gi, i: (bi, gi, 0, 0)),
            kv_spec(1), kv_spec(2),
            pl.BlockSpec((nsel, ncmp), lambda bi, gi, i: (0, 0)),
            pl.BlockSpec((HEAD_DIM, Q_BLOCK), lambda bi, gi, i: (0, 0)),
            pl.BlockSpec((N_BRANCH, LANES), lambda bi, gi, i: (0, 0)),
            pl.BlockSpec((N_BRANCH, LANES), lambda bi, gi, i: (0, 0)),
        ],
        out_specs=pl.BlockSpec((1, s, gw), lambda bi, gi, i: (bi, 0, gi)),
        out_shape=jax.ShapeDtypeStruct((b, s, Q_DIM), F32),
        scratch_shapes=[
            pltpu.VMEM((s, LANES), BF16),
            pltpu.VMEM((WINDOW + s, LANES), BF16),
            pltpu.VMEM((s // LANES, VT_ROWS, LANES), BF16),
            pltpu.VMEM(((WINDOW + s) // LANES, VT_ROWS, LANES), BF16),
            pltpu.VMEM((2, LANES, cols), BF16),
            pltpu.VMEM((n_slots, 1, cols), F32),
            pltpu.VMEM((n_slots, 1, cols), F32),
            pltpu.VMEM((n_slots, HEAD_DIM, cols), F32),
        ],
        compiler_params=_cparams(3),
        name="nsa_attention",
    )(q, gates, kvc, kv, kv, _sel_map_t(s), qg_t, kg_ext, shift)


def _ffn_core(x, g_ref, sc_ref, sh_ref, gate_ref, wg_ref, wu_ref, wo_ref):
    hh = _norm_mod(x, g_ref[...], sc_ref[0], sh_ref[0]).astype(BF16)
    a = jnp.dot(hh, wg_ref[...], preferred_element_type=F32)
    u = jnp.dot(hh, wu_ref[...], preferred_element_type=F32)
    act = (jax.nn.silu(a) * u).astype(BF16)
    return x + gate_ref[0] * jnp.dot(act, wo_ref[...], preferred_element_type=F32)


def _ffn_kernel(x_ref, g_ref, sc_ref, sh_ref, gate_ref, wg_ref, wu_ref, wo_ref, o_ref):
    o_ref[0] = _ffn_core(x_ref[0], g_ref, sc_ref, sh_ref, gate_ref, wg_ref, wu_ref, wo_ref)


def _mix_ffn_kernel(y_ref, wmix_ref, mix_gate_ref, x_ref, g_ref, sc_ref, sh_ref, gate_ref, wg_ref, wu_ref, wo_ref,
                    o_ref):
    mix = jnp.dot(y_ref[0].astype(BF16), wmix_ref[...], preferred_element_type=F32)
    x = x_ref[0] + mix_gate_ref[0] * mix
    o_ref[0] = _ffn_core(x, g_ref, sc_ref, sh_ref, gate_ref, wg_ref, wu_ref, wo_ref)


def _ffn(x, gain, sc, sh, gate, w_in, w_out, layer, mixer_out=None):
    b, s, d = x.shape
    hidden = w_out.shape[1]
    tm = min(ROW_TILE, s)
    vec = pl.BlockSpec((1, 1, d), lambda bi, i: (bi, 0, 0))
    rows = lambda width: pl.BlockSpec((1, tm, width), lambda bi, i: (bi, i, 0))
    in_specs = [
        rows(d), _resident((1, d)), vec, vec, vec,
        _layer_plane((d, hidden), layer, 0), _layer_plane((d, hidden), layer, 1),
        _layer_plane((hidden, d), layer),
    ]
    args = (x, gain, sc, sh, gate, w_in, w_in, w_out)
    body = _ffn_kernel
    if mixer_out is not None:
        y, w_mix, j, mix_gate = mixer_out
        in_specs = [rows(y.shape[-1]), _layer_plane(w_mix.shape[1:], j), vec] + in_specs
        args = (y, w_mix, mix_gate) + args
        body = _mix_ffn_kernel
    return pl.pallas_call(
        body,
        grid=(b, s // tm),
        in_specs=in_specs,
        out_specs=rows(d),
        out_shape=jax.ShapeDtypeStruct((b, s, d), F32),
        compiler_params=_cparams(2),
        name="ffn",
    )(*args)


def _band_starts(width, block_w):
    band = 2 * MXU_WIDTH
    starts = []
    for c0 in range(0, width, MXU_WIDTH):
        c1 = min(c0 + MXU_WIDTH, width)
        lo = (c0 // block_w) * block_w
        hi = ((c1 - 1) // block_w + 1) * block_w
        k0 = min((lo // LANES) * LANES, width - band)
        assert k0 >= 0 and k0 <= lo and hi <= k0 + band
        starts.append(k0)
    return starts


def _band_weights(w_a, w_x):
    nb, bw, _ = w_a.shape
    width = nb * bw
    spread = jnp.asarray(np.tile(np.eye(bw, dtype=np.float32), (1, nb)), dtype=BF16)
    blk = np.arange(width) // bw
    on_diag = jnp.asarray(blk[:, None] == blk[None, :])
    dense = lambda w: jnp.where(
        on_diag, jnp.dot(w.reshape(width, bw).astype(BF16), spread, preferred_element_type=BF16), 0)
    da, dx = dense(w_a), dense(w_x)
    band = 2 * MXU_WIDTH
    tiles = []
    for t, k0 in enumerate(_band_starts(width, bw)):
        c0 = t * MXU_WIDTH
        c1 = min(c0 + MXU_WIDTH, width)
        pad = ((0, 0), (0, MXU_WIDTH - (c1 - c0)))
        tiles.append(jnp.concatenate(
            [jnp.pad(da[k0:k0 + band, c0:c1], pad), jnp.pad(dx[k0:k0 + band, c0:c1], pad)], axis=1))
    return 0.5 * jnp.stack(tiles)


def _rg_mixer_kernel(x_ref, xn_ref, g_ref, sc_ref, sh_ref, res_gate_ref, win_ref, cw_ref, cb_ref, wb_ref, ba_ref,
                     bx_ref, lam_ref, wout_ref, o_ref,
                     u_a, u_b, gate_a, gate_b, h_scr, a_scr, b_scr, y_scr, *, band_starts):
    step = pl.program_id(0)
    nb, ts, d = x_ref.shape
    width = wout_ref.shape[0]
    rows = nb * ts
    halo = CONV_WIDTH - 1
    band = 2 * MXU_WIDTH

    pieces = [(c0, min(c0 + MXU_WIDTH, 2 * width)) for c0 in range(0, 2 * width, MXU_WIDTH)]

    def project_piece(hh, piece, u_dst, gate_dst):
        c0, c1 = pieces[piece]
        res = jnp.dot(hh, win_ref[:, c0:c1], preferred_element_type=F32)
        n_gate = max(0, min(c1, width) - c0)
        if n_gate:
            gate_dst[:, c0:c0 + n_gate] = res[:, :n_gate]
        if c1 > width:
            u0 = max(c0, width) - width
            u_dst[:, :, u0:c1 - width] = jnp.swapaxes(res[:, n_gate:].reshape(nb, ts, c1 - c0 - n_gate), 0, 1)

    def normed(tile_ref):
        return _norm_mod(tile_ref[...], g_ref[...], sc_ref[...], sh_ref[...]).reshape(rows, d).astype(BF16)

    @pl.when(step == 0)
    def _init():
        u_b[...] = jnp.zeros((ts, nb, width), F32)
        h_scr[...] = jnp.zeros(h_scr.shape, F32)
        hh = normed(x_ref)
        for piece in range(len(pieces)):
            project_piece(hh, piece, u_a, gate_a)

    def tile_step(u_cur, gate_cur, u_nxt, gate_nxt):
        prev_tail = u_nxt[ts - halo:ts]
        hh_next = normed(xn_ref)
        todo = list(range(len(pieces)))
        per_stage = -(-len(pieces) // len(band_starts))

        uc = jnp.zeros((ts, nb, width), F32) + cb_ref[...]
        for k in range(CONV_WIDTH):
            shifted = u_cur[...] if k == halo else jnp.concatenate([prev_tail[k:], u_cur[0:ts - halo + k]], axis=0)
            uc = uc + cw_ref[k:k + 1, :] * shifted
        uc = uc.reshape(rows, width)
        ub = uc.astype(BF16)

        z = -lam_ref[...]
        log_a_unit = -RG_C * (jnp.maximum(z, 0.0) + jnp.log1p(jnp.exp(-jnp.abs(z))))

        for t, k0 in enumerate(band_starts):
            c0 = t * MXU_WIDTH
            c1 = min(c0 + MXU_WIDTH, width)
            n = c1 - c0
            zz = jnp.dot(ub[:, k0:k0 + band], wb_ref[t], preferred_element_type=F32)
            for piece in todo[:per_stage]:
                project_piece(hh_next, piece, u_nxt, gate_nxt)
            todo = todo[per_stage:]
            half_unit = 0.5 * log_a_unit[:, c0:c1]
            half_u = 0.5 * uc[:, c0:c1]
            log_a = half_unit * jnp.tanh(zz[:, :n] + ba_ref[:, c0:c1]) + half_unit
            gated_u = half_u * jnp.tanh(zz[:, MXU_WIDTH:MXU_WIDTH + n] + bx_ref[:, c0:c1]) + half_u
            a = jnp.exp(log_a)
            bb = jnp.sqrt(-jnp.tanh(log_a) * (a * a + 1.0)) * gated_u
            for ct in range(n // LANES):
                lanes = slice(ct * LANES, (ct + 1) * LANES)
                a_scr[c0 // LANES + ct] = a[:, lanes]
                b_scr[c0 // LANES + ct] = bb[:, lanes]

        n_ct = width // LANES

        def scan_step(t, h):
            rows_t = pl.ds(pl.multiple_of(t * nb, nb), nb)
            new = []
            for ct in range(n_ct):
                hc = a_scr[ct, rows_t, :] * h[ct] + b_scr[ct, rows_t, :]
                b_scr[ct, rows_t, :] = hc
                new.append(hc)
            return tuple(new)

        h_fin = lax.fori_loop(0, ts, scan_step, tuple(h_scr[ct] for ct in range(n_ct)), unroll=SUBLANES)
        for ct in range(n_ct):
            lanes = slice(ct * LANES, (ct + 1) * LANES)
            h_scr[ct] = h_fin[ct]
            hs = jnp.swapaxes(b_scr[ct].reshape(ts, nb, LANES), 0, 1)
            y_scr[:, lanes] = (hs.reshape(rows, LANES) * jax.nn.gelu(gate_cur[:, lanes])).astype(BF16)
        mix = jnp.dot(y_scr[...], wout_ref[...], preferred_element_type=F32)
        o_ref[...] = x_ref[...] + res_gate_ref[...] * mix.reshape(nb, ts, d)

    @pl.when(step % 2 == 0)
    def _even():
        tile_step(u_a, gate_a, u_b, gate_b)

    @pl.when(step % 2 == 1)
    def _odd():
        tile_step(u_b, gate_b, u_a, gate_a)


def _rg_mixer(x, gain, sc, sh, res_gate, w_in, conv_w, conv_b, w_band, b_a, b_x, lam, w_out, j, block_w):
    b, s, d = x.shape
    width = w_out.shape[1]
    assert b == SUBLANES
    ts = min(RG_TIME_TILE, s)
    rows = b * ts
    band_starts = tuple(_band_starts(width, block_w))
    n_tiles = s // ts
    tile = pl.BlockSpec((b, ts, d), lambda t: (0, t, 0))
    next_tile = pl.BlockSpec((b, ts, d), lambda t: (0, jnp.minimum(t + 1, n_tiles - 1), 0))
    mod = _resident((b, 1, d))
    vec = _resident((1, width))
    kern = functools.partial(_rg_mixer_kernel, band_starts=band_starts)
    return pl.pallas_call(
        kern,
        grid=(n_tiles,),
        in_specs=[
            tile, next_tile, _resident((1, d)), mod, mod, mod,
            _layer_plane(w_in.shape[1:], j),
            _resident((CONV_WIDTH, width)), vec,
            _resident(w_band.shape),
            vec, vec, vec,
            _layer_plane(w_out.shape[1:], j),
        ],
        out_specs=tile,
        out_shape=jax.ShapeDtypeStruct((b, s, d), F32),
        scratch_shapes=[
            pltpu.VMEM((ts, b, width), F32),
            pltpu.VMEM((ts, b, width), F32),
            pltpu.VMEM((rows, width), F32),
            pltpu.VMEM((rows, width), F32),
            pltpu.VMEM((width // LANES, b, LANES), F32),
            pltpu.VMEM((width // LANES, rows, LANES), F32),
            pltpu.VMEM((width // LANES, rows, LANES), F32),
            pltpu.VMEM((rows, width), BF16),
        ],
        compiler_params=_cparams(1),
        name="rg_mixer",
    )(x, x, gain, sc, sh, res_gate, w_in, conv_w, conv_b, w_band, b_a, b_x, lam, w_out)


def kernel(x, c, ada_w, ada_b, norm1_g, norm2_g, nsa_w_in, nsa_w_out, nsa_cmp_pos, nsa_cmp_w1, nsa_cmp_w2, nsa_q_gain, nsa_k_gain, rg_w_in, rg_conv_w, rg_conv_b, rg_w_a, rg_b_a, rg_w_x, rg_b_x, rg_lam, rg_w_out, ffn_w_in, ffn_w_out):
    depth, d, _ = ada_w.shape
    b = x.shape[0]
    mod = _ada_mod(c, ada_w, ada_b).reshape(depth, b, 6, 1, d)
    ffn_w_in, ffn_w_out, nsa_w_out, rg_w_in, rg_w_out = (
        w.astype(BF16) for w in (ffn_w_in, ffn_w_out, nsa_w_out, rg_w_in, rg_w_out))
    for layer in range(depth):
        sh1, sc1, g1, sh2, sc2, g2 = (mod[layer, :, k] for k in range(6))
        n1 = norm1_g[layer].reshape(1, d)
        n2 = norm2_g[layer].reshape(1, d)
        j = layer // N_MIXERS
        if layer % N_MIXERS == 0:
            kg_ext = jnp.concatenate([nsa_k_gain[j], jnp.zeros_like(nsa_k_gain[j])], axis=1)
            qg_t = jnp.broadcast_to(nsa_q_gain[j][:, None], (HEAD_DIM, Q_BLOCK))
            q, kv, gates = _nsa_proj(x, n1, sc1, sh1, _nsa_weight_layout(nsa_w_in[j]))
            kvc = _compress(kv, *_compress_weights(nsa_cmp_pos[j], nsa_cmp_w1[j], nsa_cmp_w2[j]), kg_ext)
            bounds = _score_bounds(nsa_q_gain[j], nsa_k_gain[j])
            o = lax.cond(
                2.0 * jnp.max(bounds) <= MAX_SHIFTED_SCORE_RANGE,
                lambda *a: _attention(*a, bounds, True),
                lambda *a: _attention(*a, jnp.zeros_like(bounds), False),
                q, gates, kvc, kv, qg_t, kg_ext)
            mixer_out = (o, nsa_w_out, j, g1)
        else:
            width = rg_w_out.shape[1]
            block_w = rg_w_a.shape[-1]
            x = _rg_mixer(x, n1, sc1, sh1, g1, rg_w_in,
                          rg_conv_w[j], rg_conv_b[j].reshape(1, width),
                          _band_weights(rg_w_a[j], rg_w_x[j]),
                          0.5 * rg_b_a[j].reshape(1, width), 0.5 * rg_b_x[j].reshape(1, width),
                          rg_lam[j].reshape(1, width), rg_w_out, j, block_w)
            mixer_out = None
        x = _ffn(x, n2, sc2, sh2, g2, ffn_w_in, ffn_w_out, layer, mixer_out)
    return x
```

```python
import functools

import numpy as np
import jax
import jax.numpy as jnp
from jax import lax
from jax.experimental import pallas as pl
from jax.experimental.pallas import tpu as pltpu

F32 = jnp.float32
BF16 = jnp.bfloat16

EPS = 1e-6
N_MIXERS = 2

NSA_HEADS = 16
NSA_GROUPS = 4
NSA_HPG = NSA_HEADS // NSA_GROUPS
HEAD_DIM = 64
CMP_STRIDE = 16
CMP_BLOCK = 2 * CMP_STRIDE
SEL_BLOCK = 64
SEL_TOP_N = 8
WINDOW = 512
Q_BLOCK = 128
N_BRANCH = 3
Q_DIM = NSA_HEADS * HEAD_DIM
KV_DIM = NSA_GROUPS * HEAD_DIM
GATE_COLS = N_BRANCH * NSA_HEADS
GATE_PER_GROUP = N_BRANCH * NSA_HPG

RNN_BLOCKS = 16
CONV_WIDTH = 4
RG_C = 8.0

LANES = 128
SUBLANES = 8
MXU_WIDTH = 256
VMEM_LIMIT_BYTES = 56 * 1024 * 1024

NEG_BIG = -1e30
LOG2_E = 1.4426950408889634
BLOCK_LANE0 = HEAD_DIM
PAD_LANE = LANES - 1
SHIFT_LANE = LANES - 2
MAX_SHIFTED_SCORE_RANGE = 100.0
VT_ROWS = HEAD_DIM + 16
SEL_CHUNK = 256
SCORE_LOOKAHEAD = 3
RG_TIME_TILE = 64
ROW_TILE = 512


def _cparams(n_axes, flags=None):
    return pltpu.CompilerParams(
        dimension_semantics=("arbitrary",) * n_axes,
        vmem_limit_bytes=VMEM_LIMIT_BYTES,
        flags=flags,
    )


def _resident(shape):
    return pl.BlockSpec(shape, lambda *_: (0,) * len(shape), pipeline_mode=pl.Buffered(1))


def _layer_plane(shape, layer, col_block=0):
    return pl.BlockSpec((None,) + tuple(shape), lambda *_: (layer, 0, col_block), pipeline_mode=pl.Buffered(1))


def _norm_mod(x, gain, scale, shift):
    ms = jnp.mean(x * x, axis=-1, keepdims=True)
    y = x * lax.rsqrt(ms + EPS) * gain
    return y * (1.0 + scale) + shift


def _ada_kernel(c_ref, w_ref, b_ref, o_ref):
    cond = jax.nn.silu(c_ref[...])
    o_ref[0] = jnp.dot(cond, w_ref[0], preferred_element_type=F32,
                       precision=lax.Precision.HIGHEST) + b_ref[0]


def _ada_mod(c, ada_w, ada_b):
    depth, d, n = ada_w.shape
    b = c.shape[0]
    tn = n // 4
    return pl.pallas_call(
        _ada_kernel,
        grid=(depth, n // tn),
        in_specs=[
            pl.BlockSpec((b, d), lambda l, j: (0, 0)),
            pl.BlockSpec((1, d, tn), lambda l, j: (l, 0, j)),
            pl.BlockSpec((1, 1, tn), lambda l, j: (l, 0, j)),
        ],
        out_specs=pl.BlockSpec((1, b, tn), lambda l, j: (l, 0, j)),
        out_shape=jax.ShapeDtypeStruct((depth, b, n), F32),
        compiler_params=_cparams(2),
        name="ada_mod",
    )(c, ada_w, ada_b.reshape(depth, 1, n))


def _nsa_weight_layout(w_in):
    w_in = w_in.astype(BF16)
    src = np.arange(6 * KV_DIM).reshape(N_BRANCH, 2, NSA_GROUPS, HEAD_DIM).transpose(0, 2, 1, 3).reshape(-1)
    perm = np.zeros((6 * KV_DIM, 6 * KV_DIM), np.float32)
    perm[src, np.arange(6 * KV_DIM)] = 1.0
    kv = jnp.dot(w_in[:, Q_DIM:Q_DIM + 6 * KV_DIM], jnp.asarray(perm, dtype=BF16), preferred_element_type=BF16)
    gates = jnp.pad(w_in[:, Q_DIM + 6 * KV_DIM:], ((0, 0), (0, LANES - GATE_COLS)))
    return jnp.concatenate([w_in[:, :Q_DIM], kv, gates], axis=1)


def _nsa_proj_kernel(x_ref, g_ref, sc_ref, sh_ref, w_ref, q_ref, kv_ref, gate_ref):
    hh = _norm_mod(x_ref[0], g_ref[...], sc_ref[0], sh_ref[0])
    p = jnp.dot(hh.astype(BF16), w_ref[...], preferred_element_type=F32)
    q_ref[0] = p[:, :Q_DIM]
    for j in range(N_BRANCH):
        for g in range(NSA_GROUPS):
            lo = Q_DIM + (j * NSA_GROUPS + g) * LANES
            kv_ref[j, 0, g] = p[:, lo:lo + LANES]
    gate = jax.nn.sigmoid(p[:, Q_DIM + 6 * KV_DIM:])
    for g in range(NSA_GROUPS):
        shift = (LANES - GATE_PER_GROUP * g) % LANES
        gate_ref[0, g] = gate if shift == 0 else pltpu.roll(gate, shift, 1)


def _nsa_proj(x, gain, sc, sh, w):
    b, s, d = x.shape
    tm = min(ROW_TILE, s)
    n = w.shape[1]
    vec = pl.BlockSpec((1, 1, d), lambda bi, i: (bi, 0, 0))
    return pl.pallas_call(
        _nsa_proj_kernel,
        grid=(b, s // tm),
        in_specs=[
            pl.BlockSpec((1, tm, d), lambda bi, i: (bi, i, 0)),
            pl.BlockSpec((1, d), lambda bi, i: (0, 0)),
            vec, vec,
            pl.BlockSpec((d, n), lambda bi, i: (0, 0)),
        ],
        out_specs=[
            pl.BlockSpec((1, tm, Q_DIM), lambda bi, i: (bi, i, 0)),
            pl.BlockSpec((N_BRANCH, 1, NSA_GROUPS, tm, LANES), lambda bi, i: (0, bi, 0, i, 0)),
            pl.BlockSpec((1, NSA_GROUPS, tm, LANES), lambda bi, i: (bi, 0, i, 0)),
        ],
        out_shape=[
            jax.ShapeDtypeStruct((b, s, Q_DIM), F32),
            jax.ShapeDtypeStruct((N_BRANCH, b, NSA_GROUPS, s, LANES), F32),
            jax.ShapeDtypeStruct((b, NSA_GROUPS, s, LANES), F32),
        ],
        compiler_params=_cparams(2),
        name="nsa_proj",
    )(x, gain, sc, sh, w)


def _k_lane_norm(t, gain_ext):
    lane = lax.broadcasted_iota(jnp.int32, t.shape, 1)
    ms = jnp.sum(jnp.where(lane < HEAD_DIM, t * t, 0.0), axis=-1, keepdims=True) * (1.0 / HEAD_DIM)
    return t * lax.rsqrt(ms + EPS) * gain_ext


def _compress_weights(pos, w1, w2):
    hidden = w1.shape[-1]
    w1p = w1.reshape(2, 2, CMP_STRIDE, HEAD_DIM, hidden)
    z = jnp.zeros_like(w1p[0])
    wk = jnp.concatenate([w1p[0], z], axis=-1)
    wv = jnp.concatenate([z, w1p[1]], axis=-1)
    w1_blk = jnp.concatenate([wk, wv], axis=2).astype(BF16)
    w1_blk = w1_blk.reshape(2, CMP_STRIDE * LANES, 2 * hidden)
    pos_blk = jnp.concatenate([pos[0], pos[1]], axis=-1).reshape(2, CMP_STRIDE, 1, LANES)
    z2 = jnp.zeros_like(w2[0])
    w2_blk = jnp.concatenate([jnp.concatenate([w2[0], z2], axis=1),
                              jnp.concatenate([z2, w2[1]], axis=1)], axis=0).astype(BF16)
    return pos_blk, w1_blk, w2_blk


def _compress_kernel(kv_ref, pos_ref, w1_ref, w2_ref, kg_ref, o_ref):
    g, seq = kv_ref.shape[2:4]
    nch = seq // CMP_STRIDE
    rows = g * nch
    toks = [jnp.concatenate([kv_ref[0, 0, gi, pl.ds(p, nch, stride=CMP_STRIDE), :] for gi in range(g)], axis=0)
            for p in range(CMP_STRIDE)]
    halves = []
    for half in range(2):
        chunk = jnp.concatenate([(toks[p] + pos_ref[half, p]).astype(BF16) for p in range(CMP_STRIDE)], axis=1)
        halves.append(jnp.dot(chunk, w1_ref[half], preferred_element_type=F32))
    first, second = halves
    h1 = first + pltpu.roll(second, rows - 1, 0)
    out = jnp.dot(jax.nn.silu(h1).astype(BF16), w2_ref[...], preferred_element_type=F32)
    lane = lax.broadcasted_iota(jnp.int32, out.shape, 1)
    out = jnp.where(lane < HEAD_DIM, _k_lane_norm(out, kg_ref[0:1]), out)
    r = lax.broadcasted_iota(jnp.int32, out.shape, 0)
    out = jnp.where((r & (nch - 1)) == nch - 1, 0.0, out)
    o_ref[0] = out.reshape(g, nch, LANES)


def _compress(kv, pos_blk, w1_blk, w2_blk, kg_ext):
    _, b, g, s, _ = kv.shape
    nch = s // CMP_STRIDE
    assert nch & (nch - 1) == 0
    return pl.pallas_call(
        _compress_kernel,
        grid=(b,),
        in_specs=[
            pl.BlockSpec((1, 1, g, s, LANES), lambda bi: (0, bi, 0, 0, 0)),
            pl.BlockSpec(pos_blk.shape, lambda bi: (0, 0, 0, 0)),
            pl.BlockSpec(w1_blk.shape, lambda bi: (0, 0, 0)),
            pl.BlockSpec(w2_blk.shape, lambda bi: (0, 0)),
            pl.BlockSpec((N_BRANCH, LANES), lambda bi: (0, 0)),
        ],
        out_specs=pl.BlockSpec((1, g, nch, LANES), lambda bi: (bi, 0, 0, 0)),
        out_shape=jax.ShapeDtypeStruct((b, g, nch, LANES), F32),
        compiler_params=_cparams(1),
        name="nsa_compress",
    )(kv, pos_blk, w1_blk, w2_blk, kg_ext)


def _sel_slot_count(nqb):
    chunks = lambda i: -(-(i + 1) * Q_BLOCK // SEL_CHUNK)
    counts = {chunks(p) + chunks(nqb - 1 - p) for p in range(nqb // 2)}
    assert len(counts) == 1, counts
    return counts.pop()


def _tile_scores(k_tile, q_t):
    return jnp.concatenate([jnp.dot(k_tile, q_t[:, h:h + MXU_WIDTH], preferred_element_type=F32)
                            for h in range(0, q_t.shape[1], MXU_WIDTH)], axis=1)


def _tile_softmax(s, bias, vt_tile, shifted):
    if bias is not None:
        rows = bias.shape[0]
        biased = s[:rows] + jnp.concatenate([bias] * (s.shape[1] // Q_BLOCK), axis=1)
        s = biased if rows == s.shape[0] else jnp.concatenate([biased, s[rows:]], axis=0)
    m = None if shifted else jnp.max(s, axis=0, keepdims=True)
    p = jnp.exp2(s if shifted else s - m)
    ol = jnp.dot(vt_tile, p.astype(BF16), preferred_element_type=F32)
    return m, ol[HEAD_DIM:HEAD_DIM + 1], ol[:HEAD_DIM]


def _merge_tiles(parts):
    if parts[0][0] is None:
        weights = [None if mine is None else jnp.where(mine, 1.0, 0.0) for _, _, _, mine in parts]
    else:
        ms = [mi if mine is None else jnp.where(mine, mi, NEG_BIG) for mi, _, _, mine in parts]
        m = functools.reduce(jnp.maximum, ms)
        weights = [jnp.exp2(mi - m) for mi in ms]
    l = None
    o = None
    for w, (_, li, oi, _) in zip(weights, parts):
        if w is not None:
            li, oi = w * li, w * oi
        l = li if l is None else l + li
        o = oi if o is None else o + oi
    return o / l


def _attn_kernel(q_ref, gate_ref, kvc_ref, ks_ref, kw_ref, selmap_ref, qg_ref, kg_ref, shift_ref, o_ref,
                 ksn, kwn, vst, vwt, qt_scr, slot_m, slot_l, slot_o, *, shifted):
    pair = pl.program_id(2)
    seq = ks_ref.shape[3]
    ncmp = kvc_ref.shape[2]
    nsel = selmap_ref.shape[0]
    nqb = seq // Q_BLOCK
    cols = NSA_HPG * Q_BLOCK
    win_tiles = WINDOW // LANES
    chunk_tiles = SEL_CHUNK // LANES
    scale = HEAD_DIM ** -0.5 * LOG2_E

    @pl.when(pair == 0)
    def _prepare_kv():
        lane = lax.broadcasted_iota(jnp.int32, (LANES, LANES), 1)
        row = lax.broadcasted_iota(jnp.int32, (LANES, LANES), 0)
        vrow = lax.broadcasted_iota(jnp.int32, (VT_ROWS, LANES), 0)
        vlane = lax.broadcasted_iota(jnp.int32, (VT_ROWS, LANES), 1)
        pick_v = ((vlane == vrow + HEAD_DIM) & (vrow < HEAD_DIM)).astype(BF16)
        ones_row = (vrow == HEAD_DIM).astype(F32)
        v_plane = lambda tile: (lax.dot_general(pick_v, tile.astype(BF16), (((1,), (1,)), ((), ())),
                                                preferred_element_type=F32) + ones_row).astype(BF16)
        pad_mark = (lane == PAD_LANE).astype(BF16)
        for w in range(win_tiles):
            kwn[w * LANES:(w + 1) * LANES, :] = pad_mark
        vwt[0:win_tiles] = jnp.zeros((win_tiles, VT_ROWS, LANES), BF16)

        def body(c, carry):
            r0 = pl.multiple_of(c * LANES, LANES)
            ts = ks_ref[0, 0, 0, pl.ds(r0, LANES), :]
            in_block = (lane - BLOCK_LANE0 == lax.shift_right_logical(r0 + row, SEL_BLOCK.bit_length() - 1))
            marks = (in_block | (lane == SHIFT_LANE)).astype(F32)
            ksn[pl.ds(r0, LANES), :] = (_k_lane_norm(ts, kg_ref[1:2]) + marks).astype(BF16)
            vst[c] = v_plane(ts)
            tw = kw_ref[0, 0, 0, pl.ds(r0, LANES), :]
            kwn[pl.ds(WINDOW + r0, LANES), :] = (
                _k_lane_norm(tw, kg_ref[2:3]) + (lane == SHIFT_LANE).astype(F32)).astype(BF16)
            vwt[win_tiles + c] = v_plane(tw)
            return carry

        lax.fori_loop(0, seq // LANES, body, 0, unroll=4)

    kc = kvc_ref[0, 0]
    kc_b = kc.astype(BF16)
    vc_t = kc.T[HEAD_DIM:, :].astype(BF16)
    block_ids = (pair, nqb - 1 - pair)

    def block_branches(idx):
        i = block_ids[idx]
        t0 = i * Q_BLOCK
        row0 = pl.multiple_of(t0, Q_BLOCK)

        q = q_ref[0, pl.ds(row0, Q_BLOCK), :]
        heads = []
        for h in range(NSA_HPG // 2):
            qt = q[:, h * LANES:(h + 1) * LANES].T
            for a in range(2):
                x = qt[a * HEAD_DIM:(a + 1) * HEAD_DIM]
                ms = jnp.mean(x * x, axis=0, keepdims=True)
                heads.append(x * lax.rsqrt(ms + EPS) * qg_ref[...] * scale)
        q_top = jnp.concatenate(heads, axis=1).astype(BF16)
        mask_rows = lax.broadcasted_iota(jnp.int32, (LANES - HEAD_DIM, cols), 0) + HEAD_DIM
        q_plain = jnp.concatenate([q_top, jnp.zeros((LANES - HEAD_DIM, cols), BF16)], axis=0)
        shift_rows = lambda br: jnp.where(
            mask_rows == SHIFT_LANE, -jnp.concatenate([shift_ref[br:br + 1, :]] * NSA_HPG, axis=1), 0.0)
        q_win = jnp.concatenate(
            [q_top, jnp.where(mask_rows == PAD_LANE, NEG_BIG, shift_rows(2)).astype(BF16)], axis=0)

        tiles = []
        off = 0
        span = WINDOW + Q_BLOCK
        while off < span:
            kc_w = min(SEL_CHUNK, span - off)
            if off < Q_BLOCK:
                rows_b = min(kc_w, Q_BLOCK - off)
                newest = False
            elif off + kc_w > WINDOW:
                assert off >= WINDOW
                rows_b = kc_w
                newest = True
            else:
                rows_b = 0
                newest = False

            def scores(off=off, kc_w=kc_w):
                k_tile = kwn[pl.ds(pl.multiple_of(t0 + off, LANES), kc_w), :]
                return _tile_scores(k_tile, q_win)

            def finish(s, off=off, kc_w=kc_w, rows_b=rows_b, newest=newest):
                bias = None
                if rows_b:
                    r = off + lax.broadcasted_iota(jnp.int32, (rows_b, Q_BLOCK), 0)
                    c = lax.broadcasted_iota(jnp.int32, (rows_b, Q_BLOCK), 1)
                    bias = jnp.where((r - WINDOW <= c) if newest else (r > c), 0.0, NEG_BIG)
                vt_tile = jnp.concatenate([vwt[i + off // LANES + cc] for cc in range(kc_w // LANES)], axis=1)
                return _tile_softmax(s, bias, vt_tile, shifted)

            tiles.append((scores, finish))
            off += kc_w
        yield tiles

        s = _tile_scores(kc_b, q_plain)
        yield
        tq = t0 + (lax.broadcasted_iota(jnp.int32, (ncmp, cols), 1) & (Q_BLOCK - 1))
        cmp_end = lax.broadcasted_iota(jnp.int32, (ncmp, cols), 0) * CMP_STRIDE + (CMP_BLOCK - 1)
        visible = cmp_end <= tq
        s = jnp.where(visible, s, NEG_BIG)
        m = jnp.max(s, axis=0, keepdims=True)
        p = jnp.where(visible, jnp.exp2(s - m), 0.0)
        l = jnp.sum(p, axis=0, keepdims=True)
        p_c = p / jnp.maximum(l, jnp.finfo(F32).tiny)
        o_c = jnp.dot(vc_t, p_c.astype(BF16), preferred_element_type=F32)

        yield
        p_sum = p_c[:, 0:Q_BLOCK]
        for n in range(1, NSA_HPG):
            p_sum = p_sum + p_c[:, n * Q_BLOCK:(n + 1) * Q_BLOCK]
        imp = jnp.dot(selmap_ref[...], p_sum, preferred_element_type=F32, precision=lax.Precision.HIGHEST)
        blk = lax.broadcasted_iota(jnp.int32, (nsel, Q_BLOCK), 0)
        cur = lax.shift_right_logical(t0 + lax.broadcasted_iota(jnp.int32, (nsel, Q_BLOCK), 1), 6)
        valid = blk <= cur
        forced = (blk == 0) | (blk == cur) | (blk == cur - 1)
        val = jnp.where(valid, imp, -jnp.inf)
        val = jnp.where(forced, jnp.inf, val)
        rank = jnp.zeros((nsel, Q_BLOCK), jnp.int32)
        for j2 in range(nsel):
            row = val[j2:j2 + 1, :]
            beats = (row > val) | ((row == val) & (blk > j2))
            rank = rank + beats.astype(jnp.int32)
            yield
        picked = jnp.where((rank < SEL_TOP_N) & valid, 0.0, NEG_BIG).astype(BF16)
        qt_scr[idx] = jnp.concatenate(
            [q_top, jnp.concatenate([picked] * NSA_HPG, axis=1),
             shift_rows(1)[nsel:].astype(BF16)], axis=0)
        gate_t = gate_ref[0, 0, pl.ds(row0, Q_BLOCK), :].T
        return o_c, gate_t

    all_tiles = []
    pending = []
    results = []

    def pump():
        while len(results) + len(pending) < min(len(all_tiles), len(results) + SCORE_LOOKAHEAD + 1):
            pending.append(all_tiles[len(results) + len(pending)][0]())
        results.append(all_tiles[len(results)][1](pending.pop(0)))

    per_block = [None, None]
    running = {idx: block_branches(idx) for idx in range(2)}
    for idx in range(2):
        all_tiles += next(running[idx])
    n_win = len(all_tiles) // 2
    while running:
        for idx, stages in list(running.items()):
            try:
                next(stages)
            except StopIteration as done:
                per_block[idx] = done.value
                del running[idx]

    n_slots = _sel_slot_count(nqb)
    n0 = lax.shift_right_logical(block_ids[0] * Q_BLOCK + Q_BLOCK + SEL_CHUNK - 1, SEL_CHUNK.bit_length() - 1)
    n0_max = (nqb // 2 * Q_BLOCK + SEL_CHUNK - 1) // SEL_CHUNK
    key_minus_query = (lax.broadcasted_iota(jnp.int32, (SEL_CHUNK, Q_BLOCK), 0)
                       - lax.broadcasted_iota(jnp.int32, (SEL_CHUNK, Q_BLOCK), 1))
    sel_tiles = []
    for slot in range(n_slots):
        last_of_block = slot == 0 or slot == n_slots - 1
        if slot == 0:
            which, kb = 0, n0 - 1
        elif slot == n_slots - 1:
            which, kb = 1, n_slots - n0 - 1
        elif slot >= n0_max:
            which, kb = 1, slot - n0
        else:
            which = (slot >= n0).astype(jnp.int32)
            kb = slot - 1 + which * (1 - n0)
        k0 = pl.multiple_of(kb * SEL_CHUNK, SEL_CHUNK)

        def scores(which=which, k0=k0):
            return _tile_scores(ksn[pl.ds(k0, SEL_CHUNK), :], qt_scr[which])

        def finish(s, which=which, kb=kb, k0=k0, last_of_block=last_of_block):
            bias = None
            if last_of_block:
                t0 = block_ids[which] * Q_BLOCK
                bias = jnp.where(key_minus_query <= t0 - k0, 0.0, NEG_BIG)
            vt_tile = jnp.concatenate([vst[kb * chunk_tiles + c] for c in range(chunk_tiles)], axis=1)
            return _tile_softmax(s, bias, vt_tile, shifted)

        sel_tiles.append((scores, finish))

    all_tiles += sel_tiles
    while len(results) < len(all_tiles):
        pump()
    for slot in range(n_slots):
        m, l, o = results[2 * n_win + slot]
        if not shifted:
            slot_m[slot] = m
        slot_l[slot] = l
        slot_o[slot] = o

    for idx in range(2):
        o_c, gate_t = per_block[idx]
        o_w = _merge_tiles([part + (None,) for part in results[idx * n_win:(idx + 1) * n_win]])
        slots = range(0, n0_max) if idx == 0 else range(1, n_slots)
        parts = []
        for slot in slots:
            if slot == 0 or slot >= n0_max:
                mine = None
            else:
                mine = (slot < n0) if idx == 0 else (slot >= n0)
            parts.append((None if shifted else slot_m[slot], slot_l[slot], slot_o[slot], mine))
        o_s = _merge_tiles(parts)
        row0 = pl.multiple_of(block_ids[idx] * Q_BLOCK, Q_BLOCK)
        for hh in range(NSA_HPG // 2):
            tiles = []
            for a in range(2):
                n = 2 * hh + a
                lanes = slice(n * Q_BLOCK, (n + 1) * Q_BLOCK)
                c0 = N_BRANCH * n
                tiles.append(gate_t[c0:c0 + 1, :] * o_c[:, lanes]
                             + gate_t[c0 + 1:c0 + 2, :] * o_s[:, lanes]
                             + gate_t[c0 + 2:c0 + 3, :] * o_w[:, lanes])
            o_ref[0, pl.ds(row0, Q_BLOCK), hh * LANES:(hh + 1) * LANES] = jnp.concatenate(tiles, axis=0).T


def _sel_map_t(seq):
    n_c = seq // CMP_STRIDE - 1
    n_s = seq // SEL_BLOCK
    tok = np.arange(seq)
    start = np.arange(n_c) * CMP_STRIDE
    cover_c = (tok[None, :] >= start[:, None]) & (tok[None, :] < start[:, None] + CMP_BLOCK)
    cover_s = (tok[:, None] // SEL_BLOCK) == np.arange(n_s)[None, :]
    m = cover_c.astype(np.float32) @ cover_s.astype(np.float32) / np.float32(CMP_BLOCK)
    out = np.zeros((n_s, n_c + 1), np.float32)
    out[:, :n_c] = m.T
    return jnp.asarray(out)


def _score_bounds(q_gain, k_gain):
    bound = 1.01 * HEAD_DIM ** 0.5 * LOG2_E * jnp.max(jnp.abs(q_gain)) * jnp.max(jnp.abs(k_gain), axis=-1)
    return jnp.broadcast_to(bound[:, None], (N_BRANCH, LANES)).astype(F32)


def _attention(q, gates, kvc, kv, qg_t, kg_ext, shift, shifted):
    b, s, _ = q.shape
    g = NSA_GROUPS
    nqb = s // Q_BLOCK
    ncmp = s // CMP_STRIDE
    nsel = s // SEL_BLOCK
    assert s % SEL_CHUNK == 0 and nsel % SUBLANES == 0 and nqb % 2 == 0
    gw = NSA_HPG * HEAD_DIM
    cols = NSA_HPG * Q_BLOCK
    n_slots = _sel_slot_count(nqb)

    def kv_spec(j):
        return pl.BlockSpec((1, 1, 1, s, LANES), lambda bi, gi, i: (j, bi, gi, 0, 0))

    return pl.pallas_call(
        functools.partial(_attn_kernel, shifted=shifted),
        grid=(b, g, nqb // 2),
        in_specs=[
            pl.BlockSpec((1, s, gw), lambda bi, gi, i: (bi, 0, gi)),
            pl.BlockSpec((1, 1, s, LANES), lambda bi, gi, i: (bi, gi, 0, 0)),
            pl.BlockSpec((1, 1, ncmp, LANES), lambda bi, gi, i: (bi, gi, 0, 0)),
            kv_spec(1), kv_spec(2),
            pl.BlockSpec((nsel, ncmp), lambda bi, gi, i: (0, 0)),
            pl.BlockSpec((HEAD_DIM, Q_BLOCK), lambda bi, gi, i: (0, 0)),
            pl.BlockSpec((N_BRANCH, LANES), lambda bi, gi, i: (0, 0)),
            pl.BlockSpec((N_BRANCH, LANES), lambda bi, gi, i: (0, 0)),
        ],
        out_specs=pl.BlockSpec((1, s, gw), lambda bi, gi, i: (bi, 0, gi)),
        out_shape=jax.ShapeDtypeStruct((b, s, Q_DIM), F32),
        scratch_shapes=[
            pltpu.VMEM((s, LANES), BF16),
            pltpu.VMEM((WINDOW + s, LANES), BF16),
            pltpu.VMEM((s // LANES, VT_ROWS, LANES), BF16),
            pltpu.VMEM(((WINDOW + s) // LANES, VT_ROWS, LANES), BF16),
            pltpu.VMEM((2, LANES, cols), BF16),
            pltpu.VMEM((n_slots, 1, cols), F32),
            pltpu.VMEM((n_slots, 1, cols), F32),
            pltpu.VMEM((n_slots, HEAD_DIM, cols), F32),
        ],
        compiler_params=_cparams(3),
        name="nsa_attention",
    )(q, gates, kvc, kv, kv, _sel_map_t(s), qg_t, kg_ext, shift)


def _ffn_core(x, g_ref, sc_ref, sh_ref, gate_ref, wg_ref, wu_ref, wo_ref):
    hh = _norm_mod(x, g_ref[...], sc_ref[0], sh_ref[0]).astype(BF16)
    a = jnp.dot(hh, wg_ref[...], preferred_element_type=F32)
    u = jnp.dot(hh, wu_ref[...], preferred_element_type=F32)
    act = (jax.nn.silu(a) * u).astype(BF16)
    return x + gate_ref[0] * jnp.dot(act, wo_ref[...], preferred_element_type=F32)


def _ffn_kernel(x_ref, g_ref, sc_ref, sh_ref, gate_ref, wg_ref, wu_ref, wo_ref, o_ref):
    o_ref[0] = _ffn_core(x_ref[0], g_ref, sc_ref, sh_ref, gate_ref, wg_ref, wu_ref, wo_ref)


def _mix_ffn_kernel(y_ref, wmix_ref, mix_gate_ref, x_ref, g_ref, sc_ref, sh_ref, gate_ref, wg_ref, wu_ref, wo_ref,
                    o_ref):
    mix = jnp.dot(y_ref[0].astype(BF16), wmix_ref[...], preferred_element_type=F32)
    x = x_ref[0] + mix_gate_ref[0] * mix
    o_ref[0] = _ffn_core(x, g_ref, sc_ref, sh_ref, gate_ref, wg_ref, wu_ref, wo_ref)


def _ffn(x, gain, sc, sh, gate, w_in, w_out, layer, mixer_out=None):
    b, s, d = x.shape
    hidden = w_out.shape[1]
    tm = min(ROW_TILE, s)
    vec = pl.BlockSpec((1, 1, d), lambda bi, i: (bi, 0, 0))
    rows = lambda width: pl.BlockSpec((1, tm, width), lambda bi, i: (bi, i, 0))
    in_specs = [
        rows(d), _resident((1, d)), vec, vec, vec,
        _layer_plane((d, hidden), layer, 0), _layer_plane((d, hidden), layer, 1),
        _layer_plane((hidden, d), layer),
    ]
    args = (x, gain, sc, sh, gate, w_in, w_in, w_out)
    body = _ffn_kernel
    if mixer_out is not None:
        y, w_mix, j, mix_gate = mixer_out
        in_specs = [rows(y.shape[-1]), _layer_plane(w_mix.shape[1:], j), vec] + in_specs
        args = (y, w_mix, mix_gate) + args
        body = _mix_ffn_kernel
    return pl.pallas_call(
        body,
        grid=(b, s // tm),
        in_specs=in_specs,
        out_specs=rows(d),
        out_shape=jax.ShapeDtypeStruct((b, s, d), F32),
        compiler_params=_cparams(2),
        name="ffn",
    )(*args)


def _band_starts(width, block_w):
    band = 2 * MXU_WIDTH
    starts = []
    for c0 in range(0, width, MXU_WIDTH):
        c1 = min(c0 + MXU_WIDTH, width)
        lo = (c0 // block_w) * block_w
        hi = ((c1 - 1) // block_w + 1) * block_w
        k0 = min((lo // LANES) * LANES, width - band)
        assert k0 >= 0 and k0 <= lo and hi <= k0 + band
        starts.append(k0)
    return starts


def _band_weights(w_a, w_x):
    nb, bw, _ = w_a.shape
    width = nb * bw
    spread = jnp.asarray(np.tile(np.eye(bw, dtype=np.float32), (1, nb)), dtype=BF16)
    blk = np.arange(width) // bw
    on_diag = jnp.asarray(blk[:, None] == blk[None, :])
    dense = lambda w: jnp.where(
        on_diag, jnp.dot(w.reshape(width, bw).astype(BF16), spread, preferred_element_type=BF16), 0)
    da, dx = dense(w_a), dense(w_x)
    band = 2 * MXU_WIDTH
    tiles = []
    for t, k0 in enumerate(_band_starts(width, bw)):
        c0 = t * MXU_WIDTH
        c1 = min(c0 + MXU_WIDTH, width)
        pad = ((0, 0), (0, MXU_WIDTH - (c1 - c0)))
        tiles.append(jnp.concatenate(
            [jnp.pad(da[k0:k0 + band, c0:c1], pad), jnp.pad(dx[k0:k0 + band, c0:c1], pad)], axis=1))
    return 0.5 * jnp.stack(tiles)


def _rg_mixer_kernel(x_ref, xn_ref, g_ref, sc_ref, sh_ref, res_gate_ref, win_ref, cw_ref, cb_ref, wb_ref, ba_ref,
                     bx_ref, lam_ref, wout_ref, o_ref,
                     u_a, u_b, gate_a, gate_b, h_scr, a_scr, b_scr, y_scr, *, band_starts):
    step = pl.program_id(0)
    nb, ts, d = x_ref.shape
    width = wout_ref.shape[0]
    rows = nb * ts
    halo = CONV_WIDTH - 1
    band = 2 * MXU_WIDTH

    pieces = [(c0, min(c0 + MXU_WIDTH, 2 * width)) for c0 in range(0, 2 * width, MXU_WIDTH)]

    def project_piece(hh, piece, u_dst, gate_dst):
        c0, c1 = pieces[piece]
        res = jnp.dot(hh, win_ref[:, c0:c1], preferred_element_type=F32)
        n_gate = max(0, min(c1, width) - c0)
        if n_gate:
            gate_dst[:, c0:c0 + n_gate] = res[:, :n_gate]
        if c1 > width:
            u0 = max(c0, width) - width
            u_dst[:, :, u0:c1 - width] = jnp.swapaxes(res[:, n_gate:].reshape(nb, ts, c1 - c0 - n_gate), 0, 1)

    def normed(tile_ref):
        return _norm_mod(tile_ref[...], g_ref[...], sc_ref[...], sh_ref[...]).reshape(rows, d).astype(BF16)

    @pl.when(step == 0)
    def _init():
        u_b[...] = jnp.zeros((ts, nb, width), F32)
        h_scr[...] = jnp.zeros(h_scr.shape, F32)
        hh = normed(x_ref)
        for piece in range(len(pieces)):
            project_piece(hh, piece, u_a, gate_a)

    def tile_step(u_cur, gate_cur, u_nxt, gate_nxt):
        prev_tail = u_nxt[ts - halo:ts]
        hh_next = normed(xn_ref)
        todo = list(range(len(pieces)))
        per_stage = -(-len(pieces) // len(band_starts))

        uc = jnp.zeros((ts, nb, width), F32) + cb_ref[...]
        for k in range(CONV_WIDTH):
            shifted = u_cur[...] if k == halo else jnp.concatenate([prev_tail[k:], u_cur[0:ts - halo + k]], axis=0)
            uc = uc + cw_ref[k:k + 1, :] * shifted
        uc = uc.reshape(rows, width)
        ub = uc.astype(BF16)

        z = -lam_ref[...]
        log_a_unit = -RG_C * (jnp.maximum(z, 0.0) + jnp.log1p(jnp.exp(-jnp.abs(z))))

        for t, k0 in enumerate(band_starts):
            c0 = t * MXU_WIDTH
            c1 = min(c0 + MXU_WIDTH, width)
            n = c1 - c0
            zz = jnp.dot(ub[:, k0:k0 + band], wb_ref[t], preferred_element_type=F32)
            for piece in todo[:per_stage]:
                project_piece(hh_next, piece, u_nxt, gate_nxt)
            todo = todo[per_stage:]
            half_unit = 0.5 * log_a_unit[:, c0:c1]
            half_u = 0.5 * uc[:, c0:c1]
            log_a = half_unit * jnp.tanh(zz[:, :n] + ba_ref[:, c0:c1]) + half_unit
            gated_u = half_u * jnp.tanh(zz[:, MXU_WIDTH:MXU_WIDTH + n] + bx_ref[:, c0:c1]) + half_u
            a = jnp.exp(log_a)
            bb = jnp.sqrt(-jnp.tanh(log_a) * (a * a + 1.0)) * gated_u
            for ct in range(n // LANES):
                lanes = slice(ct * LANES, (ct + 1) * LANES)
                a_scr[c0 // LANES + ct] = a[:, lanes]
                b_scr[c0 // LANES + ct] = bb[:, lanes]

        n_ct = width // LANES

        def scan_step(t, h):
            rows_t = pl.ds(pl.multiple_of(t * nb, nb), nb)
            new = []
            for ct in range(n_ct):
                hc = a_scr[ct, rows_t, :] * h[ct] + b_scr[ct, rows_t, :]
                b_scr[ct, rows_t, :] = hc
                new.append(hc)
            return tuple(new)

        h_fin = lax.fori_loop(0, ts, scan_step, tuple(h_scr[ct] for ct in range(n_ct)), unroll=SUBLANES)
        for ct in range(n_ct):
            lanes = slice(ct * LANES, (ct + 1) * LANES)
            h_scr[ct] = h_fin[ct]
            hs = jnp.swapaxes(b_scr[ct].reshape(ts, nb, LANES), 0, 1)
            y_scr[:, lanes] = (hs.reshape(rows, LANES) * jax.nn.gelu(gate_cur[:, lanes])).astype(BF16)
        mix = jnp.dot(y_scr[...], wout_ref[...], preferred_element_type=F32)
        o_ref[...] = x_ref[...] + res_gate_ref[...] * mix.reshape(nb, ts, d)

    @pl.when(step % 2 == 0)
    def _even():
        tile_step(u_a, gate_a, u_b, gate_b)

    @pl.when(step % 2 == 1)
    def _odd():
        tile_step(u_b, gate_b, u_a, gate_a)


def _rg_mixer(x, gain, sc, sh, res_gate, w_in, conv_w, conv_b, w_band, b_a, b_x, lam, w_out, j, block_w):
    b, s, d = x.shape
    width = w_out.shape[1]
    assert b == SUBLANES
    ts = min(RG_TIME_TILE, s)
    rows = b * ts
    band_starts = tuple(_band_starts(width, block_w))
    n_tiles = s // ts
    tile = pl.BlockSpec((b, ts, d), lambda t: (0, t, 0))
    next_tile = pl.BlockSpec((b, ts, d), lambda t: (0, jnp.minimum(t + 1, n_tiles - 1), 0))
    mod = _resident((b, 1, d))
    vec = _resident((1, width))
    kern = functools.partial(_rg_mixer_kernel, band_starts=band_starts)
    return pl.pallas_call(
        kern,
        grid=(n_tiles,),
        in_specs=[
            tile, next_tile, _resident((1, d)), mod, mod, mod,
            _layer_plane(w_in.shape[1:], j),
            _resident((CONV_WIDTH, width)), vec,
            _resident(w_band.shape),
            vec, vec, vec,
            _layer_plane(w_out.shape[1:], j),
        ],
        out_specs=tile,
        out_shape=jax.ShapeDtypeStruct((b, s, d), F32),
        scratch_shapes=[
            pltpu.VMEM((ts, b, width), F32),
            pltpu.VMEM((ts, b, width), F32),
            pltpu.VMEM((rows, width), F32),
            pltpu.VMEM((rows, width), F32),
            pltpu.VMEM((width // LANES, b, LANES), F32),
            pltpu.VMEM((width // LANES, rows, LANES), F32),
            pltpu.VMEM((width // LANES, rows, LANES), F32),
            pltpu.VMEM((rows, width), BF16),
        ],
        compiler_params=_cparams(1),
        name="rg_mixer",
    )(x, x, gain, sc, sh, res_gate, w_in, conv_w, conv_b, w_band, b_a, b_x, lam, w_out)


def kernel(x, c, ada_w, ada_b, norm1_g, norm2_g, nsa_w_in, nsa_w_out, nsa_cmp_pos, nsa_cmp_w1, nsa_cmp_w2, nsa_q_gain, nsa_k_gain, rg_w_in, rg_conv_w, rg_conv_b, rg_w_a, rg_b_a, rg_w_x, rg_b_x, rg_lam, rg_w_out, ffn_w_in, ffn_w_out):
    depth, d, _ = ada_w.shape
    b = x.shape[0]
    mod = _ada_mod(c, ada_w, ada_b).reshape(depth, b, 6, 1, d)
    ffn_w_in, ffn_w_out, nsa_w_out, rg_w_in, rg_w_out = (
        w.astype(BF16) for w in (ffn_w_in, ffn_w_out, nsa_w_out, rg_w_in, rg_w_out))
    for layer in range(depth):
        sh1, sc1, g1, sh2, sc2, g2 = (mod[layer, :, k] for k in range(6))
        n1 = norm1_g[layer].reshape(1, d)
        n2 = norm2_g[layer].reshape(1, d)
        j = layer // N_MIXERS
        if layer % N_MIXERS == 0:
            kg_ext = jnp.concatenate([nsa_k_gain[j], jnp.zeros_like(nsa_k_gain[j])], axis=1)
            qg_t = jnp.broadcast_to(nsa_q_gain[j][:, None], (HEAD_DIM, Q_BLOCK))
            q, kv, gates = _nsa_proj(x, n1, sc1, sh1, _nsa_weight_layout(nsa_w_in[j]))
            kvc = _compress(kv, *_compress_weights(nsa_cmp_pos[j], nsa_cmp_w1[j], nsa_cmp_w2[j]), kg_ext)
            bounds = _score_bounds(nsa_q_gain[j], nsa_k_gain[j])
            o = lax.cond(
                2.0 * jnp.max(bounds) <= MAX_SHIFTED_SCORE_RANGE,
                lambda *a: _attention(*a, bounds, True),
                lambda *a: _attention(*a, jnp.zeros_like(bounds), False),
                q, gates, kvc, kv, qg_t, kg_ext)
            mixer_out = (o, nsa_w_out, j, g1)
        else:
            width = rg_w_out.shape[1]
            block_w = rg_w_a.shape[-1]
            x = _rg_mixer(x, n1, sc1, sh1, g1, rg_w_in,
                          rg_conv_w[j], rg_conv_b[j].reshape(1, width),
                          _band_weights(rg_w_a[j], rg_w_x[j]),
                          0.5 * rg_b_a[j].reshape(1, width), 0.5 * rg_b_x[j].reshape(1, width),
                          rg_lam[j].reshape(1, width), rg_w_out, j, block_w)
            mixer_out = None
        x = _ffn(x, n2, sc2, sh2, g2, ffn_w_in, ffn_w_out, layer, mixer_out)
    return x
```

```python
import functools

import numpy as np
import jax
import jax.numpy as jnp
from jax import lax
from jax.experimental import pallas as pl
from jax.experimental.pallas import tpu as pltpu

F32 = jnp.float32
BF16 = jnp.bfloat16

EPS = 1e-6
N_MIXERS = 2

NSA_HEADS = 16
NSA_GROUPS = 4
NSA_HPG = NSA_HEADS // NSA_GROUPS
HEAD_DIM = 64
CMP_STRIDE = 16
CMP_BLOCK = 2 * CMP_STRIDE
SEL_BLOCK = 64
SEL_TOP_N = 8
WINDOW = 512
Q_BLOCK = 128
N_BRANCH = 3
Q_DIM = NSA_HEADS * HEAD_DIM
KV_DIM = NSA_GROUPS * HEAD_DIM
GATE_COLS = N_BRANCH * NSA_HEADS
GATE_PER_GROUP = N_BRANCH * NSA_HPG

RNN_BLOCKS = 16
CONV_WIDTH = 4
RG_C = 8.0

LANES = 128
SUBLANES = 8
MXU_WIDTH = 256
VMEM_LIMIT_BYTES = 56 * 1024 * 1024

NEG_BIG = -1e30
LOG2_E = 1.4426950408889634
BLOCK_LANE0 = HEAD_DIM
PAD_LANE = LANES - 1
SHIFT_LANE = LANES - 2
MAX_SHIFTED_SCORE_RANGE = 100.0
VT_ROWS = HEAD_DIM + 16
SEL_CHUNK = 256
SCORE_LOOKAHEAD = 3
PROLOGUE_ROUNDS_PER_TILE = 1
RG_TIME_TILE = 64
ROW_TILE = 512


def _cparams(n_axes, flags=None):
    return pltpu.CompilerParams(
        dimension_semantics=("arbitrary",) * n_axes,
        vmem_limit_bytes=VMEM_LIMIT_BYTES,
        flags=flags,
    )


def _resident(shape):
    return pl.BlockSpec(shape, lambda *_: (0,) * len(shape), pipeline_mode=pl.Buffered(1))


def _layer_plane(shape, layer, col_block=0):
    return pl.BlockSpec((None,) + tuple(shape), lambda *_: (layer, 0, col_block), pipeline_mode=pl.Buffered(1))


def _norm_mod(x, gain, scale, shift):
    ms = jnp.mean(x * x, axis=-1, keepdims=True)
    y = x * lax.rsqrt(ms + EPS) * gain
    return y * (1.0 + scale) + shift


def _ada_kernel(c_ref, w_ref, b_ref, o_ref):
    cond = jax.nn.silu(c_ref[...])
    o_ref[0] = jnp.dot(cond, w_ref[0], preferred_element_type=F32,
                       precision=lax.Precision.HIGHEST) + b_ref[0]


def _ada_mod(c, ada_w, ada_b):
    depth, d, n = ada_w.shape
    b = c.shape[0]
    tn = n // 4
    return pl.pallas_call(
        _ada_kernel,
        grid=(depth, n // tn),
        in_specs=[
            pl.BlockSpec((b, d), lambda l, j: (0, 0)),
            pl.BlockSpec((1, d, tn), lambda l, j: (l, 0, j)),
            pl.BlockSpec((1, 1, tn), lambda l, j: (l, 0, j)),
        ],
        out_specs=pl.BlockSpec((1, b, tn), lambda l, j: (l, 0, j)),
        out_shape=jax.ShapeDtypeStruct((depth, b, n), F32),
        compiler_params=_cparams(2),
        name="ada_mod",
    )(c, ada_w, ada_b.reshape(depth, 1, n))


def _nsa_weight_layout(w_in):
    w_in = w_in.astype(BF16)
    src = np.arange(6 * KV_DIM).reshape(N_BRANCH, 2, NSA_GROUPS, HEAD_DIM).transpose(0, 2, 1, 3).reshape(-1)
    perm = np.zeros((6 * KV_DIM, 6 * KV_DIM), np.float32)
    perm[src, np.arange(6 * KV_DIM)] = 1.0
    kv = jnp.dot(w_in[:, Q_DIM:Q_DIM + 6 * KV_DIM], jnp.asarray(perm, dtype=BF16), preferred_element_type=BF16)
    gates = jnp.pad(w_in[:, Q_DIM + 6 * KV_DIM:], ((0, 0), (0, LANES - GATE_COLS)))
    return jnp.concatenate([w_in[:, :Q_DIM], kv, gates], axis=1)


def _nsa_proj_kernel(x_ref, g_ref, sc_ref, sh_ref, w_ref, q_ref, kv_ref, gate_ref):
    hh = _norm_mod(x_ref[0], g_ref[...], sc_ref[0], sh_ref[0])
    p = jnp.dot(hh.astype(BF16), w_ref[...], preferred_element_type=F32)
    q_ref[0] = p[:, :Q_DIM]
    for j in range(N_BRANCH):
        for g in range(NSA_GROUPS):
            lo = Q_DIM + (j * NSA_GROUPS + g) * LANES
            kv_ref[j, 0, g] = p[:, lo:lo + LANES]
    gate = jax.nn.sigmoid(p[:, Q_DIM + 6 * KV_DIM:])
    for g in range(NSA_GROUPS):
        shift = (LANES - GATE_PER_GROUP * g) % LANES
        gate_ref[0, g] = gate if shift == 0 else pltpu.roll(gate, shift, 1)


def _nsa_proj(x, gain, sc, sh, w):
    b, s, d = x.shape
    tm = min(ROW_TILE, s)
    n = w.shape[1]
    vec = pl.BlockSpec((1, 1, d), lambda bi, i: (bi, 0, 0))
    return pl.pallas_call(
        _nsa_proj_kernel,
        grid=(b, s // tm),
        in_specs=[
            pl.BlockSpec((1, tm, d), lambda bi, i: (bi, i, 0)),
            pl.BlockSpec((1, d), lambda bi, i: (0, 0)),
            vec, vec,
            pl.BlockSpec((d, n), lambda bi, i: (0, 0)),
        ],
        out_specs=[
            pl.BlockSpec((1, tm, Q_DIM), lambda bi, i: (bi, i, 0)),
            pl.BlockSpec((N_BRANCH, 1, NSA_GROUPS, tm, LANES), lambda bi, i: (0, bi, 0, i, 0)),
            pl.BlockSpec((1, NSA_GROUPS, tm, LANES), lambda bi, i: (bi, 0, i, 0)),
        ],
        out_shape=[
            jax.ShapeDtypeStruct((b, s, Q_DIM), F32),
            jax.ShapeDtypeStruct((N_BRANCH, b, NSA_GROUPS, s, LANES), F32),
            jax.ShapeDtypeStruct((b, NSA_GROUPS, s, LANES), F32),
        ],
        compiler_params=_cparams(2),
        name="nsa_proj",
    )(x, gain, sc, sh, w)


def _k_lane_norm(t, gain_ext):
    lane = lax.broadcasted_iota(jnp.int32, t.shape, 1)
    ms = jnp.sum(jnp.where(lane < HEAD_DIM, t * t, 0.0), axis=-1, keepdims=True) * (1.0 / HEAD_DIM)
    return t * lax.rsqrt(ms + EPS) * gain_ext


def _compress_weights(pos, w1, w2):
    hidden = w1.shape[-1]
    w1p = w1.reshape(2, 2, CMP_STRIDE, HEAD_DIM, hidden)
    z = jnp.zeros_like(w1p[0])
    wk = jnp.concatenate([w1p[0], z], axis=-1)
    wv = jnp.concatenate([z, w1p[1]], axis=-1)
    w1_blk = jnp.concatenate([wk, wv], axis=2).astype(BF16)
    w1_blk = w1_blk.reshape(2, CMP_STRIDE * LANES, 2 * hidden)
    pos_blk = jnp.concatenate([pos[0], pos[1]], axis=-1).reshape(2, CMP_STRIDE, 1, LANES)
    z2 = jnp.zeros_like(w2[0])
    w2_blk = jnp.concatenate([jnp.concatenate([w2[0], z2], axis=1),
                              jnp.concatenate([z2, w2[1]], axis=1)], axis=0).astype(BF16)
    return pos_blk, w1_blk, w2_blk


def _compress_kernel(kv_ref, pos_ref, w1_ref, w2_ref, kg_ref, o_ref):
    g, seq = kv_ref.shape[2:4]
    nch = seq // CMP_STRIDE
    rows = g * nch
    toks = [jnp.concatenate([kv_ref[0, 0, gi, pl.ds(p, nch, stride=CMP_STRIDE), :] for gi in range(g)], axis=0)
            for p in range(CMP_STRIDE)]
    halves = []
    for half in range(2):
        chunk = jnp.concatenate([(toks[p] + pos_ref[half, p]).astype(BF16) for p in range(CMP_STRIDE)], axis=1)
        halves.append(jnp.dot(chunk, w1_ref[half], preferred_element_type=F32))
    first, second = halves
    h1 = first + pltpu.roll(second, rows - 1, 0)
    out = jnp.dot(jax.nn.silu(h1).astype(BF16), w2_ref[...], preferred_element_type=F32)
    lane = lax.broadcasted_iota(jnp.int32, out.shape, 1)
    out = jnp.where(lane < HEAD_DIM, _k_lane_norm(out, kg_ref[0:1]), out)
    r = lax.broadcasted_iota(jnp.int32, out.shape, 0)
    out = jnp.where((r & (nch - 1)) == nch - 1, 0.0, out)
    o_ref[0] = out.reshape(g, nch, LANES)


def _compress(kv, pos_blk, w1_blk, w2_blk, kg_ext):
    _, b, g, s, _ = kv.shape
    nch = s // CMP_STRIDE
    assert nch & (nch - 1) == 0
    return pl.pallas_call(
        _compress_kernel,
        grid=(b,),
        in_specs=[
            pl.BlockSpec((1, 1, g, s, LANES), lambda bi: (0, bi, 0, 0, 0)),
            pl.BlockSpec(pos_blk.shape, lambda bi: (0, 0, 0, 0)),
            pl.BlockSpec(w1_blk.shape, lambda bi: (0, 0, 0)),
            pl.BlockSpec(w2_blk.shape, lambda bi: (0, 0)),
            pl.BlockSpec((N_BRANCH, LANES), lambda bi: (0, 0)),
        ],
        out_specs=pl.BlockSpec((1, g, nch, LANES), lambda bi: (bi, 0, 0, 0)),
        out_shape=jax.ShapeDtypeStruct((b, g, nch, LANES), F32),
        compiler_params=_cparams(1),
        name="nsa_compress",
    )(kv, pos_blk, w1_blk, w2_blk, kg_ext)


def _sel_slot_count(nqb):
    chunks = lambda i: -(-(i + 1) * Q_BLOCK // SEL_CHUNK)
    counts = {chunks(p) + chunks(nqb - 1 - p) for p in range(nqb // 2)}
    assert len(counts) == 1, counts
    return counts.pop()


def _tile_scores(k_tile, q_t):
    return jnp.concatenate([jnp.dot(k_tile, q_t[:, h:h + MXU_WIDTH], preferred_element_type=F32)
                            for h in range(0, q_t.shape[1], MXU_WIDTH)], axis=1)


def _tile_softmax(s, bias, vt_tile, shifted):
    if bias is not None:
        rows = bias.shape[0]
        biased = s[:rows] + jnp.concatenate([bias] * (s.shape[1] // Q_BLOCK), axis=1)
        s = biased if rows == s.shape[0] else jnp.concatenate([biased, s[rows:]], axis=0)
    m = None if shifted else jnp.max(s, axis=0, keepdims=True)
    p = jnp.exp2(s if shifted else s - m)
    ol = jnp.dot(vt_tile, p.astype(BF16), preferred_element_type=F32)
    return m, ol[HEAD_DIM:HEAD_DIM + 1], ol[:HEAD_DIM]


def _merge_tiles(parts):
    if parts[0][0] is None:
        weights = [None if mine is None else jnp.where(mine, 1.0, 0.0) for _, _, _, mine in parts]
    else:
        ms = [mi if mine is None else jnp.where(mine, mi, NEG_BIG) for mi, _, _, mine in parts]
        m = functools.reduce(jnp.maximum, ms)
        weights = [jnp.exp2(mi - m) for mi in ms]
    l = None
    o = None
    for w, (_, li, oi, _) in zip(weights, parts):
        if w is not None:
            li, oi = w * li, w * oi
        l = li if l is None else l + li
        o = oi if o is None else o + oi
    return o / l


def _attn_kernel(q_ref, gate_ref, kvc_ref, ks_ref, kw_ref, selmap_ref, qg_ref, kg_ref, shift_ref, o_ref,
                 ksn, kwn, vst, vwt, qsel_scr, qwin_scr, oc_scr, slot_m, slot_l, slot_o, *, shifted):
    pair = pl.program_id(2)
    seq = ks_ref.shape[3]
    ncmp = kvc_ref.shape[2]
    nsel = selmap_ref.shape[0]
    nqb = seq // Q_BLOCK
    cols = NSA_HPG * Q_BLOCK
    win_tiles = WINDOW // LANES
    chunk_tiles = SEL_CHUNK // LANES
    scale = HEAD_DIM ** -0.5 * LOG2_E

    @pl.when(pair == 0)
    def _prepare_kv():
        lane = lax.broadcasted_iota(jnp.int32, (LANES, LANES), 1)
        row = lax.broadcasted_iota(jnp.int32, (LANES, LANES), 0)
        vrow = lax.broadcasted_iota(jnp.int32, (VT_ROWS, LANES), 0)
        vlane = lax.broadcasted_iota(jnp.int32, (VT_ROWS, LANES), 1)
        pick_v = ((vlane == vrow + HEAD_DIM) & (vrow < HEAD_DIM)).astype(BF16)
        ones_row = (vrow == HEAD_DIM).astype(F32)
        v_plane = lambda tile: (lax.dot_general(pick_v, tile.astype(BF16), (((1,), (1,)), ((), ())),
                                                preferred_element_type=F32) + ones_row).astype(BF16)
        pad_mark = (lane == PAD_LANE).astype(BF16)
        for w in range(win_tiles):
            kwn[w * LANES:(w + 1) * LANES, :] = pad_mark
        vwt[0:win_tiles] = jnp.zeros((win_tiles, VT_ROWS, LANES), BF16)

        def body(c, carry):
            r0 = pl.multiple_of(c * LANES, LANES)
            ts = ks_ref[0, 0, 0, pl.ds(r0, LANES), :]
            in_block = (lane - BLOCK_LANE0 == lax.shift_right_logical(r0 + row, SEL_BLOCK.bit_length() - 1))
            marks = (in_block | (lane == SHIFT_LANE)).astype(F32)
            ksn[pl.ds(r0, LANES), :] = (_k_lane_norm(ts, kg_ref[1:2]) + marks).astype(BF16)
            vst[c] = v_plane(ts)
            tw = kw_ref[0, 0, 0, pl.ds(r0, LANES), :]
            kwn[pl.ds(WINDOW + r0, LANES), :] = (
                _k_lane_norm(tw, kg_ref[2:3]) + (lane == SHIFT_LANE).astype(F32)).astype(BF16)
            vwt[win_tiles + c] = v_plane(tw)
            return carry

        lax.fori_loop(0, seq // LANES, body, 0, unroll=4)

    kc = kvc_ref[0, 0]
    kc_b = kc.astype(BF16)
    vc_t = kc.T[HEAD_DIM:, :].astype(BF16)
    n_pairs = nqb // 2
    pair_blocks = lambda p: (p, nqb - 1 - p)
    bufs = lambda half: (qsel_scr.at[half], qwin_scr.at[half], oc_scr.at[half])

    def block_prologue(idx, of_pair, bufs):
        qsel_buf, qwin_buf, oc_buf = bufs
        i = pair_blocks(of_pair)[idx]
        t0 = i * Q_BLOCK
        row0 = pl.multiple_of(t0, Q_BLOCK)

        q = q_ref[0, pl.ds(row0, Q_BLOCK), :]
        heads = []
        for h in range(NSA_HPG // 2):
            qt = q[:, h * LANES:(h + 1) * LANES].T
            for a in range(2):
                x = qt[a * HEAD_DIM:(a + 1) * HEAD_DIM]
                ms = jnp.mean(x * x, axis=0, keepdims=True)
                heads.append(x * lax.rsqrt(ms + EPS) * qg_ref[...] * scale)
        q_top = jnp.concatenate(heads, axis=1).astype(BF16)
        mask_rows = lax.broadcasted_iota(jnp.int32, (LANES - HEAD_DIM, cols), 0) + HEAD_DIM
        q_plain = jnp.concatenate([q_top, jnp.zeros((LANES - HEAD_DIM, cols), BF16)], axis=0)
        shift_rows = lambda br: jnp.where(
            mask_rows == SHIFT_LANE, -jnp.concatenate([shift_ref[br:br + 1, :]] * NSA_HPG, axis=1), 0.0)
        qwin_buf[idx] = jnp.concatenate(
            [q_top, jnp.where(mask_rows == PAD_LANE, NEG_BIG, shift_rows(2)).astype(BF16)], axis=0)
        yield

        s = _tile_scores(kc_b, q_plain)
        yield
        tq = t0 + (lax.broadcasted_iota(jnp.int32, (ncmp, cols), 1) & (Q_BLOCK - 1))
        cmp_end = lax.broadcasted_iota(jnp.int32, (ncmp, cols), 0) * CMP_STRIDE + (CMP_BLOCK - 1)
        visible = cmp_end <= tq
        s = jnp.where(visible, s, NEG_BIG)
        m = jnp.max(s, axis=0, keepdims=True)
        p = jnp.where(visible, jnp.exp2(s - m), 0.0)
        l = jnp.sum(p, axis=0, keepdims=True)
        p_c = p / jnp.maximum(l, jnp.finfo(F32).tiny)
        oc_buf[idx] = jnp.dot(vc_t, p_c.astype(BF16), preferred_element_type=F32)

        yield
        p_sum = p_c[:, 0:Q_BLOCK]
        for n in range(1, NSA_HPG):
            p_sum = p_sum + p_c[:, n * Q_BLOCK:(n + 1) * Q_BLOCK]
        imp = jnp.dot(selmap_ref[...], p_sum, preferred_element_type=F32, precision=lax.Precision.HIGHEST)
        blk = lax.broadcasted_iota(jnp.int32, (nsel, Q_BLOCK), 0)
        cur = lax.shift_right_logical(t0 + lax.broadcasted_iota(jnp.int32, (nsel, Q_BLOCK), 1), 6)
        valid = blk <= cur
        forced = (blk == 0) | (blk == cur) | (blk == cur - 1)
        val = jnp.where(valid, imp, -jnp.inf)
        val = jnp.where(forced, jnp.inf, val)
        rank = jnp.zeros((nsel, Q_BLOCK), jnp.int32)
        for j2 in range(nsel):
            row = val[j2:j2 + 1, :]
            beats = (row > val) | ((row == val) & (blk > j2))
            rank = rank + beats.astype(jnp.int32)
            yield
        picked = jnp.where((rank < SEL_TOP_N) & valid, 0.0, NEG_BIG).astype(BF16)
        qsel_buf[idx] = jnp.concatenate(
            [q_top, jnp.concatenate([picked] * NSA_HPG, axis=1),
             shift_rows(1)[nsel:].astype(BF16)], axis=0)

    def window_tiles(idx, of_pair, qwin_buf):
        i = pair_blocks(of_pair)[idx]
        t0 = i * Q_BLOCK
        tiles = []
        off = 0
        span = WINDOW + Q_BLOCK
        while off < span:
            kc_w = min(SEL_CHUNK, span - off)
            if off < Q_BLOCK:
                rows_b = min(kc_w, Q_BLOCK - off)
                newest = False
            elif off + kc_w > WINDOW:
                assert off >= WINDOW
                rows_b = kc_w
                newest = True
            else:
                rows_b = 0
                newest = False

            def scores(off=off, kc_w=kc_w):
                k_tile = kwn[pl.ds(pl.multiple_of(t0 + off, LANES), kc_w), :]
                return _tile_scores(k_tile, qwin_buf[idx])

            def finish(s, off=off, kc_w=kc_w, rows_b=rows_b, newest=newest):
                bias = None
                if rows_b:
                    r = off + lax.broadcasted_iota(jnp.int32, (rows_b, Q_BLOCK), 0)
                    c = lax.broadcasted_iota(jnp.int32, (rows_b, Q_BLOCK), 1)
                    bias = jnp.where((r - WINDOW <= c) if newest else (r > c), 0.0, NEG_BIG)
                vt_tile = jnp.concatenate([vwt[i + off // LANES + cc] for cc in range(kc_w // LANES)], axis=1)
                return _tile_softmax(s, bias, vt_tile, shifted)

            tiles.append((scores, finish))
            off += kc_w
        return tiles

    def lockstep(chains, between=None):
        chains = list(chains)
        while chains:
            for chain in list(chains):
                try:
                    next(chain)
                except StopIteration:
                    chains.remove(chain)
            if between is not None:
                between()

    def pair_step(cur, nxt):
        block_ids = pair_blocks(pair)
        qsel_buf, qwin_buf, oc_buf = cur
        all_tiles = window_tiles(0, pair, qwin_buf) + window_tiles(1, pair, qwin_buf)
        n_win = len(all_tiles) // 2
        pending = []
        results = []

        def pump():
            if len(results) == len(all_tiles):
                return
            while len(results) + len(pending) < min(len(all_tiles), len(results) + SCORE_LOOKAHEAD + 1):
                pending.append(all_tiles[len(results) + len(pending)][0]())
            results.append(all_tiles[len(results)][1](pending.pop(0)))

        n_slots = _sel_slot_count(nqb)
        n0 = lax.shift_right_logical(block_ids[0] * Q_BLOCK + Q_BLOCK + SEL_CHUNK - 1, SEL_CHUNK.bit_length() - 1)
        n0_max = (nqb // 2 * Q_BLOCK + SEL_CHUNK - 1) // SEL_CHUNK
        key_minus_query = (lax.broadcasted_iota(jnp.int32, (SEL_CHUNK, Q_BLOCK), 0)
                           - lax.broadcasted_iota(jnp.int32, (SEL_CHUNK, Q_BLOCK), 1))
        sel_tiles = []
        for slot in range(n_slots):
            last_of_block = slot == 0 or slot == n_slots - 1
            if slot == 0:
                which, kb = 0, n0 - 1
            elif slot == n_slots - 1:
                which, kb = 1, n_slots - n0 - 1
            elif slot >= n0_max:
                which, kb = 1, slot - n0
            else:
                which = (slot >= n0).astype(jnp.int32)
                kb = slot - 1 + which * (1 - n0)
            k0 = pl.multiple_of(kb * SEL_CHUNK, SEL_CHUNK)

            def scores(which=which, k0=k0):
                return _tile_scores(ksn[pl.ds(k0, SEL_CHUNK), :], qsel_buf[which])

            def finish(s, which=which, kb=kb, k0=k0, last_of_block=last_of_block):
                bias = None
                if last_of_block:
                    t0 = block_ids[which] * Q_BLOCK
                    bias = jnp.where(key_minus_query <= t0 - k0, 0.0, NEG_BIG)
                vt_tile = jnp.concatenate([vst[kb * chunk_tiles + c] for c in range(chunk_tiles)], axis=1)
                return _tile_softmax(s, bias, vt_tile, shifted)

            sel_tiles.append((scores, finish))

        all_tiles += sel_tiles

        next_pair = jnp.minimum(pair + 1, n_pairs - 1)
        rounds = [0]

        def tile_between_rounds():
            rounds[0] += 1
            if rounds[0] % PROLOGUE_ROUNDS_PER_TILE == 0:
                pump()

        lockstep((block_prologue(idx, next_pair, nxt) for idx in range(2)), tile_between_rounds)
        while len(results) < len(all_tiles):
            pump()
        for slot in range(n_slots):
            m, l, o = results[2 * n_win + slot]
            if not shifted:
                slot_m[slot] = m
            slot_l[slot] = l
            slot_o[slot] = o

        for idx in range(2):
            o_c = oc_buf[idx]
            o_w = _merge_tiles([part + (None,) for part in results[idx * n_win:(idx + 1) * n_win]])
            slots = range(0, n0_max) if idx == 0 else range(1, n_slots)
            parts = []
            for slot in slots:
                if slot == 0 or slot >= n0_max:
                    mine = None
                else:
                    mine = (slot < n0) if idx == 0 else (slot >= n0)
                parts.append((None if shifted else slot_m[slot], slot_l[slot], slot_o[slot], mine))
            o_s = _merge_tiles(parts)
            row0 = pl.multiple_of(block_ids[idx] * Q_BLOCK, Q_BLOCK)
            gate_t = gate_ref[0, 0, pl.ds(row0, Q_BLOCK), :].T
            for hh in range(NSA_HPG // 2):
                tiles = []
                for a in range(2):
                    n = 2 * hh + a
                    lanes = slice(n * Q_BLOCK, (n + 1) * Q_BLOCK)
                    c0 = N_BRANCH * n
                    tiles.append(gate_t[c0:c0 + 1, :] * o_c[:, lanes]
                                 + gate_t[c0 + 1:c0 + 2, :] * o_s[:, lanes]
                                 + gate_t[c0 + 2:c0 + 3, :] * o_w[:, lanes])
                o_ref[0, pl.ds(row0, Q_BLOCK), hh * LANES:(hh + 1) * LANES] = jnp.concatenate(tiles, axis=0).T

    @pl.when(pair == 0)
    def _first_prologue():
        lockstep(block_prologue(idx, pair, bufs(0)) for idx in range(2))

    pair_step(bufs(pair % 2), bufs(1 - pair % 2))


def _sel_map_t(seq):
    n_c = seq // CMP_STRIDE - 1
    n_s = seq // SEL_BLOCK
    tok = np.arange(seq)
    start = np.arange(n_c) * CMP_STRIDE
    cover_c = (tok[None, :] >= start[:, None]) & (tok[None, :] < start[:, None] + CMP_BLOCK)
    cover_s = (tok[:, None] // SEL_BLOCK) == np.arange(n_s)[None, :]
    m = cover_c.astype(np.float32) @ cover_s.astype(np.float32) / np.float32(CMP_BLOCK)
    out = np.zeros((n_s, n_c + 1), np.float32)
    out[:, :n_c] = m.T
    return jnp.asarray(out)


def _score_bounds(q_gain, k_gain):
    bound = 1.01 * HEAD_DIM ** 0.5 * LOG2_E * jnp.max(jnp.abs(q_gain)) * jnp.max(jnp.abs(k_gain), axis=-1)
    return jnp.broadcast_to(bound[:, None], (N_BRANCH, LANES)).astype(F32)


def _attention(q, gates, kvc, kv, qg_t, kg_ext, shift, shifted):
    b, s, _ = q.shape
    g = NSA_GROUPS
    nqb = s // Q_BLOCK
    ncmp = s // CMP_STRIDE
    nsel = s // SEL_BLOCK
    assert s % SEL_CHUNK == 0 and nsel % SUBLANES == 0 and nqb % 2 == 0
    gw = NSA_HPG * HEAD_DIM
    cols = NSA_HPG * Q_BLOCK
    n_slots = _sel_slot_count(nqb)

    def kv_spec(j):
        return pl.BlockSpec((1, 1, 1, s, LANES), lambda bi, gi, i: (j, bi, gi, 0, 0))

    return pl.pallas_call(
        functools.partial(_attn_kernel, shifted=shifted),
        grid=(b, g, nqb // 2),
        in_specs=[
            pl.BlockSpec((1, s, gw), lambda bi, gi, i: (bi, 0, gi)),
            pl.BlockSpec((1, 1, s, LANES), lambda bi, gi, i: (bi, gi, 0, 0)),
            pl.BlockSpec((1, 1, ncmp, LANES), lambda bi, gi, i: (bi, gi, 0, 0)),
            kv_spec(1), kv_spec(2),
            pl.BlockSpec((nsel, ncmp), lambda bi, gi, i: (0, 0)),
            pl.BlockSpec((HEAD_DIM, Q_BLOCK), lambda bi, gi, i: (0, 0)),
            pl.BlockSpec((N_BRANCH, LANES), lambda bi, gi, i: (0, 0)),
            pl.BlockSpec((N_BRANCH, LANES), lambda bi, gi, i: (0, 0)),
        ],
        out_specs=pl.BlockSpec((1, s, gw), lambda bi, gi, i: (bi, 0, gi)),
        out_shape=jax.ShapeDtypeStruct((b, s, Q_DIM), F32),
        scratch_shapes=[
            pltpu.VMEM((s, LANES), BF16),
            pltpu.VMEM((WINDOW + s, LANES), BF16),
            pltpu.VMEM((s // LANES, VT_ROWS, LANES), BF16),
            pltpu.VMEM(((WINDOW + s) // LANES, VT_ROWS, LANES), BF16),
            pltpu.VMEM((2, 2, LANES, cols), BF16), pltpu.VMEM((2, 2, LANES, cols), BF16),
            pltpu.VMEM((2, 2, HEAD_DIM, cols), F32),
            pltpu.VMEM((n_slots, 1, cols), F32),
            pltpu.VMEM((n_slots, 1, cols), F32),
            pltpu.VMEM((n_slots, HEAD_DIM, cols), F32),
        ],
        compiler_params=_cparams(3),
        name="nsa_attention",
    )(q, gates, kvc, kv, kv, _sel_map_t(s), qg_t, kg_ext, shift)


def _ffn_core(x, g_ref, sc_ref, sh_ref, gate_ref, wg_ref, wu_ref, wo_ref):
    hh = _norm_mod(x, g_ref[...], sc_ref[0], sh_ref[0]).astype(BF16)
    a = jnp.dot(hh, wg_ref[...], preferred_element_type=F32)
    u = jnp.dot(hh, wu_ref[...], preferred_element_type=F32)
    act = (jax.nn.silu(a) * u).astype(BF16)
    return x + gate_ref[0] * jnp.dot(act, wo_ref[...], preferred_element_type=F32)


def _ffn_kernel(x_ref, g_ref, sc_ref, sh_ref, gate_ref, wg_ref, wu_ref, wo_ref, o_ref):
    o_ref[0] = _ffn_core(x_ref[0], g_ref, sc_ref, sh_ref, gate_ref, wg_ref, wu_ref, wo_ref)


def _mix_ffn_kernel(y_ref, wmix_ref, mix_gate_ref, x_ref, g_ref, sc_ref, sh_ref, gate_ref, wg_ref, wu_ref, wo_ref,
                    o_ref):
    mix = jnp.dot(y_ref[0].astype(BF16), wmix_ref[...], preferred_element_type=F32)
    x = x_ref[0] + mix_gate_ref[0] * mix
    o_ref[0] = _ffn_core(x, g_ref, sc_ref, sh_ref, gate_ref, wg_ref, wu_ref, wo_ref)


def _ffn(x, gain, sc, sh, gate, w_in, w_out, layer, mixer_out=None):
    b, s, d = x.shape
    hidden = w_out.shape[1]
    tm = min(ROW_TILE, s)
    vec = pl.BlockSpec((1, 1, d), lambda bi, i: (bi, 0, 0))
    rows = lambda width: pl.BlockSpec((1, tm, width), lambda bi, i: (bi, i, 0))
    in_specs = [
        rows(d), _resident((1, d)), vec, vec, vec,
        _layer_plane((d, hidden), layer, 0), _layer_plane((d, hidden), layer, 1),
        _layer_plane((hidden, d), layer),
    ]
    args = (x, gain, sc, sh, gate, w_in, w_in, w_out)
    body = _ffn_kernel
    if mixer_out is not None:
        y, w_mix, j, mix_gate = mixer_out
        in_specs = [rows(y.shape[-1]), _layer_plane(w_mix.shape[1:], j), vec] + in_specs
        args = (y, w_mix, mix_gate) + args
        body = _mix_ffn_kernel
    return pl.pallas_call(
        body,
        grid=(b, s // tm),
        in_specs=in_specs,
        out_specs=rows(d),
        out_shape=jax.ShapeDtypeStruct((b, s, d), F32),
        compiler_params=_cparams(2),
        name="ffn",
    )(*args)


def _band_starts(width, block_w):
    band = 2 * MXU_WIDTH
    starts = []
    for c0 in range(0, width, MXU_WIDTH):
        c1 = min(c0 + MXU_WIDTH, width)
        lo = (c0 // block_w) * block_w
        hi = ((c1 - 1) // block_w + 1) * block_w
        k0 = min((lo // LANES) * LANES, width - band)
        assert k0 >= 0 and k0 <= lo and hi <= k0 + band
        starts.append(k0)
    return starts


def _band_weights(w_a, w_x):
    nb, bw, _ = w_a.shape
    width = nb * bw
    spread = jnp.asarray(np.tile(np.eye(bw, dtype=np.float32), (1, nb)), dtype=BF16)
    blk = np.arange(width) // bw
    on_diag = jnp.asarray(blk[:, None] == blk[None, :])
    dense = lambda w: jnp.where(
        on_diag, jnp.dot(w.reshape(width, bw).astype(BF16), spread, preferred_element_type=BF16), 0)
    da, dx = dense(w_a), dense(w_x)
    band = 2 * MXU_WIDTH
    tiles = []
    for t, k0 in enumerate(_band_starts(width, bw)):
        c0 = t * MXU_WIDTH
        c1 = min(c0 + MXU_WIDTH, width)
        pad = ((0, 0), (0, MXU_WIDTH - (c1 - c0)))
        tiles.append(jnp.concatenate(
            [jnp.pad(da[k0:k0 + band, c0:c1], pad), jnp.pad(dx[k0:k0 + band, c0:c1], pad)], axis=1))
    return 0.5 * jnp.stack(tiles)


def _rg_mixer_kernel(x_ref, xn_ref, g_ref, sc_ref, sh_ref, res_gate_ref, win_ref, cw_ref, cb_ref, wb_ref, ba_ref,
                     bx_ref, lam_ref, wout_ref, o_ref,
                     u_a, u_b, gate_a, gate_b, h_scr, a_scr, b_scr, y_scr, *, band_starts):
    step = pl.program_id(0)
    nb, ts, d = x_ref.shape
    width = wout_ref.shape[0]
    rows = nb * ts
    halo = CONV_WIDTH - 1
    band = 2 * MXU_WIDTH

    pieces = [(c0, min(c0 + MXU_WIDTH, 2 * width)) for c0 in range(0, 2 * width, MXU_WIDTH)]

    def project_piece(hh, piece, u_dst, gate_dst):
        c0, c1 = pieces[piece]
        res = jnp.dot(hh, win_ref[:, c0:c1], preferred_element_type=F32)
        n_gate = max(0, min(c1, width) - c0)
        if n_gate:
            gate_dst[:, c0:c0 + n_gate] = res[:, :n_gate]
        if c1 > width:
            u0 = max(c0, width) - width
            u_dst[:, :, u0:c1 - width] = jnp.swapaxes(res[:, n_gate:].reshape(nb, ts, c1 - c0 - n_gate), 0, 1)

    def normed(tile_ref):
        return _norm_mod(tile_ref[...], g_ref[...], sc_ref[...], sh_ref[...]).reshape(rows, d).astype(BF16)

    @pl.when(step == 0)
    def _init():
        u_b[...] = jnp.zeros((ts, nb, width), F32)
        h_scr[...] = jnp.zeros(h_scr.shape, F32)
        hh = normed(x_ref)
        for piece in range(len(pieces)):
            project_piece(hh, piece, u_a, gate_a)

    def tile_step(u_cur, gate_cur, u_nxt, gate_nxt):
        prev_tail = u_nxt[ts - halo:ts]
        hh_next = normed(xn_ref)
        todo = list(range(len(pieces)))
        per_stage = -(-len(pieces) // len(band_starts))

        uc = jnp.zeros((ts, nb, width), F32) + cb_ref[...]
        for k in range(CONV_WIDTH):
            shifted = u_cur[...] if k == halo else jnp.concatenate([prev_tail[k:], u_cur[0:ts - halo + k]], axis=0)
            uc = uc + cw_ref[k:k + 1, :] * shifted
        uc = uc.reshape(rows, width)
        ub = uc.astype(BF16)

        z = -lam_ref[...]
        log_a_unit = -RG_C * (jnp.maximum(z, 0.0) + jnp.log1p(jnp.exp(-jnp.abs(z))))

        for t, k0 in enumerate(band_starts):
            c0 = t * MXU_WIDTH
            c1 = min(c0 + MXU_WIDTH, width)
            n = c1 - c0
            zz = jnp.dot(ub[:, k0:k0 + band], wb_ref[t], preferred_element_type=F32)
            for piece in todo[:per_stage]:
                project_piece(hh_next, piece, u_nxt, gate_nxt)
            todo = todo[per_stage:]
            half_unit = 0.5 * log_a_unit[:, c0:c1]
            half_u = 0.5 * uc[:, c0:c1]
            log_a = half_unit * jnp.tanh(zz[:, :n] + ba_ref[:, c0:c1]) + half_unit
            gated_u = half_u * jnp.tanh(zz[:, MXU_WIDTH:MXU_WIDTH + n] + bx_ref[:, c0:c1]) + half_u
            a = jnp.exp(log_a)
            bb = jnp.sqrt(-jnp.tanh(log_a) * (a * a + 1.0)) * gated_u
            for ct in range(n // LANES):
                lanes = slice(ct * LANES, (ct + 1) * LANES)
                a_scr[c0 // LANES + ct] = a[:, lanes]
                b_scr[c0 // LANES + ct] = bb[:, lanes]

        n_ct = width // LANES

        def scan_step(t, h):
            rows_t = pl.ds(pl.multiple_of(t * nb, nb), nb)
            new = []
            for ct in range(n_ct):
                hc = a_scr[ct, rows_t, :] * h[ct] + b_scr[ct, rows_t, :]
                b_scr[ct, rows_t, :] = hc
                new.append(hc)
            return tuple(new)

        h_fin = lax.fori_loop(0, ts, scan_step, tuple(h_scr[ct] for ct in range(n_ct)), unroll=SUBLANES)
        for ct in range(n_ct):
            lanes = slice(ct * LANES, (ct + 1) * LANES)
            h_scr[ct] = h_fin[ct]
            hs = jnp.swapaxes(b_scr[ct].reshape(ts, nb, LANES), 0, 1)
            y_scr[:, lanes] = (hs.reshape(rows, LANES) * jax.nn.gelu(gate_cur[:, lanes])).astype(BF16)
        mix = jnp.dot(y_scr[...], wout_ref[...], preferred_element_type=F32)
        o_ref[...] = x_ref[...] + res_gate_ref[...] * mix.reshape(nb, ts, d)

    @pl.when(step % 2 == 0)
    def _even():
        tile_step(u_a, gate_a, u_b, gate_b)

    @pl.when(step % 2 == 1)
    def _odd():
        tile_step(u_b, gate_b, u_a, gate_a)


def _rg_mixer(x, gain, sc, sh, res_gate, w_in, conv_w, conv_b, w_band, b_a, b_x, lam, w_out, j, block_w):
    b, s, d = x.shape
    width = w_out.shape[1]
    assert b == SUBLANES
    ts = min(RG_TIME_TILE, s)
    rows = b * ts
    band_starts = tuple(_band_starts(width, block_w))
    n_tiles = s // ts
    tile = pl.BlockSpec((b, ts, d), lambda t: (0, t, 0))
    next_tile = pl.BlockSpec((b, ts, d), lambda t: (0, jnp.minimum(t + 1, n_tiles - 1), 0))
    mod = _resident((b, 1, d))
    vec = _resident((1, width))
    kern = functools.partial(_rg_mixer_kernel, band_starts=band_starts)
    return pl.pallas_call(
        kern,
        grid=(n_tiles,),
        in_specs=[
            tile, next_tile, _resident((1, d)), mod, mod, mod,
            _layer_plane(w_in.shape[1:], j),
            _resident((CONV_WIDTH, width)), vec,
            _resident(w_band.shape),
            vec, vec, vec,
            _layer_plane(w_out.shape[1:], j),
        ],
        out_specs=tile,
        out_shape=jax.ShapeDtypeStruct((b, s, d), F32),
        scratch_shapes=[
            pltpu.VMEM((ts, b, width), F32),
            pltpu.VMEM((ts, b, width), F32),
            pltpu.VMEM((rows, width), F32),
            pltpu.VMEM((rows, width), F32),
            pltpu.VMEM((width // LANES, b, LANES), F32),
            pltpu.VMEM((width // LANES, rows, LANES), F32),
            pltpu.VMEM((width // LANES, rows, LANES), F32),
            pltpu.VMEM((rows, width), BF16),
        ],
        compiler_params=_cparams(1),
        name="rg_mixer",
    )(x, x, gain, sc, sh, res_gate, w_in, conv_w, conv_b, w_band, b_a, b_x, lam, w_out)


def kernel(x, c, ada_w, ada_b, norm1_g, norm2_g, nsa_w_in, nsa_w_out, nsa_cmp_pos, nsa_cmp_w1, nsa_cmp_w2, nsa_q_gain, nsa_k_gain, rg_w_in, rg_conv_w, rg_conv_b, rg_w_a, rg_b_a, rg_w_x, rg_b_x, rg_lam, rg_w_out, ffn_w_in, ffn_w_out):
    depth, d, _ = ada_w.shape
    b = x.shape[0]
    mod = _ada_mod(c, ada_w, ada_b).reshape(depth, b, 6, 1, d)
    ffn_w_in, ffn_w_out, nsa_w_out, rg_w_in, rg_w_out = (
        w.astype(BF16) for w in (ffn_w_in, ffn_w_out, nsa_w_out, rg_w_in, rg_w_out))
    for layer in range(depth):
        sh1, sc1, g1, sh2, sc2, g2 = (mod[layer, :, k] for k in range(6))
        n1 = norm1_g[layer].reshape(1, d)
        n2 = norm2_g[layer].reshape(1, d)
        j = layer // N_MIXERS
        if layer % N_MIXERS == 0:
            kg_ext = jnp.concatenate([nsa_k_gain[j], jnp.zeros_like(nsa_k_gain[j])], axis=1)
            qg_t = jnp.broadcast_to(nsa_q_gain[j][:, None], (HEAD_DIM, Q_BLOCK))
            q, kv, gates = _nsa_proj(x, n1, sc1, sh1, _nsa_weight_layout(nsa_w_in[j]))
            kvc = _compress(kv, *_compress_weights(nsa_cmp_pos[j], nsa_cmp_w1[j], nsa_cmp_w2[j]), kg_ext)
            bounds = _score_bounds(nsa_q_gain[j], nsa_k_gain[j])
            o = lax.cond(
                2.0 * jnp.max(bounds) <= MAX_SHIFTED_SCORE_RANGE,
                lambda *a: _attention(*a, bounds, True),
                lambda *a: _attention(*a, jnp.zeros_like(bounds), False),
                q, gates, kvc, kv, qg_t, kg_ext)
            mixer_out = (o, nsa_w_out, j, g1)
        else:
            width = rg_w_out.shape[1]
            block_w = rg_w_a.shape[-1]
            x = _rg_mixer(x, n1, sc1, sh1, g1, rg_w_in,
                          rg_conv_w[j], rg_conv_b[j].reshape(1, width),
                          _band_weights(rg_w_a[j], rg_w_x[j]),
                          0.5 * rg_b_a[j].reshape(1, width), 0.5 * rg_b_x[j].reshape(1, width),
                          rg_lam[j].reshape(1, width), rg_w_out, j, block_w)
            mixer_out = None
        x = _ffn(x, n2, sc2, sh2, g2, ffn_w_in, ffn_w_out, layer, mixer_out)
    return x
```

```python
import functools

import numpy as np
import jax
import jax.numpy as jnp
from jax import lax
from jax.experimental import pallas as pl
from jax.experimental.pallas import tpu as pltpu

F32 = jnp.float32
BF16 = jnp.bfloat16

EPS = 1e-6
N_MIXERS = 2

NSA_HEADS = 16
NSA_GROUPS = 4
NSA_HPG = NSA_HEADS // NSA_GROUPS
HEAD_DIM = 64
CMP_STRIDE = 16
CMP_BLOCK = 2 * CMP_STRIDE
SEL_BLOCK = 64
SEL_TOP_N = 8
WINDOW = 512
Q_BLOCK = 128
N_BRANCH = 3
Q_DIM = NSA_HEADS * HEAD_DIM
KV_DIM = NSA_GROUPS * HEAD_DIM
GATE_COLS = N_BRANCH * NSA_HEADS
GATE_PER_GROUP = N_BRANCH * NSA_HPG

CONV_WIDTH = 4
RG_C = 8.0

LANES = 128
SUBLANES = 8
MXU_WIDTH = 256
VMEM_LIMIT_BYTES = 56 * 1024 * 1024

NEG_BIG = -1e30
LOG2_E = 1.4426950408889634
BLOCK_LANE0 = HEAD_DIM
PAD_LANE = LANES - 1
SHIFT_LANE = LANES - 2
MAX_SHIFTED_SCORE_RANGE = 100.0
VT_ROWS = HEAD_DIM + 16
SEL_CHUNK = 256
SCORE_LOOKAHEAD = 3
RG_TIME_TILE = 64
ROW_TILE = 512


def _cparams(n_axes):
    return pltpu.CompilerParams(
        dimension_semantics=("arbitrary",) * n_axes,
        vmem_limit_bytes=VMEM_LIMIT_BYTES,
    )


def _resident(shape):
    return pl.BlockSpec(shape, lambda *_: (0,) * len(shape), pipeline_mode=pl.Buffered(1))


def _layer_plane(shape, layer, col_block=0):
    return pl.BlockSpec((None,) + tuple(shape), lambda *_: (layer, 0, col_block), pipeline_mode=pl.Buffered(1))


def _norm_mod(x, gain, scale, shift):
    ms = jnp.mean(x * x, axis=-1, keepdims=True)
    y = x * lax.rsqrt(ms + EPS) * gain
    return y * (1.0 + scale) + shift


def _ada_kernel(c_ref, w_ref, b_ref, o_ref):
    cond = jax.nn.silu(c_ref[...])
    o_ref[0] = jnp.dot(cond, w_ref[0], preferred_element_type=F32,
                       precision=lax.Precision.HIGHEST) + b_ref[0]


def _ada_mod(c, ada_w, ada_b):
    depth, d, n = ada_w.shape
    b = c.shape[0]
    tn = n // 4
    return pl.pallas_call(
        _ada_kernel,
        grid=(depth, n // tn),
        in_specs=[
            pl.BlockSpec((b, d), lambda l, j: (0, 0)),
            pl.BlockSpec((1, d, tn), lambda l, j: (l, 0, j)),
            pl.BlockSpec((1, 1, tn), lambda l, j: (l, 0, j)),
        ],
        out_specs=pl.BlockSpec((1, b, tn), lambda l, j: (l, 0, j)),
        out_shape=jax.ShapeDtypeStruct((depth, b, n), F32),
        compiler_params=_cparams(2),
        name="ada_mod",
    )(c, ada_w, ada_b.reshape(depth, 1, n))


def _nsa_weight_layout(w_in):
    w_in = w_in.astype(BF16)
    src = np.arange(6 * KV_DIM).reshape(N_BRANCH, 2, NSA_GROUPS, HEAD_DIM).transpose(0, 2, 1, 3).reshape(-1)
    perm = np.zeros((6 * KV_DIM, 6 * KV_DIM), np.float32)
    perm[src, np.arange(6 * KV_DIM)] = 1.0
    kv = jnp.dot(w_in[:, Q_DIM:Q_DIM + 6 * KV_DIM], jnp.asarray(perm, dtype=BF16), preferred_element_type=BF16)
    gates = jnp.pad(w_in[:, Q_DIM + 6 * KV_DIM:], ((0, 0), (0, LANES - GATE_COLS)))
    return jnp.concatenate([w_in[:, :Q_DIM], kv, gates], axis=1)


def _nsa_proj_kernel(x_ref, g_ref, sc_ref, sh_ref, w_ref, q_ref, kv_ref, gate_ref):
    hh = _norm_mod(x_ref[0], g_ref[...], sc_ref[0], sh_ref[0])
    p = jnp.dot(hh.astype(BF16), w_ref[...], preferred_element_type=F32)
    q_ref[0] = p[:, :Q_DIM]
    for j in range(N_BRANCH):
        for g in range(NSA_GROUPS):
            lo = Q_DIM + (j * NSA_GROUPS + g) * LANES
            kv_ref[j, 0, g] = p[:, lo:lo + LANES]
    gate = jax.nn.sigmoid(p[:, Q_DIM + 6 * KV_DIM:])
    for g in range(NSA_GROUPS):
        shift = (LANES - GATE_PER_GROUP * g) % LANES
        gate_ref[0, g] = gate if shift == 0 else pltpu.roll(gate, shift, 1)


def _nsa_proj(x, gain, sc, sh, w):
    b, s, d = x.shape
    tm = min(ROW_TILE, s)
    n = w.shape[1]
    vec = pl.BlockSpec((1, 1, d), lambda bi, i: (bi, 0, 0))
    return pl.pallas_call(
        _nsa_proj_kernel,
        grid=(b, s // tm),
        in_specs=[
            pl.BlockSpec((1, tm, d), lambda bi, i: (bi, i, 0)),
            pl.BlockSpec((1, d), lambda bi, i: (0, 0)),
            vec, vec,
            pl.BlockSpec((d, n), lambda bi, i: (0, 0)),
        ],
        out_specs=[
            pl.BlockSpec((1, tm, Q_DIM), lambda bi, i: (bi, i, 0)),
            pl.BlockSpec((N_BRANCH, 1, NSA_GROUPS, tm, LANES), lambda bi, i: (0, bi, 0, i, 0)),
            pl.BlockSpec((1, NSA_GROUPS, tm, LANES), lambda bi, i: (bi, 0, i, 0)),
        ],
        out_shape=[
            jax.ShapeDtypeStruct((b, s, Q_DIM), F32),
            jax.ShapeDtypeStruct((N_BRANCH, b, NSA_GROUPS, s, LANES), F32),
            jax.ShapeDtypeStruct((b, NSA_GROUPS, s, LANES), F32),
        ],
        compiler_params=_cparams(2),
        name="nsa_proj",
    )(x, gain, sc, sh, w)


def _k_lane_norm(t, gain_ext):
    lane = lax.broadcasted_iota(jnp.int32, t.shape, 1)
    ms = jnp.sum(jnp.where(lane < HEAD_DIM, t * t, 0.0), axis=-1, keepdims=True) * (1.0 / HEAD_DIM)
    return t * lax.rsqrt(ms + EPS) * gain_ext


def _compress_weights(pos, w1, w2):
    hidden = w1.shape[-1]
    w1p = w1.reshape(2, 2, CMP_STRIDE, HEAD_DIM, hidden)
    z = jnp.zeros_like(w1p[0])
    wk = jnp.concatenate([w1p[0], z], axis=-1)
    wv = jnp.concatenate([z, w1p[1]], axis=-1)
    w1_blk = jnp.concatenate([wk, wv], axis=2).astype(BF16)
    w1_blk = w1_blk.reshape(2, CMP_STRIDE * LANES, 2 * hidden)
    pos_blk = jnp.concatenate([pos[0], pos[1]], axis=-1).reshape(2, CMP_STRIDE, 1, LANES)
    z2 = jnp.zeros_like(w2[0])
    w2_blk = jnp.concatenate([jnp.concatenate([w2[0], z2], axis=1),
                              jnp.concatenate([z2, w2[1]], axis=1)], axis=0).astype(BF16)
    return pos_blk, w1_blk, w2_blk


def _compress_kernel(kv_ref, pos_ref, w1_ref, w2_ref, kg_ref, o_ref):
    g, seq = kv_ref.shape[2:4]
    nch = seq // CMP_STRIDE
    rows = g * nch
    toks = [jnp.concatenate([kv_ref[0, 0, gi, pl.ds(p, nch, stride=CMP_STRIDE), :] for gi in range(g)], axis=0)
            for p in range(CMP_STRIDE)]
    halves = []
    for half in range(2):
        chunk = jnp.concatenate([(toks[p] + pos_ref[half, p]).astype(BF16) for p in range(CMP_STRIDE)], axis=1)
        halves.append(jnp.dot(chunk, w1_ref[half], preferred_element_type=F32))
    first, second = halves
    h1 = first + pltpu.roll(second, rows - 1, 0)
    out = jnp.dot(jax.nn.silu(h1).astype(BF16), w2_ref[...], preferred_element_type=F32)
    lane = lax.broadcasted_iota(jnp.int32, out.shape, 1)
    out = jnp.where(lane < HEAD_DIM, _k_lane_norm(out, kg_ref[0:1]), out)
    r = lax.broadcasted_iota(jnp.int32, out.shape, 0)
    out = jnp.where((r & (nch - 1)) == nch - 1, 0.0, out)
    o_ref[0] = out.reshape(g, nch, LANES)


def _compress(kv, pos_blk, w1_blk, w2_blk, kg_ext):
    _, b, g, s, _ = kv.shape
    nch = s // CMP_STRIDE
    assert nch & (nch - 1) == 0
    return pl.pallas_call(
        _compress_kernel,
        grid=(b,),
        in_specs=[
            pl.BlockSpec((1, 1, g, s, LANES), lambda bi: (0, bi, 0, 0, 0)),
            pl.BlockSpec(pos_blk.shape, lambda bi: (0, 0, 0, 0)),
            pl.BlockSpec(w1_blk.shape, lambda bi: (0, 0, 0)),
            pl.BlockSpec(w2_blk.shape, lambda bi: (0, 0)),
            pl.BlockSpec((N_BRANCH, LANES), lambda bi: (0, 0)),
        ],
        out_specs=pl.BlockSpec((1, g, nch, LANES), lambda bi: (bi, 0, 0, 0)),
        out_shape=jax.ShapeDtypeStruct((b, g, nch, LANES), F32),
        compiler_params=_cparams(1),
        name="nsa_compress",
    )(kv, pos_blk, w1_blk, w2_blk, kg_ext)


def _sel_slot_count(nqb):
    chunks = lambda i: -(-(i + 1) * Q_BLOCK // SEL_CHUNK)
    counts = {chunks(p) + chunks(nqb - 1 - p) for p in range(nqb // 2)}
    assert len(counts) == 1, counts
    return counts.pop()


def _tile_scores(k_tile, q_t):
    return jnp.concatenate([jnp.dot(k_tile, q_t[:, h:h + MXU_WIDTH], preferred_element_type=F32)
                            for h in range(0, q_t.shape[1], MXU_WIDTH)], axis=1)


def _tile_softmax(s, bias, vt_tile, shifted):
    if bias is not None:
        rows = bias.shape[0]
        biased = s[:rows] + jnp.concatenate([bias] * (s.shape[1] // Q_BLOCK), axis=1)
        s = biased if rows == s.shape[0] else jnp.concatenate([biased, s[rows:]], axis=0)
    m = None if shifted else jnp.max(s, axis=0, keepdims=True)
    p = jnp.exp2(s if shifted else s - m)
    ol = jnp.dot(vt_tile, p.astype(BF16), preferred_element_type=F32)
    return m, ol[HEAD_DIM:HEAD_DIM + 1], ol[:HEAD_DIM]


def _merge_tiles(parts):
    if parts[0][0] is None:
        weights = [None if mine is None else jnp.where(mine, 1.0, 0.0) for _, _, _, mine in parts]
    else:
        ms = [mi if mine is None else jnp.where(mine, mi, NEG_BIG) for mi, _, _, mine in parts]
        m = functools.reduce(jnp.maximum, ms)
        weights = [jnp.exp2(mi - m) for mi in ms]
    l = None
    o = None
    for w, (_, li, oi, _) in zip(weights, parts):
        if w is not None:
            li, oi = w * li, w * oi
        l = li if l is None else l + li
        o = oi if o is None else o + oi
    return o / l


def _attn_kernel(q_ref, gate_ref, kvc_ref, ks_ref, kw_ref, selmap_ref, qg_ref, kg_ref, shift_ref, o_ref,
                 ksn, kwn, vst, vwt, qt_scr, slot_m, slot_l, slot_o, *, shifted):
    pair = pl.program_id(2)
    seq = ks_ref.shape[3]
    ncmp = kvc_ref.shape[2]
    nsel = selmap_ref.shape[0]
    nqb = seq // Q_BLOCK
    cols = NSA_HPG * Q_BLOCK
    win_tiles = WINDOW // LANES
    chunk_tiles = SEL_CHUNK // LANES
    scale = HEAD_DIM ** -0.5 * LOG2_E

    @pl.when(pair == 0)
    def _prepare_kv():
        lane = lax.broadcasted_iota(jnp.int32, (LANES, LANES), 1)
        row = lax.broadcasted_iota(jnp.int32, (LANES, LANES), 0)
        vrow = lax.broadcasted_iota(jnp.int32, (VT_ROWS, LANES), 0)
        vlane = lax.broadcasted_iota(jnp.int32, (VT_ROWS, LANES), 1)
        pick_v = ((vlane == vrow + HEAD_DIM) & (vrow < HEAD_DIM)).astype(BF16)
        ones_row = (vrow == HEAD_DIM).astype(F32)
        v_plane = lambda tile: (lax.dot_general(pick_v, tile.astype(BF16), (((1,), (1,)), ((), ())),
                                                preferred_element_type=F32) + ones_row).astype(BF16)
        pad_mark = (lane == PAD_LANE).astype(BF16)
        for w in range(win_tiles):
            kwn[w * LANES:(w + 1) * LANES, :] = pad_mark
        vwt[0:win_tiles] = jnp.zeros((win_tiles, VT_ROWS, LANES), BF16)

        def body(c, carry):
            r0 = pl.multiple_of(c * LANES, LANES)
            ts = ks_ref[0, 0, 0, pl.ds(r0, LANES), :]
            in_block = (lane - BLOCK_LANE0 == lax.shift_right_logical(r0 + row, SEL_BLOCK.bit_length() - 1))
            marks = (in_block | (lane == SHIFT_LANE)).astype(F32)
            ksn[pl.ds(r0, LANES), :] = (_k_lane_norm(ts, kg_ref[1:2]) + marks).astype(BF16)
            vst[c] = v_plane(ts)
            tw = kw_ref[0, 0, 0, pl.ds(r0, LANES), :]
            kwn[pl.ds(WINDOW + r0, LANES), :] = (
                _k_lane_norm(tw, kg_ref[2:3]) + (lane == SHIFT_LANE).astype(F32)).astype(BF16)
            vwt[win_tiles + c] = v_plane(tw)
            return carry

        lax.fori_loop(0, seq // LANES, body, 0, unroll=4)

    kc = kvc_ref[0, 0]
    kc_b = kc.astype(BF16)
    vc_t = kc.T[HEAD_DIM:, :].astype(BF16)
    block_ids = (pair, nqb - 1 - pair)

    def block_branches(idx):
        i = block_ids[idx]
        t0 = i * Q_BLOCK
        row0 = pl.multiple_of(t0, Q_BLOCK)

        q = q_ref[0, pl.ds(row0, Q_BLOCK), :]
        heads = []
        for h in range(NSA_HPG // 2):
            qt = q[:, h * LANES:(h + 1) * LANES].T
            for a in range(2):
                x = qt[a * HEAD_DIM:(a + 1) * HEAD_DIM]
                ms = jnp.mean(x * x, axis=0, keepdims=True)
                heads.append(x * lax.rsqrt(ms + EPS) * qg_ref[...] * scale)
        q_top = jnp.concatenate(heads, axis=1).astype(BF16)
        mask_rows = lax.broadcasted_iota(jnp.int32, (LANES - HEAD_DIM, cols), 0) + HEAD_DIM
        q_plain = jnp.concatenate([q_top, jnp.zeros((LANES - HEAD_DIM, cols), BF16)], axis=0)
        shift_rows = lambda br: jnp.where(
            mask_rows == SHIFT_LANE, -jnp.concatenate([shift_ref[br:br + 1, :]] * NSA_HPG, axis=1), 0.0)
        q_win = jnp.concatenate(
            [q_top, jnp.where(mask_rows == PAD_LANE, NEG_BIG, shift_rows(2)).astype(BF16)], axis=0)

        tiles = []
        off = 0
        span = WINDOW + Q_BLOCK
        while off < span:
            kc_w = min(SEL_CHUNK, span - off)
            if off < Q_BLOCK:
                rows_b = min(kc_w, Q_BLOCK - off)
                newest = False
            elif off + kc_w > WINDOW:
                assert off >= WINDOW
                rows_b = kc_w
                newest = True
            else:
                rows_b = 0
                newest = False

            def scores(off=off, kc_w=kc_w):
                k_tile = kwn[pl.ds(pl.multiple_of(t0 + off, LANES), kc_w), :]
                return _tile_scores(k_tile, q_win)

            def finish(s, off=off, kc_w=kc_w, rows_b=rows_b, newest=newest):
                bias = None
                if rows_b:
                    r = off + lax.broadcasted_iota(jnp.int32, (rows_b, Q_BLOCK), 0)
                    c = lax.broadcasted_iota(jnp.int32, (rows_b, Q_BLOCK), 1)
                    bias = jnp.where((r - WINDOW <= c) if newest else (r > c), 0.0, NEG_BIG)
                vt_tile = jnp.concatenate([vwt[i + off // LANES + cc] for cc in range(kc_w // LANES)], axis=1)
                return _tile_softmax(s, bias, vt_tile, shifted)

            tiles.append((scores, finish))
            off += kc_w
        yield tiles

        s = _tile_scores(kc_b, q_plain)
        yield
        tq = t0 + (lax.broadcasted_iota(jnp.int32, (ncmp, cols), 1) & (Q_BLOCK - 1))
        cmp_end = lax.broadcasted_iota(jnp.int32, (ncmp, cols), 0) * CMP_STRIDE + (CMP_BLOCK - 1)
        visible = cmp_end <= tq
        s = jnp.where(visible, s, NEG_BIG)
        m = jnp.max(s, axis=0, keepdims=True)
        p = jnp.where(visible, jnp.exp2(s - m), 0.0)
        l = jnp.sum(p, axis=0, keepdims=True)
        p_c = p / jnp.maximum(l, jnp.finfo(F32).tiny)
        o_c = jnp.dot(vc_t, p_c.astype(BF16), preferred_element_type=F32)

        yield
        p_sum = p_c[:, 0:Q_BLOCK]
        for n in range(1, NSA_HPG):
            p_sum = p_sum + p_c[:, n * Q_BLOCK:(n + 1) * Q_BLOCK]
        imp = jnp.dot(selmap_ref[...], p_sum, preferred_element_type=F32, precision=lax.Precision.HIGHEST)
        blk = lax.broadcasted_iota(jnp.int32, (nsel, Q_BLOCK), 0)
        cur = lax.shift_right_logical(t0 + lax.broadcasted_iota(jnp.int32, (nsel, Q_BLOCK), 1), 6)
        valid = blk <= cur
        forced = (blk == 0) | (blk == cur) | (blk == cur - 1)
        val = jnp.where(valid, imp, -jnp.inf)
        val = jnp.where(forced, jnp.inf, val)
        rank = jnp.zeros((nsel, Q_BLOCK), jnp.int32)
        for j2 in range(nsel):
            row = val[j2:j2 + 1, :]
            beats = (row > val) | ((row == val) & (blk > j2))
            rank = rank + beats.astype(jnp.int32)
            yield
        picked = jnp.where((rank < SEL_TOP_N) & valid, 0.0, NEG_BIG).astype(BF16)
        qt_scr[idx] = jnp.concatenate(
            [q_top, jnp.concatenate([picked] * NSA_HPG, axis=1),
             shift_rows(1)[nsel:].astype(BF16)], axis=0)
        gate_t = gate_ref[0, 0, pl.ds(row0, Q_BLOCK), :].T
        return o_c, gate_t

    all_tiles = []
    pending = []
    results = []

    def pump():
        while len(results) + len(pending) < min(len(all_tiles), len(results) + SCORE_LOOKAHEAD + 1):
            pending.append(all_tiles[len(results) + len(pending)][0]())
        results.append(all_tiles[len(results)][1](pending.pop(0)))

    per_block = [None, None]
    running = {idx: block_branches(idx) for idx in range(2)}
    for idx in range(2):
        all_tiles += next(running[idx])
    n_win = len(all_tiles) // 2
    while running:
        for idx, stages in list(running.items()):
            try:
                next(stages)
            except StopIteration as done:
                per_block[idx] = done.value
                del running[idx]

    n_slots = _sel_slot_count(nqb)
    n0 = lax.shift_right_logical(block_ids[0] * Q_BLOCK + Q_BLOCK + SEL_CHUNK - 1, SEL_CHUNK.bit_length() - 1)
    n0_max = (nqb // 2 * Q_BLOCK + SEL_CHUNK - 1) // SEL_CHUNK
    key_minus_query = (lax.broadcasted_iota(jnp.int32, (SEL_CHUNK, Q_BLOCK), 0)
                       - lax.broadcasted_iota(jnp.int32, (SEL_CHUNK, Q_BLOCK), 1))
    sel_tiles = []
    for slot in range(n_slots):
        last_of_block = slot == 0 or slot == n_slots - 1
        if slot == 0:
            which, kb = 0, n0 - 1
        elif slot == n_slots - 1:
            which, kb = 1, n_slots - n0 - 1
        elif slot >= n0_max:
            which, kb = 1, slot - n0
        else:
            which = (slot >= n0).astype(jnp.int32)
            kb = slot - 1 + which * (1 - n0)
        k0 = pl.multiple_of(kb * SEL_CHUNK, SEL_CHUNK)

        def scores(which=which, k0=k0):
            return _tile_scores(ksn[pl.ds(k0, SEL_CHUNK), :], qt_scr[which])

        def finish(s, which=which, kb=kb, k0=k0, last_of_block=last_of_block):
            bias = None
            if last_of_block:
                t0 = block_ids[which] * Q_BLOCK
                bias = jnp.where(key_minus_query <= t0 - k0, 0.0, NEG_BIG)
            vt_tile = jnp.concatenate([vst[kb * chunk_tiles + c] for c in range(chunk_tiles)], axis=1)
            return _tile_softmax(s, bias, vt_tile, shifted)

        sel_tiles.append((scores, finish))

    all_tiles += sel_tiles
    while len(results) < len(all_tiles):
        pump()
    for slot in range(n_slots):
        m, l, o = results[2 * n_win + slot]
        if not shifted:
            slot_m[slot] = m
        slot_l[slot] = l
        slot_o[slot] = o

    for idx in range(2):
        o_c, gate_t = per_block[idx]
        o_w = _merge_tiles([part + (None,) for part in results[idx * n_win:(idx + 1) * n_win]])
        slots = range(0, n0_max) if idx == 0 else range(1, n_slots)
        parts = []
        for slot in slots:
            if slot == 0 or slot >= n0_max:
                mine = None
            else:
                mine = (slot < n0) if idx == 0 else (slot >= n0)
            parts.append((None if shifted else slot_m[slot], slot_l[slot], slot_o[slot], mine))
        o_s = _merge_tiles(parts)
        row0 = pl.multiple_of(block_ids[idx] * Q_BLOCK, Q_BLOCK)
        for hh in range(NSA_HPG // 2):
            tiles = []
            for a in range(2):
                n = 2 * hh + a
                lanes = slice(n * Q_BLOCK, (n + 1) * Q_BLOCK)
                c0 = N_BRANCH * n
                tiles.append(gate_t[c0:c0 + 1, :] * o_c[:, lanes]
                             + gate_t[c0 + 1:c0 + 2, :] * o_s[:, lanes]
                             + gate_t[c0 + 2:c0 + 3, :] * o_w[:, lanes])
            o_ref[0, pl.ds(row0, Q_BLOCK), hh * LANES:(hh + 1) * LANES] = jnp.concatenate(tiles, axis=0).T


def _sel_map_t(seq):
    n_c = seq // CMP_STRIDE - 1
    n_s = seq // SEL_BLOCK
    tok = np.arange(seq)
    start = np.arange(n_c) * CMP_STRIDE
    cover_c = (tok[None, :] >= start[:, None]) & (tok[None, :] < start[:, None] + CMP_BLOCK)
    cover_s = (tok[:, None] // SEL_BLOCK) == np.arange(n_s)[None, :]
    m = cover_c.astype(np.float32) @ cover_s.astype(np.float32) / np.float32(CMP_BLOCK)
    out = np.zeros((n_s, n_c + 1), np.float32)
    out[:, :n_c] = m.T
    return jnp.asarray(out)


def _score_bounds(q_gain, k_gain):
    bound = 1.01 * HEAD_DIM ** 0.5 * LOG2_E * jnp.max(jnp.abs(q_gain)) * jnp.max(jnp.abs(k_gain), axis=-1)
    return jnp.broadcast_to(bound[:, None], (N_BRANCH, LANES)).astype(F32)


def _attention(q, gates, kvc, kv, qg_t, kg_ext, shift, shifted):
    b, s, _ = q.shape
    g = NSA_GROUPS
    nqb = s // Q_BLOCK
    ncmp = s // CMP_STRIDE
    nsel = s // SEL_BLOCK
    assert s % SEL_CHUNK == 0 and nsel % SUBLANES == 0 and nqb % 2 == 0
    gw = NSA_HPG * HEAD_DIM
    cols = NSA_HPG * Q_BLOCK
    n_slots = _sel_slot_count(nqb)

    def kv_spec(j):
        return pl.BlockSpec((1, 1, 1, s, LANES), lambda bi, gi, i: (j, bi, gi, 0, 0))

    return pl.pallas_call(
        functools.partial(_attn_kernel, shifted=shifted),
        grid=(b, g, nqb // 2),
        in_specs=[
            pl.BlockSpec((1, s, gw), lambda bi, gi, i: (bi, 0, gi)),
            pl.BlockSpec((1, 1, s, LANES), lambda bi, gi, i: (bi, gi, 0, 0)),
            pl.BlockSpec((1, 1, ncmp, LANES), lambda bi, gi, i: (bi, gi, 0, 0)),
            kv_spec(1), kv_spec(2),
            pl.BlockSpec((nsel, ncmp), lambda bi, gi, i: (0, 0)),
            pl.BlockSpec((HEAD_DIM, Q_BLOCK), lambda bi, gi, i: (0, 0)),
            pl.BlockSpec((N_BRANCH, LANES), lambda bi, gi, i: (0, 0)),
            pl.BlockSpec((N_BRANCH, LANES), lambda bi, gi, i: (0, 0)),
        ],
        out_specs=pl.BlockSpec((1, s, gw), lambda bi, gi, i: (bi, 0, gi)),
        out_shape=jax.ShapeDtypeStruct((b, s, Q_DIM), F32),
        scratch_shapes=[
            pltpu.VMEM((s, LANES), BF16),
            pltpu.VMEM((WINDOW + s, LANES), BF16),
            pltpu.VMEM((s // LANES, VT_ROWS, LANES), BF16),
            pltpu.VMEM(((WINDOW + s) // LANES, VT_ROWS, LANES), BF16),
            pltpu.VMEM((2, LANES, cols), BF16),
            pltpu.VMEM((n_slots, 1, cols), F32),
            pltpu.VMEM((n_slots, 1, cols), F32),
            pltpu.VMEM((n_slots, HEAD_DIM, cols), F32),
        ],
        compiler_params=_cparams(3),
        name="nsa_attention",
    )(q, gates, kvc, kv, kv, _sel_map_t(s), qg_t, kg_ext, shift)


def _ffn_core(x, g_ref, sc_ref, sh_ref, gate_ref, wg_ref, wu_ref, wo_ref):
    hh = _norm_mod(x, g_ref[...], sc_ref[0], sh_ref[0]).astype(BF16)
    a = jnp.dot(hh, wg_ref[...], preferred_element_type=F32)
    u = jnp.dot(hh, wu_ref[...], preferred_element_type=F32)
    act = (jax.nn.silu(a) * u).astype(BF16)
    return x + gate_ref[0] * jnp.dot(act, wo_ref[...], preferred_element_type=F32)


def _ffn_kernel(x_ref, g_ref, sc_ref, sh_ref, gate_ref, wg_ref, wu_ref, wo_ref, o_ref):
    o_ref[0] = _ffn_core(x_ref[0], g_ref, sc_ref, sh_ref, gate_ref, wg_ref, wu_ref, wo_ref)


def _mix_ffn_kernel(y_ref, wmix_ref, mix_gate_ref, x_ref, g_ref, sc_ref, sh_ref, gate_ref, wg_ref, wu_ref, wo_ref,
                    o_ref):
    mix = jnp.dot(y_ref[0].astype(BF16), wmix_ref[...], preferred_element_type=F32)
    x = x_ref[0] + mix_gate_ref[0] * mix
    o_ref[0] = _ffn_core(x, g_ref, sc_ref, sh_ref, gate_ref, wg_ref, wu_ref, wo_ref)


def _ffn(x, gain, sc, sh, gate, w_in, w_out, layer, mixer_out=None):
    b, s, d = x.shape
    hidden = w_out.shape[1]
    tm = min(ROW_TILE, s)
    vec = pl.BlockSpec((1, 1, d), lambda bi, i: (bi, 0, 0))
    rows = lambda width: pl.BlockSpec((1, tm, width), lambda bi, i: (bi, i, 0))
    in_specs = [
        rows(d), _resident((1, d)), vec, vec, vec,
        _layer_plane((d, hidden), layer, 0), _layer_plane((d, hidden), layer, 1),
        _layer_plane((hidden, d), layer),
    ]
    args = (x, gain, sc, sh, gate, w_in, w_in, w_out)
    body = _ffn_kernel
    if mixer_out is not None:
        y, w_mix, j, mix_gate = mixer_out
        in_specs = [rows(y.shape[-1]), _layer_plane(w_mix.shape[1:], j), vec] + in_specs
        args = (y, w_mix, mix_gate) + args
        body = _mix_ffn_kernel
    return pl.pallas_call(
        body,
        grid=(b, s // tm),
        in_specs=in_specs,
        out_specs=rows(d),
        out_shape=jax.ShapeDtypeStruct((b, s, d), F32),
        compiler_params=_cparams(2),
        name="ffn",
    )(*args)


def _band_starts(width, block_w):
    band = 2 * MXU_WIDTH
    starts = []
    for c0 in range(0, width, MXU_WIDTH):
        c1 = min(c0 + MXU_WIDTH, width)
        lo = (c0 // block_w) * block_w
        hi = ((c1 - 1) // block_w + 1) * block_w
        k0 = min((lo // LANES) * LANES, width - band)
        assert k0 >= 0 and k0 <= lo and hi <= k0 + band
        starts.append(k0)
    return starts


def _band_weights(w_a, w_x):
    nb, bw, _ = w_a.shape
    width = nb * bw
    spread = jnp.asarray(np.tile(np.eye(bw, dtype=np.float32), (1, nb)), dtype=BF16)
    blk = np.arange(width) // bw
    on_diag = jnp.asarray(blk[:, None] == blk[None, :])
    dense = lambda w: jnp.where(
        on_diag, jnp.dot(w.reshape(width, bw).astype(BF16), spread, preferred_element_type=BF16), 0)
    da, dx = dense(w_a), dense(w_x)
    band = 2 * MXU_WIDTH
    tiles = []
    for t, k0 in enumerate(_band_starts(width, bw)):
        c0 = t * MXU_WIDTH
        c1 = min(c0 + MXU_WIDTH, width)
        pad = ((0, 0), (0, MXU_WIDTH - (c1 - c0)))
        tiles.append(jnp.concatenate(
            [jnp.pad(da[k0:k0 + band, c0:c1], pad), jnp.pad(dx[k0:k0 + band, c0:c1], pad)], axis=1))
    return 0.5 * jnp.stack(tiles)


def _rg_mixer_kernel(x_ref, xn_ref, g_ref, sc_ref, sh_ref, res_gate_ref, win_ref, cw_ref, cb_ref, wb_ref, ba_ref,
                     bx_ref, lam_ref, wout_ref, o_ref,
                     u_a, u_b, gate_a, gate_b, h_scr, a_scr, b_scr, y_scr, *, band_starts):
    step = pl.program_id(0)
    nb, ts, d = x_ref.shape
    width = wout_ref.shape[0]
    rows = nb * ts
    halo = CONV_WIDTH - 1
    band = 2 * MXU_WIDTH

    pieces = [(c0, min(c0 + MXU_WIDTH, 2 * width)) for c0 in range(0, 2 * width, MXU_WIDTH)]

    def project_piece(hh, piece, u_dst, gate_dst):
        c0, c1 = pieces[piece]
        res = jnp.dot(hh, win_ref[:, c0:c1], preferred_element_type=F32)
        n_gate = max(0, min(c1, width) - c0)
        if n_gate:
            gate_dst[:, c0:c0 + n_gate] = res[:, :n_gate]
        if c1 > width:
            u0 = max(c0, width) - width
            u_dst[:, :, u0:c1 - width] = jnp.swapaxes(res[:, n_gate:].reshape(nb, ts, c1 - c0 - n_gate), 0, 1)

    def normed(tile_ref):
        return _norm_mod(tile_ref[...], g_ref[...], sc_ref[...], sh_ref[...]).reshape(rows, d).astype(BF16)

    @pl.when(step == 0)
    def _init():
        u_b[...] = jnp.zeros((ts, nb, width), F32)
        h_scr[...] = jnp.zeros(h_scr.shape, F32)
        hh = normed(x_ref)
        for piece in range(len(pieces)):
            project_piece(hh, piece, u_a, gate_a)

    def tile_step(u_cur, gate_cur, u_nxt, gate_nxt):
        prev_tail = u_nxt[ts - halo:ts]
        hh_next = normed(xn_ref)
        todo = list(range(len(pieces)))
        per_stage = -(-len(pieces) // len(band_starts))

        uc = jnp.zeros((ts, nb, width), F32) + cb_ref[...]
        for k in range(CONV_WIDTH):
            shifted = u_cur[...] if k == halo else jnp.concatenate([prev_tail[k:], u_cur[0:ts - halo + k]], axis=0)
            uc = uc + cw_ref[k:k + 1, :] * shifted
        uc = uc.reshape(rows, width)
        ub = uc.astype(BF16)

        z = -lam_ref[...]
        log_a_unit = -RG_C * (jnp.maximum(z, 0.0) + jnp.log1p(jnp.exp(-jnp.abs(z))))

        for t, k0 in enumerate(band_starts):
            c0 = t * MXU_WIDTH
            c1 = min(c0 + MXU_WIDTH, width)
            n = c1 - c0
            zz = jnp.dot(ub[:, k0:k0 + band], wb_ref[t], preferred_element_type=F32)
            for piece in todo[:per_stage]:
                project_piece(hh_next, piece, u_nxt, gate_nxt)
            todo = todo[per_stage:]
            half_unit = 0.5 * log_a_unit[:, c0:c1]
            half_u = 0.5 * uc[:, c0:c1]
            log_a = half_unit * jnp.tanh(zz[:, :n] + ba_ref[:, c0:c1]) + half_unit
            gated_u = half_u * jnp.tanh(zz[:, MXU_WIDTH:MXU_WIDTH + n] + bx_ref[:, c0:c1]) + half_u
            a = jnp.exp(log_a)
            bb = jnp.sqrt(-jnp.tanh(log_a) * (a * a + 1.0)) * gated_u
            for ct in range(n // LANES):
                lanes = slice(ct * LANES, (ct + 1) * LANES)
                a_scr[c0 // LANES + ct] = a[:, lanes]
                b_scr[c0 // LANES + ct] = bb[:, lanes]

        n_ct = width // LANES

        def scan_step(t, h):
            rows_t = pl.ds(pl.multiple_of(t * nb, nb), nb)
            new = []
            for ct in range(n_ct):
                hc = a_scr[ct, rows_t, :] * h[ct] + b_scr[ct, rows_t, :]
                b_scr[ct, rows_t, :] = hc
                new.append(hc)
            return tuple(new)

        h_fin = lax.fori_loop(0, ts, scan_step, tuple(h_scr[ct] for ct in range(n_ct)), unroll=SUBLANES)
        for ct in range(n_ct):
            lanes = slice(ct * LANES, (ct + 1) * LANES)
            h_scr[ct] = h_fin[ct]
            hs = jnp.swapaxes(b_scr[ct].reshape(ts, nb, LANES), 0, 1)
            y_scr[:, lanes] = (hs.reshape(rows, LANES) * jax.nn.gelu(gate_cur[:, lanes])).astype(BF16)
        mix = jnp.dot(y_scr[...], wout_ref[...], preferred_element_type=F32)
        o_ref[...] = x_ref[...] + res_gate_ref[...] * mix.reshape(nb, ts, d)

    @pl.when(step % 2 == 0)
    def _even():
        tile_step(u_a, gate_a, u_b, gate_b)

    @pl.when(step % 2 == 1)
    def _odd():
        tile_step(u_b, gate_b, u_a, gate_a)


def _rg_mixer(x, gain, sc, sh, res_gate, w_in, conv_w, conv_b, w_band, b_a, b_x, lam, w_out, j, block_w):
    b, s, d = x.shape
    width = w_out.shape[1]
    assert b == SUBLANES
    ts = min(RG_TIME_TILE, s)
    rows = b * ts
    band_starts = tuple(_band_starts(width, block_w))
    n_tiles = s // ts
    tile = pl.BlockSpec((b, ts, d), lambda t: (0, t, 0))
    next_tile = pl.BlockSpec((b, ts, d), lambda t: (0, jnp.minimum(t + 1, n_tiles - 1), 0))
    mod = _resident((b, 1, d))
    vec = _resident((1, width))
    kern = functools.partial(_rg_mixer_kernel, band_starts=band_starts)
    return pl.pallas_call(
        kern,
        grid=(n_tiles,),
        in_specs=[
            tile, next_tile, _resident((1, d)), mod, mod, mod,
            _layer_plane(w_in.shape[1:], j),
            _resident((CONV_WIDTH, width)), vec,
            _resident(w_band.shape),
            vec, vec, vec,
            _layer_plane(w_out.shape[1:], j),
        ],
        out_specs=tile,
        out_shape=jax.ShapeDtypeStruct((b, s, d), F32),
        scratch_shapes=[
            pltpu.VMEM((ts, b, width), F32),
            pltpu.VMEM((ts, b, width), F32),
            pltpu.VMEM((rows, width), F32),
            pltpu.VMEM((rows, width), F32),
            pltpu.VMEM((width // LANES, b, LANES), F32),
            pltpu.VMEM((width // LANES, rows, LANES), F32),
            pltpu.VMEM((width // LANES, rows, LANES), F32),
            pltpu.VMEM((rows, width), BF16),
        ],
        compiler_params=_cparams(1),
        name="rg_mixer",
    )(x, x, gain, sc, sh, res_gate, w_in, conv_w, conv_b, w_band, b_a, b_x, lam, w_out)


def kernel(x, c, ada_w, ada_b, norm1_g, norm2_g, nsa_w_in, nsa_w_out, nsa_cmp_pos, nsa_cmp_w1, nsa_cmp_w2, nsa_q_gain, nsa_k_gain, rg_w_in, rg_conv_w, rg_conv_b, rg_w_a, rg_b_a, rg_w_x, rg_b_x, rg_lam, rg_w_out, ffn_w_in, ffn_w_out):
    depth, d, _ = ada_w.shape
    b = x.shape[0]
    mod = _ada_mod(c, ada_w, ada_b).reshape(depth, b, 6, 1, d)
    ffn_w_in, ffn_w_out, nsa_w_out, rg_w_in, rg_w_out = (
        w.astype(BF16) for w in (ffn_w_in, ffn_w_out, nsa_w_out, rg_w_in, rg_w_out))
    for layer in range(depth):
        sh1, sc1, g1, sh2, sc2, g2 = (mod[layer, :, k] for k in range(6))
        n1 = norm1_g[layer].reshape(1, d)
        n2 = norm2_g[layer].reshape(1, d)
        j = layer // N_MIXERS
        if layer % N_MIXERS == 0:
            kg_ext = jnp.concatenate([nsa_k_gain[j], jnp.zeros_like(nsa_k_gain[j])], axis=1)
            qg_t = jnp.broadcast_to(nsa_q_gain[j][:, None], (HEAD_DIM, Q_BLOCK))
            q, kv, gates = _nsa_proj(x, n1, sc1, sh1, _nsa_weight_layout(nsa_w_in[j]))
            kvc = _compress(kv, *_compress_weights(nsa_cmp_pos[j], nsa_cmp_w1[j], nsa_cmp_w2[j]), kg_ext)
            bounds = _score_bounds(nsa_q_gain[j], nsa_k_gain[j])
            o = lax.cond(
                2.0 * jnp.max(bounds) <= MAX_SHIFTED_SCORE_RANGE,
                lambda *a: _attention(*a, bounds, True),
                lambda *a: _attention(*a, jnp.zeros_like(bounds), False),
                q, gates, kvc, kv, qg_t, kg_ext)
            mixer_out = (o, nsa_w_out, j, g1)
        else:
            width = rg_w_out.shape[1]
            block_w = rg_w_a.shape[-1]
            x = _rg_mixer(x, n1, sc1, sh1, g1, rg_w_in,
                          rg_conv_w[j], rg_conv_b[j].reshape(1, width),
                          _band_weights(rg_w_a[j], rg_w_x[j]),
                          0.5 * rg_b_a[j].reshape(1, width), 0.5 * rg_b_x[j].reshape(1, width),
                          rg_lam[j].reshape(1, width), rg_w_out, j, block_w)
            mixer_out = None
        x = _ffn(x, n2, sc2, sh2, g2, ffn_w_in, ffn_w_out, layer, mixer_out)
    return x
```

```python
import functools

import numpy as np
import jax
import jax.numpy as jnp
from jax import lax
from jax.experimental import pallas as pl
from jax.experimental.pallas import tpu as pltpu

F32 = jnp.float32
BF16 = jnp.bfloat16

EPS = 1e-6
N_MIXERS = 2

NSA_HEADS = 16
NSA_GROUPS = 4
NSA_HPG = NSA_HEADS // NSA_GROUPS
HEAD_DIM = 64
CMP_STRIDE = 16
CMP_BLOCK = 2 * CMP_STRIDE
SEL_BLOCK = 64
SEL_TOP_N = 8
WINDOW = 512
Q_BLOCK = 128
N_BRANCH = 3
Q_DIM = NSA_HEADS * HEAD_DIM
KV_DIM = NSA_GROUPS * HEAD_DIM
GATE_COLS = N_BRANCH * NSA_HEADS
GATE_PER_GROUP = N_BRANCH * NSA_HPG

CONV_WIDTH = 4
RG_C = 8.0

LANES = 128
SUBLANES = 8
MXU_WIDTH = 256
VMEM_LIMIT_BYTES = 56 * 1024 * 1024

NEG_BIG = -1e30
LOG2_E = 1.4426950408889634
BLOCK_LANE0 = HEAD_DIM
PAD_LANE = LANES - 1
SHIFT_LANE = LANES - 2
MAX_SHIFTED_SCORE_RANGE = 100.0
VT_ROWS = HEAD_DIM + 16
SEL_CHUNK = 256
SCORE_LOOKAHEAD = 3
RG_TIME_TILE = 64
ROW_TILE = 512


def _cparams(n_axes):
    return pltpu.CompilerParams(
        dimension_semantics=("arbitrary",) * n_axes,
        vmem_limit_bytes=VMEM_LIMIT_BYTES,
    )


def _resident(shape):
    return pl.BlockSpec(shape, lambda *_: (0,) * len(shape), pipeline_mode=pl.Buffered(1))


def _layer_plane(shape, layer, col_block=0):
    return pl.BlockSpec((None,) + tuple(shape), lambda *_: (layer, 0, col_block), pipeline_mode=pl.Buffered(1))


def _norm_mod(x, gain, scale, shift):
    ms = jnp.mean(x * x, axis=-1, keepdims=True)
    y = x * lax.rsqrt(ms + EPS) * gain
    return y * (1.0 + scale) + shift


def _ada_kernel(c_ref, w_ref, b_ref, o_ref):
    cond = jax.nn.silu(c_ref[...])
    o_ref[0] = jnp.dot(cond, w_ref[0], preferred_element_type=F32,
                       precision=lax.Precision.HIGHEST) + b_ref[0]


def _ada_mod(c, ada_w, ada_b):
    depth, d, n = ada_w.shape
    b = c.shape[0]
    tn = n // 4
    return pl.pallas_call(
        _ada_kernel,
        grid=(depth, n // tn),
        in_specs=[
            pl.BlockSpec((b, d), lambda l, j: (0, 0)),
            pl.BlockSpec((1, d, tn), lambda l, j: (l, 0, j)),
            pl.BlockSpec((1, 1, tn), lambda l, j: (l, 0, j)),
        ],
        out_specs=pl.BlockSpec((1, b, tn), lambda l, j: (l, 0, j)),
        out_shape=jax.ShapeDtypeStruct((depth, b, n), F32),
        compiler_params=_cparams(2),
        name="ada_mod",
    )(c, ada_w, ada_b.reshape(depth, 1, n))


def _nsa_weight_layout(w_in):
    w_in = w_in.astype(BF16)
    src = np.arange(6 * KV_DIM).reshape(N_BRANCH, 2, NSA_GROUPS, HEAD_DIM).transpose(0, 2, 1, 3).reshape(-1)
    perm = np.zeros((6 * KV_DIM, 6 * KV_DIM), np.float32)
    perm[src, np.arange(6 * KV_DIM)] = 1.0
    kv = jnp.dot(w_in[:, Q_DIM:Q_DIM + 6 * KV_DIM], jnp.asarray(perm, dtype=BF16), preferred_element_type=BF16)
    gates = jnp.pad(w_in[:, Q_DIM + 6 * KV_DIM:], ((0, 0), (0, LANES - GATE_COLS)))
    return jnp.concatenate([w_in[:, :Q_DIM], kv, gates], axis=1)


def _nsa_proj_kernel(x_ref, g_ref, sc_ref, sh_ref, w_ref, q_ref, kv_ref, gate_ref):
    hh = _norm_mod(x_ref[0], g_ref[...], sc_ref[0], sh_ref[0])
    p = jnp.dot(hh.astype(BF16), w_ref[...], preferred_element_type=F32)
    q_ref[0] = p[:, :Q_DIM]
    for j in range(N_BRANCH):
        for g in range(NSA_GROUPS):
            lo = Q_DIM + (j * NSA_GROUPS + g) * LANES
            kv_ref[j, 0, g] = p[:, lo:lo + LANES]
    gate = jax.nn.sigmoid(p[:, Q_DIM + 6 * KV_DIM:])
    for g in range(NSA_GROUPS):
        shift = (LANES - GATE_PER_GROUP * g) % LANES
        gate_ref[0, g] = gate if shift == 0 else pltpu.roll(gate, shift, 1)


def _nsa_proj(x, gain, sc, sh, w):
    b, s, d = x.shape
    tm = min(ROW_TILE, s)
    n = w.shape[1]
    vec = pl.BlockSpec((1, 1, d), lambda bi, i: (bi, 0, 0))
    return pl.pallas_call(
        _nsa_proj_kernel,
        grid=(b, s // tm),
        in_specs=[
            pl.BlockSpec((1, tm, d), lambda bi, i: (bi, i, 0)),
            pl.BlockSpec((1, d), lambda bi, i: (0, 0)),
            vec, vec,
            pl.BlockSpec((d, n), lambda bi, i: (0, 0)),
        ],
        out_specs=[
            pl.BlockSpec((1, tm, Q_DIM), lambda bi, i: (bi, i, 0)),
            pl.BlockSpec((N_BRANCH, 1, NSA_GROUPS, tm, LANES), lambda bi, i: (0, bi, 0, i, 0)),
            pl.BlockSpec((1, NSA_GROUPS, tm, LANES), lambda bi, i: (bi, 0, i, 0)),
        ],
        out_shape=[
            jax.ShapeDtypeStruct((b, s, Q_DIM), F32),
            jax.ShapeDtypeStruct((N_BRANCH, b, NSA_GROUPS, s, LANES), F32),
            jax.ShapeDtypeStruct((b, NSA_GROUPS, s, LANES), F32),
        ],
        compiler_params=_cparams(2),
        name="nsa_proj",
    )(x, gain, sc, sh, w)


def _k_lane_norm(t, gain_ext):
    lane = lax.broadcasted_iota(jnp.int32, t.shape, 1)
    ms = jnp.sum(jnp.where(lane < HEAD_DIM, t * t, 0.0), axis=-1, keepdims=True) * (1.0 / HEAD_DIM)
    return t * lax.rsqrt(ms + EPS) * gain_ext


def _compress_weights(pos, w1, w2):
    hidden = w1.shape[-1]
    w1p = w1.reshape(2, 2, CMP_STRIDE, HEAD_DIM, hidden)
    z = jnp.zeros_like(w1p[0])
    wk = jnp.concatenate([w1p[0], z], axis=-1)
    wv = jnp.concatenate([z, w1p[1]], axis=-1)
    w1_blk = jnp.concatenate([wk, wv], axis=2).astype(BF16)
    w1_blk = w1_blk.reshape(2, CMP_STRIDE * LANES, 2 * hidden)
    pos_blk = jnp.concatenate([pos[0], pos[1]], axis=-1).reshape(2, CMP_STRIDE, 1, LANES)
    z2 = jnp.zeros_like(w2[0])
    w2_blk = jnp.concatenate([jnp.concatenate([w2[0], z2], axis=1),
                              jnp.concatenate([z2, w2[1]], axis=1)], axis=0).astype(BF16)
    return pos_blk, w1_blk, w2_blk


def _compress_kernel(kv_ref, pos_ref, w1_ref, w2_ref, kg_ref, o_ref):
    g, seq = kv_ref.shape[2:4]
    nch = seq // CMP_STRIDE
    rows = g * nch
    toks = [jnp.concatenate([kv_ref[0, 0, gi, pl.ds(p, nch, stride=CMP_STRIDE), :] for gi in range(g)], axis=0)
            for p in range(CMP_STRIDE)]
    halves = []
    for half in range(2):
        chunk = jnp.concatenate([(toks[p] + pos_ref[half, p]).astype(BF16) for p in range(CMP_STRIDE)], axis=1)
        halves.append(jnp.dot(chunk, w1_ref[half], preferred_element_type=F32))
    first, second = halves
    h1 = first + pltpu.roll(second, rows - 1, 0)
    out = jnp.dot(jax.nn.silu(h1).astype(BF16), w2_ref[...], preferred_element_type=F32)
    lane = lax.broadcasted_iota(jnp.int32, out.shape, 1)
    out = jnp.where(lane < HEAD_DIM, _k_lane_norm(out, kg_ref[0:1]), out)
    r = lax.broadcasted_iota(jnp.int32, out.shape, 0)
    out = jnp.where((r & (nch - 1)) == nch - 1, 0.0, out)
    o_ref[0] = out.reshape(g, nch, LANES)


def _compress(kv, pos_blk, w1_blk, w2_blk, kg_ext):
    _, b, g, s, _ = kv.shape
    nch = s // CMP_STRIDE
    assert nch & (nch - 1) == 0
    return pl.pallas_call(
        _compress_kernel,
        grid=(b,),
        in_specs=[
            pl.BlockSpec((1, 1, g, s, LANES), lambda bi: (0, bi, 0, 0, 0)),
            pl.BlockSpec(pos_blk.shape, lambda bi: (0, 0, 0, 0)),
            pl.BlockSpec(w1_blk.shape, lambda bi: (0, 0, 0)),
            pl.BlockSpec(w2_blk.shape, lambda bi: (0, 0)),
            pl.BlockSpec((N_BRANCH, LANES), lambda bi: (0, 0)),
        ],
        out_specs=pl.BlockSpec((1, g, nch, LANES), lambda bi: (bi, 0, 0, 0)),
        out_shape=jax.ShapeDtypeStruct((b, g, nch, LANES), F32),
        compiler_params=_cparams(1),
        name="nsa_compress",
    )(kv, pos_blk, w1_blk, w2_blk, kg_ext)


def _sel_slot_count(nqb):
    chunks = lambda i: -(-(i + 1) * Q_BLOCK // SEL_CHUNK)
    counts = {chunks(p) + chunks(nqb - 1 - p) for p in range(nqb // 2)}
    assert len(counts) == 1, counts
    return counts.pop()


def _tile_scores(k_tile, q_t):
    return jnp.concatenate([jnp.dot(k_tile, q_t[:, h:h + MXU_WIDTH], preferred_element_type=F32)
                            for h in range(0, q_t.shape[1], MXU_WIDTH)], axis=1)


def _tile_softmax(s, bias, vt_tile, shifted):
    if bias is not None:
        rows = bias.shape[0]
        biased = s[:rows] + jnp.concatenate([bias] * (s.shape[1] // Q_BLOCK), axis=1)
        s = biased if rows == s.shape[0] else jnp.concatenate([biased, s[rows:]], axis=0)
    m = None if shifted else jnp.max(s, axis=0, keepdims=True)
    p = jnp.exp2(s if shifted else s - m)
    ol = jnp.dot(vt_tile, p.astype(BF16), preferred_element_type=F32)
    return m, ol[HEAD_DIM:HEAD_DIM + 1], ol[:HEAD_DIM]


def _merge_tiles(parts):
    if parts[0][0] is None:
        weights = [None if mine is None else jnp.where(mine, 1.0, 0.0) for _, _, _, mine in parts]
    else:
        ms = [mi if mine is None else jnp.where(mine, mi, NEG_BIG) for mi, _, _, mine in parts]
        m = functools.reduce(jnp.maximum, ms)
        weights = [jnp.exp2(mi - m) for mi in ms]
    l = None
    o = None
    for w, (_, li, oi, _) in zip(weights, parts):
        if w is not None:
            li, oi = w * li, w * oi
        l = li if l is None else l + li
        o = oi if o is None else o + oi
    return o / l


def _attn_kernel(q_ref, gate_ref, kvc_ref, ks_ref, kw_ref, selmap_ref, qg_ref, kg_ref, shift_ref, o_ref,
                 ksn, kwn, vst, vwt, qt_scr, slot_m, slot_l, slot_o, *, shifted):
    pair = pl.program_id(2)
    seq = ks_ref.shape[3]
    ncmp = kvc_ref.shape[2]
    nsel = selmap_ref.shape[0]
    nqb = seq // Q_BLOCK
    cols = NSA_HPG * Q_BLOCK
    win_tiles = WINDOW // LANES
    chunk_tiles = SEL_CHUNK // LANES
    scale = HEAD_DIM ** -0.5 * LOG2_E

    @pl.when(pair == 0)
    def _prepare_kv():
        lane = lax.broadcasted_iota(jnp.int32, (LANES, LANES), 1)
        row = lax.broadcasted_iota(jnp.int32, (LANES, LANES), 0)
        vrow = lax.broadcasted_iota(jnp.int32, (VT_ROWS, LANES), 0)
        vlane = lax.broadcasted_iota(jnp.int32, (VT_ROWS, LANES), 1)
        pick_v = ((vlane == vrow + HEAD_DIM) & (vrow < HEAD_DIM)).astype(BF16)
        ones_row = (vrow == HEAD_DIM).astype(F32)
        v_plane = lambda tile: (lax.dot_general(pick_v, tile.astype(BF16), (((1,), (1,)), ((), ())),
                                                preferred_element_type=F32) + ones_row).astype(BF16)
        pad_mark = (lane == PAD_LANE).astype(BF16)
        for w in range(win_tiles):
            kwn[w * LANES:(w + 1) * LANES, :] = pad_mark
        vwt[0:win_tiles] = jnp.zeros((win_tiles, VT_ROWS, LANES), BF16)

        def body(c, carry):
            r0 = pl.multiple_of(c * LANES, LANES)
            ts = ks_ref[0, 0, 0, pl.ds(r0, LANES), :]
            in_block = (lane - BLOCK_LANE0 == lax.shift_right_logical(r0 + row, SEL_BLOCK.bit_length() - 1))
            marks = (in_block | (lane == SHIFT_LANE)).astype(F32)
            ksn[pl.ds(r0, LANES), :] = (_k_lane_norm(ts, kg_ref[1:2]) + marks).astype(BF16)
            vst[c] = v_plane(ts)
            tw = kw_ref[0, 0, 0, pl.ds(r0, LANES), :]
            kwn[pl.ds(WINDOW + r0, LANES), :] = (
                _k_lane_norm(tw, kg_ref[2:3]) + (lane == SHIFT_LANE).astype(F32)).astype(BF16)
            vwt[win_tiles + c] = v_plane(tw)
            return carry

        lax.fori_loop(0, seq // LANES, body, 0, unroll=4)

    kc = kvc_ref[0, 0]
    kc_b = kc.astype(BF16)
    vc_t = kc.T[HEAD_DIM:, :].astype(BF16)
    block_ids = (pair, nqb - 1 - pair)

    def block_branches(idx):
        i = block_ids[idx]
        t0 = i * Q_BLOCK
        row0 = pl.multiple_of(t0, Q_BLOCK)

        q = q_ref[0, pl.ds(row0, Q_BLOCK), :]
        heads = []
        for h in range(NSA_HPG // 2):
            qt = q[:, h * LANES:(h + 1) * LANES].T
            for a in range(2):
                x = qt[a * HEAD_DIM:(a + 1) * HEAD_DIM]
                ms = jnp.mean(x * x, axis=0, keepdims=True)
                heads.append(x * lax.rsqrt(ms + EPS) * qg_ref[...] * scale)
        q_top = jnp.concatenate(heads, axis=1).astype(BF16)
        mask_rows = lax.broadcasted_iota(jnp.int32, (LANES - HEAD_DIM, cols), 0) + HEAD_DIM
        q_plain = jnp.concatenate([q_top, jnp.zeros((LANES - HEAD_DIM, cols), BF16)], axis=0)
        shift_rows = lambda br: jnp.where(
            mask_rows == SHIFT_LANE, -jnp.concatenate([shift_ref[br:br + 1, :]] * NSA_HPG, axis=1), 0.0)
        q_win = jnp.concatenate(
            [q_top, jnp.where(mask_rows == PAD_LANE, NEG_BIG, shift_rows(2)).astype(BF16)], axis=0)

        tiles = []
        off = 0
        span = WINDOW + Q_BLOCK
        while off < span:
            kc_w = min(SEL_CHUNK, span - off)
            if off < Q_BLOCK:
                rows_b = min(kc_w, Q_BLOCK - off)
                newest = False
            elif off + kc_w > WINDOW:
                assert off >= WINDOW
                rows_b = kc_w
                newest = True
            else:
                rows_b = 0
                newest = False

            def scores(off=off, kc_w=kc_w):
                k_tile = kwn[pl.ds(pl.multiple_of(t0 + off, LANES), kc_w), :]
                return _tile_scores(k_tile, q_win)

            def finish(s, off=off, kc_w=kc_w, rows_b=rows_b, newest=newest):
                bias = None
                if rows_b:
                    r = off + lax.broadcasted_iota(jnp.int32, (rows_b, Q_BLOCK), 0)
                    c = lax.broadcasted_iota(jnp.int32, (rows_b, Q_BLOCK), 1)
                    bias = jnp.where((r - WINDOW <= c) if newest else (r > c), 0.0, NEG_BIG)
                vt_tile = jnp.concatenate([vwt[i + off // LANES + cc] for cc in range(kc_w // LANES)], axis=1)
                return _tile_softmax(s, bias, vt_tile, shifted)

            tiles.append((scores, finish))
            off += kc_w
        yield tiles

        s = _tile_scores(kc_b, q_plain)
        yield
        tq = t0 + (lax.broadcasted_iota(jnp.int32, (ncmp, cols), 1) & (Q_BLOCK - 1))
        cmp_end = lax.broadcasted_iota(jnp.int32, (ncmp, cols), 0) * CMP_STRIDE + (CMP_BLOCK - 1)
        visible = cmp_end <= tq
        s = jnp.where(visible, s, NEG_BIG)
        m = jnp.max(s, axis=0, keepdims=True)
        p = jnp.where(visible, jnp.exp2(s - m), 0.0)
        l = jnp.sum(p, axis=0, keepdims=True)
        p_c = p / jnp.maximum(l, jnp.finfo(F32).tiny)
        o_c = jnp.dot(vc_t, p_c.astype(BF16), preferred_element_type=F32)

        yield
        p_sum = p_c[:, 0:Q_BLOCK]
        for n in range(1, NSA_HPG):
            p_sum = p_sum + p_c[:, n * Q_BLOCK:(n + 1) * Q_BLOCK]
        sel_map = selmap_ref[...].astype(BF16)
        hi = p_sum.astype(BF16)
        rest = p_sum - hi.astype(F32)
        mid = rest.astype(BF16)
        lo = (rest - mid.astype(F32)).astype(BF16)
        imp = (jnp.dot(sel_map, hi, preferred_element_type=F32) + jnp.dot(sel_map, mid, preferred_element_type=F32)
               + jnp.dot(sel_map, lo, preferred_element_type=F32))
        blk = lax.broadcasted_iota(jnp.int32, (nsel, Q_BLOCK), 0)
        cur = lax.shift_right_logical(t0 + lax.broadcasted_iota(jnp.int32, (nsel, Q_BLOCK), 1), 6)
        valid = blk <= cur
        forced = (blk == 0) | (blk == cur) | (blk == cur - 1)
        val = jnp.where(valid, imp, -jnp.inf)
        val = jnp.where(forced, jnp.inf, val)
        rank = jnp.zeros((nsel, Q_BLOCK), jnp.int32)
        for j2 in range(nsel):
            row = val[j2:j2 + 1, :]
            beats = (row > val) | ((row == val) & (blk > j2))
            rank = rank + beats.astype(jnp.int32)
            yield
        picked = jnp.where((rank < SEL_TOP_N) & valid, 0.0, NEG_BIG).astype(BF16)
        qt_scr[idx] = jnp.concatenate(
            [q_top, jnp.concatenate([picked] * NSA_HPG, axis=1),
             shift_rows(1)[nsel:].astype(BF16)], axis=0)
        gate_t = gate_ref[0, 0, pl.ds(row0, Q_BLOCK), :].T
        return o_c, gate_t

    all_tiles = []
    pending = []
    results = []

    def pump():
        while len(results) + len(pending) < min(len(all_tiles), len(results) + SCORE_LOOKAHEAD + 1):
            pending.append(all_tiles[len(results) + len(pending)][0]())
        results.append(all_tiles[len(results)][1](pending.pop(0)))

    per_block = [None, None]
    running = {idx: block_branches(idx) for idx in range(2)}
    for idx in range(2):
        all_tiles += next(running[idx])
    n_win = len(all_tiles) // 2
    while running:
        for idx, stages in list(running.items()):
            try:
                next(stages)
            except StopIteration as done:
                per_block[idx] = done.value
                del running[idx]

    n_slots = _sel_slot_count(nqb)
    n0 = lax.shift_right_logical(block_ids[0] * Q_BLOCK + Q_BLOCK + SEL_CHUNK - 1, SEL_CHUNK.bit_length() - 1)
    n0_max = (nqb // 2 * Q_BLOCK + SEL_CHUNK - 1) // SEL_CHUNK
    key_minus_query = (lax.broadcasted_iota(jnp.int32, (SEL_CHUNK, Q_BLOCK), 0)
                       - lax.broadcasted_iota(jnp.int32, (SEL_CHUNK, Q_BLOCK), 1))
    sel_tiles = []
    for slot in range(n_slots):
        last_of_block = slot == 0 or slot == n_slots - 1
        if slot == 0:
            which, kb = 0, n0 - 1
        elif slot == n_slots - 1:
            which, kb = 1, n_slots - n0 - 1
        elif slot >= n0_max:
            which, kb = 1, slot - n0
        else:
            which = (slot >= n0).astype(jnp.int32)
            kb = slot - 1 + which * (1 - n0)
        k0 = pl.multiple_of(kb * SEL_CHUNK, SEL_CHUNK)

        def scores(which=which, k0=k0):
            return _tile_scores(ksn[pl.ds(k0, SEL_CHUNK), :], qt_scr[which])

        def finish(s, which=which, kb=kb, k0=k0, last_of_block=last_of_block):
            bias = None
            if last_of_block:
                t0 = block_ids[which] * Q_BLOCK
                bias = jnp.where(key_minus_query <= t0 - k0, 0.0, NEG_BIG)
            vt_tile = jnp.concatenate([vst[kb * chunk_tiles + c] for c in range(chunk_tiles)], axis=1)
            return _tile_softmax(s, bias, vt_tile, shifted)

        sel_tiles.append((scores, finish))

    all_tiles += sel_tiles
    while len(results) < len(all_tiles):
        pump()
    for slot in range(n_slots):
        m, l, o = results[2 * n_win + slot]
        if not shifted:
            slot_m[slot] = m
        slot_l[slot] = l
        slot_o[slot] = o

    for idx in range(2):
        o_c, gate_t = per_block[idx]
        o_w = _merge_tiles([part + (None,) for part in results[idx * n_win:(idx + 1) * n_win]])
        slots = range(0, n0_max) if idx == 0 else range(1, n_slots)
        parts = []
        for slot in slots:
            if slot == 0 or slot >= n0_max:
                mine = None
            else:
                mine = (slot < n0) if idx == 0 else (slot >= n0)
            parts.append((None if shifted else slot_m[slot], slot_l[slot], slot_o[slot], mine))
        o_s = _merge_tiles(parts)
        row0 = pl.multiple_of(block_ids[idx] * Q_BLOCK, Q_BLOCK)
        for hh in range(NSA_HPG // 2):
            tiles = []
            for a in range(2):
                n = 2 * hh + a
                lanes = slice(n * Q_BLOCK, (n + 1) * Q_BLOCK)
                c0 = N_BRANCH * n
                tiles.append(gate_t[c0:c0 + 1, :] * o_c[:, lanes]
                             + gate_t[c0 + 1:c0 + 2, :] * o_s[:, lanes]
                             + gate_t[c0 + 2:c0 + 3, :] * o_w[:, lanes])
            o_ref[0, pl.ds(row0, Q_BLOCK), hh * LANES:(hh + 1) * LANES] = jnp.concatenate(tiles, axis=0).T


def _sel_map_t(seq):
    n_c = seq // CMP_STRIDE - 1
    n_s = seq // SEL_BLOCK
    tok = np.arange(seq)
    start = np.arange(n_c) * CMP_STRIDE
    cover_c = (tok[None, :] >= start[:, None]) & (tok[None, :] < start[:, None] + CMP_BLOCK)
    cover_s = (tok[:, None] // SEL_BLOCK) == np.arange(n_s)[None, :]
    m = cover_c.astype(np.float32) @ cover_s.astype(np.float32) / np.float32(CMP_BLOCK)
    out = np.zeros((n_s, n_c + 1), np.float32)
    out[:, :n_c] = m.T
    return jnp.asarray(out)


def _score_bounds(q_gain, k_gain):
    bound = 1.01 * HEAD_DIM ** 0.5 * LOG2_E * jnp.max(jnp.abs(q_gain)) * jnp.max(jnp.abs(k_gain), axis=-1)
    return jnp.broadcast_to(bound[:, None], (N_BRANCH, LANES)).astype(F32)


def _attention(q, gates, kvc, kv, qg_t, kg_ext, shift, shifted):
    b, s, _ = q.shape
    g = NSA_GROUPS
    nqb = s // Q_BLOCK
    ncmp = s // CMP_STRIDE
    nsel = s // SEL_BLOCK
    assert s % SEL_CHUNK == 0 and nsel % SUBLANES == 0 and nqb % 2 == 0
    gw = NSA_HPG * HEAD_DIM
    cols = NSA_HPG * Q_BLOCK
    n_slots = _sel_slot_count(nqb)

    def kv_spec(j):
        return pl.BlockSpec((1, 1, 1, s, LANES), lambda bi, gi, i: (j, bi, gi, 0, 0))

    return pl.pallas_call(
        functools.partial(_attn_kernel, shifted=shifted),
        grid=(b, g, nqb // 2),
        in_specs=[
            pl.BlockSpec((1, s, gw), lambda bi, gi, i: (bi, 0, gi)),
            pl.BlockSpec((1, 1, s, LANES), lambda bi, gi, i: (bi, gi, 0, 0)),
            pl.BlockSpec((1, 1, ncmp, LANES), lambda bi, gi, i: (bi, gi, 0, 0)),
            kv_spec(1), kv_spec(2),
            pl.BlockSpec((nsel, ncmp), lambda bi, gi, i: (0, 0)),
            pl.BlockSpec((HEAD_DIM, Q_BLOCK), lambda bi, gi, i: (0, 0)),
            pl.BlockSpec((N_BRANCH, LANES), lambda bi, gi, i: (0, 0)),
            pl.BlockSpec((N_BRANCH, LANES), lambda bi, gi, i: (0, 0)),
        ],
        out_specs=pl.BlockSpec((1, s, gw), lambda bi, gi, i: (bi, 0, gi)),
        out_shape=jax.ShapeDtypeStruct((b, s, Q_DIM), F32),
        scratch_shapes=[
            pltpu.VMEM((s, LANES), BF16),
            pltpu.VMEM((WINDOW + s, LANES), BF16),
            pltpu.VMEM((s // LANES, VT_ROWS, LANES), BF16),
            pltpu.VMEM(((WINDOW + s) // LANES, VT_ROWS, LANES), BF16),
            pltpu.VMEM((2, LANES, cols), BF16),
            pltpu.VMEM((n_slots, 1, cols), F32),
            pltpu.VMEM((n_slots, 1, cols), F32),
            pltpu.VMEM((n_slots, HEAD_DIM, cols), F32),
        ],
        compiler_params=_cparams(3),
        name="nsa_attention",
    )(q, gates, kvc, kv, kv, _sel_map_t(s), qg_t, kg_ext, shift)


def _ffn_core(x, g_ref, sc_ref, sh_ref, gate_ref, wg_ref, wu_ref, wo_ref):
    hh = _norm_mod(x, g_ref[...], sc_ref[0], sh_ref[0]).astype(BF16)
    a = jnp.dot(hh, wg_ref[...], preferred_element_type=F32)
    u = jnp.dot(hh, wu_ref[...], preferred_element_type=F32)
    act = (jax.nn.silu(a) * u).astype(BF16)
    return x + gate_ref[0] * jnp.dot(act, wo_ref[...], preferred_element_type=F32)


def _ffn_kernel(x_ref, g_ref, sc_ref, sh_ref, gate_ref, wg_ref, wu_ref, wo_ref, o_ref):
    o_ref[0] = _ffn_core(x_ref[0], g_ref, sc_ref, sh_ref, gate_ref, wg_ref, wu_ref, wo_ref)


def _mix_ffn_kernel(y_ref, wmix_ref, mix_gate_ref, x_ref, g_ref, sc_ref, sh_ref, gate_ref, wg_ref, wu_ref, wo_ref,
                    o_ref):
    mix = jnp.dot(y_ref[0].astype(BF16), wmix_ref[...], preferred_element_type=F32)
    x = x_ref[0] + mix_gate_ref[0] * mix
    o_ref[0] = _ffn_core(x, g_ref, sc_ref, sh_ref, gate_ref, wg_ref, wu_ref, wo_ref)


def _ffn(x, gain, sc, sh, gate, w_in, w_out, layer, mixer_out=None):
    b, s, d = x.shape
    hidden = w_out.shape[1]
    tm = min(ROW_TILE, s)
    vec = pl.BlockSpec((1, 1, d), lambda bi, i: (bi, 0, 0))
    rows = lambda width: pl.BlockSpec((1, tm, width), lambda bi, i: (bi, i, 0))
    in_specs = [
        rows(d), _resident((1, d)), vec, vec, vec,
        _layer_plane((d, hidden), layer, 0), _layer_plane((d, hidden), layer, 1),
        _layer_plane((hidden, d), layer),
    ]
    args = (x, gain, sc, sh, gate, w_in, w_in, w_out)
    body = _ffn_kernel
    if mixer_out is not None:
        y, w_mix, j, mix_gate = mixer_out
        in_specs = [rows(y.shape[-1]), _layer_plane(w_mix.shape[1:], j), vec] + in_specs
        args = (y, w_mix, mix_gate) + args
        body = _mix_ffn_kernel
    return pl.pallas_call(
        body,
        grid=(b, s // tm),
        in_specs=in_specs,
        out_specs=rows(d),
        out_shape=jax.ShapeDtypeStruct((b, s, d), F32),
        compiler_params=_cparams(2),
        name="ffn",
    )(*args)


def _band_starts(width, block_w):
    band = 2 * MXU_WIDTH
    starts = []
    for c0 in range(0, width, MXU_WIDTH):
        c1 = min(c0 + MXU_WIDTH, width)
        lo = (c0 // block_w) * block_w
        hi = ((c1 - 1) // block_w + 1) * block_w
        k0 = min((lo // LANES) * LANES, width - band)
        assert k0 >= 0 and k0 <= lo and hi <= k0 + band
        starts.append(k0)
    return starts


def _band_weights(w_a, w_x):
    nb, bw, _ = w_a.shape
    width = nb * bw
    spread = jnp.asarray(np.tile(np.eye(bw, dtype=np.float32), (1, nb)), dtype=BF16)
    blk = np.arange(width) // bw
    on_diag = jnp.asarray(blk[:, None] == blk[None, :])
    dense = lambda w: jnp.where(
        on_diag, jnp.dot(w.reshape(width, bw).astype(BF16), spread, preferred_element_type=BF16), 0)
    da, dx = dense(w_a), dense(w_x)
    band = 2 * MXU_WIDTH
    tiles = []
    for t, k0 in enumerate(_band_starts(width, bw)):
        c0 = t * MXU_WIDTH
        c1 = min(c0 + MXU_WIDTH, width)
        pad = ((0, 0), (0, MXU_WIDTH - (c1 - c0)))
        tiles.append(jnp.concatenate(
            [jnp.pad(da[k0:k0 + band, c0:c1], pad), jnp.pad(dx[k0:k0 + band, c0:c1], pad)], axis=1))
    return 0.5 * jnp.stack(tiles)


def _rg_mixer_kernel(x_ref, xn_ref, g_ref, sc_ref, sh_ref, res_gate_ref, win_ref, cw_ref, cb_ref, wb_ref, ba_ref,
                     bx_ref, lam_ref, wout_ref, o_ref,
                     u_a, u_b, gate_a, gate_b, h_scr, a_scr, b_scr, y_scr, *, band_starts):
    step = pl.program_id(0)
    nb, ts, d = x_ref.shape
    width = wout_ref.shape[0]
    rows = nb * ts
    halo = CONV_WIDTH - 1
    band = 2 * MXU_WIDTH

    pieces = [(c0, min(c0 + MXU_WIDTH, 2 * width)) for c0 in range(0, 2 * width, MXU_WIDTH)]

    def project_piece(hh, piece, u_dst, gate_dst):
        c0, c1 = pieces[piece]
        res = jnp.dot(hh, win_ref[:, c0:c1], preferred_element_type=F32)
        n_gate = max(0, min(c1, width) - c0)
        if n_gate:
            gate_dst[:, c0:c0 + n_gate] = res[:, :n_gate]
        if c1 > width:
            u0 = max(c0, width) - width
            u_dst[:, :, u0:c1 - width] = jnp.swapaxes(res[:, n_gate:].reshape(nb, ts, c1 - c0 - n_gate), 0, 1)

    def normed(tile_ref):
        return _norm_mod(tile_ref[...], g_ref[...], sc_ref[...], sh_ref[...]).reshape(rows, d).astype(BF16)

    @pl.when(step == 0)
    def _init():
        u_b[...] = jnp.zeros((ts, nb, width), F32)
        h_scr[...] = jnp.zeros(h_scr.shape, F32)
        hh = normed(x_ref)
        for piece in range(len(pieces)):
            project_piece(hh, piece, u_a, gate_a)

    def tile_step(u_cur, gate_cur, u_nxt, gate_nxt):
        prev_tail = u_nxt[ts - halo:ts]
        hh_next = normed(xn_ref)
        todo = list(range(len(pieces)))
        per_stage = -(-len(pieces) // len(band_starts))

        uc = jnp.zeros((ts, nb, width), F32) + cb_ref[...]
        for k in range(CONV_WIDTH):
            shifted = u_cur[...] if k == halo else jnp.concatenate([prev_tail[k:], u_cur[0:ts - halo + k]], axis=0)
            uc = uc + cw_ref[k:k + 1, :] * shifted
        uc = uc.reshape(rows, width)
        ub = uc.astype(BF16)

        z = -lam_ref[...]
        log_a_unit = -RG_C * (jnp.maximum(z, 0.0) + jnp.log1p(jnp.exp(-jnp.abs(z))))

        for t, k0 in enumerate(band_starts):
            c0 = t * MXU_WIDTH
            c1 = min(c0 + MXU_WIDTH, width)
            n = c1 - c0
            zz = jnp.dot(ub[:, k0:k0 + band], wb_ref[t], preferred_element_type=F32)
            for piece in todo[:per_stage]:
                project_piece(hh_next, piece, u_nxt, gate_nxt)
            todo = todo[per_stage:]
            half_unit = 0.5 * log_a_unit[:, c0:c1]
            half_u = 0.5 * uc[:, c0:c1]
            log_a = half_unit * jnp.tanh(zz[:, :n] + ba_ref[:, c0:c1]) + half_unit
            gated_u = half_u * jnp.tanh(zz[:, MXU_WIDTH:MXU_WIDTH + n] + bx_ref[:, c0:c1]) + half_u
            a = jnp.exp(log_a)
            bb = jnp.sqrt(-jnp.tanh(log_a) * (a * a + 1.0)) * gated_u
            for ct in range(n // LANES):
                lanes = slice(ct * LANES, (ct + 1) * LANES)
                a_scr[c0 // LANES + ct] = a[:, lanes]
                b_scr[c0 // LANES + ct] = bb[:, lanes]

        n_ct = width // LANES

        def scan_step(t, h):
            rows_t = pl.ds(pl.multiple_of(t * nb, nb), nb)
            new = []
            for ct in range(n_ct):
                hc = a_scr[ct, rows_t, :] * h[ct] + b_scr[ct, rows_t, :]
                b_scr[ct, rows_t, :] = hc
                new.append(hc)
            return tuple(new)

        h_fin = lax.fori_loop(0, ts, scan_step, tuple(h_scr[ct] for ct in range(n_ct)), unroll=SUBLANES)
        for ct in range(n_ct):
            lanes = slice(ct * LANES, (ct + 1) * LANES)
            h_scr[ct] = h_fin[ct]
            hs = jnp.swapaxes(b_scr[ct].reshape(ts, nb, LANES), 0, 1)
            y_scr[:, lanes] = (hs.reshape(rows, LANES) * jax.nn.gelu(gate_cur[:, lanes])).astype(BF16)
        mix = jnp.dot(y_scr[...], wout_ref[...], preferred_element_type=F32)
        o_ref[...] = x_ref[...] + res_gate_ref[...] * mix.reshape(nb, ts, d)

    @pl.when(step % 2 == 0)
    def _even():
        tile_step(u_a, gate_a, u_b, gate_b)

    @pl.when(step % 2 == 1)
    def _odd():
        tile_step(u_b, gate_b, u_a, gate_a)


def _rg_mixer(x, gain, sc, sh, res_gate, w_in, conv_w, conv_b, w_band, b_a, b_x, lam, w_out, j, block_w):
    b, s, d = x.shape
    width = w_out.shape[1]
    assert b == SUBLANES
    ts = min(RG_TIME_TILE, s)
    rows = b * ts
    band_starts = tuple(_band_starts(width, block_w))
    n_tiles = s // ts
    tile = pl.BlockSpec((b, ts, d), lambda t: (0, t, 0))
    next_tile = pl.BlockSpec((b, ts, d), lambda t: (0, jnp.minimum(t + 1, n_tiles - 1), 0))
    mod = _resident((b, 1, d))
    vec = _resident((1, width))
    kern = functools.partial(_rg_mixer_kernel, band_starts=band_starts)
    return pl.pallas_call(
        kern,
        grid=(n_tiles,),
        in_specs=[
            tile, next_tile, _resident((1, d)), mod, mod, mod,
            _layer_plane(w_in.shape[1:], j),
            _resident((CONV_WIDTH, width)), vec,
            _resident(w_band.shape),
            vec, vec, vec,
            _layer_plane(w_out.shape[1:], j),
        ],
        out_specs=tile,
        out_shape=jax.ShapeDtypeStruct((b, s, d), F32),
        scratch_shapes=[
            pltpu.VMEM((ts, b, width), F32),
            pltpu.VMEM((ts, b, width), F32),
            pltpu.VMEM((rows, width), F32),
            pltpu.VMEM((rows, width), F32),
            pltpu.VMEM((width // LANES, b, LANES), F32),
            pltpu.VMEM((width // LANES, rows, LANES), F32),
            pltpu.VMEM((width // LANES, rows, LANES), F32),
            pltpu.VMEM((rows, width), BF16),
        ],
        compiler_params=_cparams(1),
        name="rg_mixer",
    )(x, x, gain, sc, sh, res_gate, w_in, conv_w, conv_b, w_band, b_a, b_x, lam, w_out)


def kernel(x, c, ada_w, ada_b, norm1_g, norm2_g, nsa_w_in, nsa_w_out, nsa_cmp_pos, nsa_cmp_w1, nsa_cmp_w2, nsa_q_gain, nsa_k_gain, rg_w_in, rg_conv_w, rg_conv_b, rg_w_a, rg_b_a, rg_w_x, rg_b_x, rg_lam, rg_w_out, ffn_w_in, ffn_w_out):
    depth, d, _ = ada_w.shape
    b = x.shape[0]
    mod = _ada_mod(c, ada_w, ada_b).reshape(depth, b, 6, 1, d)
    ffn_w_in, ffn_w_out, nsa_w_out, rg_w_in, rg_w_out = (
        w.astype(BF16) for w in (ffn_w_in, ffn_w_out, nsa_w_out, rg_w_in, rg_w_out))
    for layer in range(depth):
        sh1, sc1, g1, sh2, sc2, g2 = (mod[layer, :, k] for k in range(6))
        n1 = norm1_g[layer].reshape(1, d)
        n2 = norm2_g[layer].reshape(1, d)
        j = layer // N_MIXERS
        if layer % N_MIXERS == 0:
            kg_ext = jnp.concatenate([nsa_k_gain[j], jnp.zeros_like(nsa_k_gain[j])], axis=1)
            qg_t = jnp.broadcast_to(nsa_q_gain[j][:, None], (HEAD_DIM, Q_BLOCK))
            q, kv, gates = _nsa_proj(x, n1, sc1, sh1, _nsa_weight_layout(nsa_w_in[j]))
            kvc = _compress(kv, *_compress_weights(nsa_cmp_pos[j], nsa_cmp_w1[j], nsa_cmp_w2[j]), kg_ext)
            bounds = _score_bounds(nsa_q_gain[j], nsa_k_gain[j])
            o = lax.cond(
                2.0 * jnp.max(bounds) <= MAX_SHIFTED_SCORE_RANGE,
                lambda *a: _attention(*a, bounds, True),
                lambda *a: _attention(*a, jnp.zeros_like(bounds), False),
                q, gates, kvc, kv, qg_t, kg_ext)
            mixer_out = (o, nsa_w_out, j, g1)
        else:
            width = rg_w_out.shape[1]
            block_w = rg_w_a.shape[-1]
            x = _rg_mixer(x, n1, sc1, sh1, g1, rg_w_in,
                          rg_conv_w[j], rg_conv_b[j].reshape(1, width),
                          _band_weights(rg_w_a[j], rg_w_x[j]),
                          0.5 * rg_b_a[j].reshape(1, width), 0.5 * rg_b_x[j].reshape(1, width),
                          rg_lam[j].reshape(1, width), rg_w_out, j, block_w)
            mixer_out = None
        x = _ffn(x, n2, sc2, sh2, g2, ffn_w_in, ffn_w_out, layer, mixer_out)
    return x
```

```python
import functools

import numpy as np
import jax
import jax.numpy as jnp
from jax import lax
from jax.experimental import pallas as pl
from jax.experimental.pallas import tpu as pltpu

F32 = jnp.float32
BF16 = jnp.bfloat16

EPS = 1e-6
N_MIXERS = 2

NSA_HEADS = 16
NSA_GROUPS = 4
NSA_HPG = NSA_HEADS // NSA_GROUPS
HEAD_DIM = 64
CMP_STRIDE = 16
CMP_BLOCK = 2 * CMP_STRIDE
SEL_BLOCK = 64
SEL_TOP_N = 8
WINDOW = 512
Q_BLOCK = 128
N_BRANCH = 3
Q_DIM = NSA_HEADS * HEAD_DIM
KV_DIM = NSA_GROUPS * HEAD_DIM
GATE_COLS = N_BRANCH * NSA_HEADS
GATE_PER_GROUP = N_BRANCH * NSA_HPG

CONV_WIDTH = 4
RG_C = 8.0

LANES = 128
SUBLANES = 8
MXU_WIDTH = 256
VMEM_LIMIT_BYTES = 56 * 1024 * 1024

NEG_BIG = -1e30
LOG2_E = 1.4426950408889634
BLOCK_LANE0 = HEAD_DIM
PAD_LANE = LANES - 1
SHIFT_LANE = LANES - 2
MAX_SHIFTED_SCORE_RANGE = 100.0
VT_ROWS = HEAD_DIM + 16
SEL_CHUNK = 256
SCORE_LOOKAHEAD = 3
RG_TIME_TILE = 64
ROW_TILE = 512


def _cparams(n_axes):
    return pltpu.CompilerParams(
        dimension_semantics=("arbitrary",) * n_axes,
        vmem_limit_bytes=VMEM_LIMIT_BYTES,
    )


def _resident(shape):
    return pl.BlockSpec(shape, lambda *_: (0,) * len(shape), pipeline_mode=pl.Buffered(1))


def _layer_plane(shape, layer, col_block=0):
    return pl.BlockSpec((None,) + tuple(shape), lambda *_: (layer, 0, col_block), pipeline_mode=pl.Buffered(1))


def _norm_mod(x, gain, scale, shift):
    ms = jnp.mean(x * x, axis=-1, keepdims=True)
    y = x * lax.rsqrt(ms + EPS) * gain
    return y * (1.0 + scale) + shift


def _ada_kernel(c_ref, w_ref, b_ref, o_ref):
    cond = jax.nn.silu(c_ref[...])
    w = w_ref[0].astype(BF16)
    hi = cond.astype(BF16)
    rest = cond - hi.astype(F32)
    mid = rest.astype(BF16)
    lo = (rest - mid.astype(F32)).astype(BF16)
    acc = (jnp.dot(hi, w, preferred_element_type=F32) + jnp.dot(mid, w, preferred_element_type=F32)
           + jnp.dot(lo, w, preferred_element_type=F32))
    o_ref[0] = acc + b_ref[0]


def _ada_mod(c, ada_w, ada_b):
    depth, d, n = ada_w.shape
    b = c.shape[0]
    tn = n // 4
    return pl.pallas_call(
        _ada_kernel,
        grid=(depth, n // tn),
        in_specs=[
            pl.BlockSpec((b, d), lambda l, j: (0, 0)),
            pl.BlockSpec((1, d, tn), lambda l, j: (l, 0, j)),
            pl.BlockSpec((1, 1, tn), lambda l, j: (l, 0, j)),
        ],
        out_specs=pl.BlockSpec((1, b, tn), lambda l, j: (l, 0, j)),
        out_shape=jax.ShapeDtypeStruct((depth, b, n), F32),
        compiler_params=_cparams(2),
        name="ada_mod",
    )(c, ada_w, ada_b.reshape(depth, 1, n))


def _nsa_weight_layout(w_in):
    w_in = w_in.astype(BF16)
    src = np.arange(6 * KV_DIM).reshape(N_BRANCH, 2, NSA_GROUPS, HEAD_DIM).transpose(0, 2, 1, 3).reshape(-1)
    perm = np.zeros((6 * KV_DIM, 6 * KV_DIM), np.float32)
    perm[src, np.arange(6 * KV_DIM)] = 1.0
    kv = jnp.dot(w_in[:, Q_DIM:Q_DIM + 6 * KV_DIM], jnp.asarray(perm, dtype=BF16), preferred_element_type=BF16)
    gates = jnp.pad(w_in[:, Q_DIM + 6 * KV_DIM:], ((0, 0), (0, LANES - GATE_COLS)))
    return jnp.concatenate([w_in[:, :Q_DIM], kv, gates], axis=1)


def _nsa_proj_kernel(x_ref, g_ref, sc_ref, sh_ref, w_ref, q_ref, kv_ref, gate_ref):
    hh = _norm_mod(x_ref[0], g_ref[...], sc_ref[0], sh_ref[0])
    p = jnp.dot(hh.astype(BF16), w_ref[...], preferred_element_type=F32)
    q_ref[0] = p[:, :Q_DIM]
    for j in range(N_BRANCH):
        for g in range(NSA_GROUPS):
            lo = Q_DIM + (j * NSA_GROUPS + g) * LANES
            kv_ref[j, 0, g] = p[:, lo:lo + LANES]
    gate = jax.nn.sigmoid(p[:, Q_DIM + 6 * KV_DIM:])
    for g in range(NSA_GROUPS):
        shift = (LANES - GATE_PER_GROUP * g) % LANES
        gate_ref[0, g] = gate if shift == 0 else pltpu.roll(gate, shift, 1)


def _nsa_proj(x, gain, sc, sh, w):
    b, s, d = x.shape
    tm = min(ROW_TILE, s)
    n = w.shape[1]
    vec = pl.BlockSpec((1, 1, d), lambda bi, i: (bi, 0, 0))
    return pl.pallas_call(
        _nsa_proj_kernel,
        grid=(b, s // tm),
        in_specs=[
            pl.BlockSpec((1, tm, d), lambda bi, i: (bi, i, 0)),
            pl.BlockSpec((1, d), lambda bi, i: (0, 0)),
            vec, vec,
            pl.BlockSpec((d, n), lambda bi, i: (0, 0)),
        ],
        out_specs=[
            pl.BlockSpec((1, tm, Q_DIM), lambda bi, i: (bi, i, 0)),
            pl.BlockSpec((N_BRANCH, 1, NSA_GROUPS, tm, LANES), lambda bi, i: (0, bi, 0, i, 0)),
            pl.BlockSpec((1, NSA_GROUPS, tm, LANES), lambda bi, i: (bi, 0, i, 0)),
        ],
        out_shape=[
            jax.ShapeDtypeStruct((b, s, Q_DIM), F32),
            jax.ShapeDtypeStruct((N_BRANCH, b, NSA_GROUPS, s, LANES), F32),
            jax.ShapeDtypeStruct((b, NSA_GROUPS, s, LANES), F32),
        ],
        compiler_params=_cparams(2),
        name="nsa_proj",
    )(x, gain, sc, sh, w)


def _k_lane_norm(t, gain_ext):
    lane = lax.broadcasted_iota(jnp.int32, t.shape, 1)
    ms = jnp.sum(jnp.where(lane < HEAD_DIM, t * t, 0.0), axis=-1, keepdims=True) * (1.0 / HEAD_DIM)
    return t * lax.rsqrt(ms + EPS) * gain_ext


def _compress_weights(pos, w1, w2):
    hidden = w1.shape[-1]
    w1p = w1.reshape(2, 2, CMP_STRIDE, HEAD_DIM, hidden)
    z = jnp.zeros_like(w1p[0])
    wk = jnp.concatenate([w1p[0], z], axis=-1)
    wv = jnp.concatenate([z, w1p[1]], axis=-1)
    w1_blk = jnp.concatenate([wk, wv], axis=2).astype(BF16)
    w1_blk = w1_blk.reshape(2, CMP_STRIDE * LANES, 2 * hidden)
    pos_blk = jnp.concatenate([pos[0], pos[1]], axis=-1).reshape(2, CMP_STRIDE, 1, LANES)
    z2 = jnp.zeros_like(w2[0])
    w2_blk = jnp.concatenate([jnp.concatenate([w2[0], z2], axis=1),
                              jnp.concatenate([z2, w2[1]], axis=1)], axis=0).astype(BF16)
    return pos_blk, w1_blk, w2_blk


def _compress_kernel(kv_ref, pos_ref, w1_ref, w2_ref, kg_ref, o_ref):
    g, seq = kv_ref.shape[2:4]
    nch = seq // CMP_STRIDE
    rows = g * nch
    toks = [jnp.concatenate([kv_ref[0, 0, gi, pl.ds(p, nch, stride=CMP_STRIDE), :] for gi in range(g)], axis=0)
            for p in range(CMP_STRIDE)]
    halves = []
    for half in range(2):
        chunk = jnp.concatenate([(toks[p] + pos_ref[half, p]).astype(BF16) for p in range(CMP_STRIDE)], axis=1)
        halves.append(jnp.dot(chunk, w1_ref[half], preferred_element_type=F32))
    first, second = halves
    h1 = first + pltpu.roll(second, rows - 1, 0)
    out = jnp.dot(jax.nn.silu(h1).astype(BF16), w2_ref[...], preferred_element_type=F32)
    lane = lax.broadcasted_iota(jnp.int32, out.shape, 1)
    out = jnp.where(lane < HEAD_DIM, _k_lane_norm(out, kg_ref[0:1]), out)
    r = lax.broadcasted_iota(jnp.int32, out.shape, 0)
    out = jnp.where((r & (nch - 1)) == nch - 1, 0.0, out)
    o_ref[0] = out.reshape(g, nch, LANES)


def _compress(kv, pos_blk, w1_blk, w2_blk, kg_ext):
    _, b, g, s, _ = kv.shape
    nch = s // CMP_STRIDE
    assert nch & (nch - 1) == 0
    return pl.pallas_call(
        _compress_kernel,
        grid=(b,),
        in_specs=[
            pl.BlockSpec((1, 1, g, s, LANES), lambda bi: (0, bi, 0, 0, 0)),
            pl.BlockSpec(pos_blk.shape, lambda bi: (0, 0, 0, 0)),
            pl.BlockSpec(w1_blk.shape, lambda bi: (0, 0, 0)),
            pl.BlockSpec(w2_blk.shape, lambda bi: (0, 0)),
            pl.BlockSpec((N_BRANCH, LANES), lambda bi: (0, 0)),
        ],
        out_specs=pl.BlockSpec((1, g, nch, LANES), lambda bi: (bi, 0, 0, 0)),
        out_shape=jax.ShapeDtypeStruct((b, g, nch, LANES), F32),
        compiler_params=_cparams(1),
        name="nsa_compress",
    )(kv, pos_blk, w1_blk, w2_blk, kg_ext)


def _sel_slot_count(nqb):
    chunks = lambda i: -(-(i + 1) * Q_BLOCK // SEL_CHUNK)
    counts = {chunks(p) + chunks(nqb - 1 - p) for p in range(nqb // 2)}
    assert len(counts) == 1, counts
    return counts.pop()


def _tile_scores(k_tile, q_t):
    return jnp.concatenate([jnp.dot(k_tile, q_t[:, h:h + MXU_WIDTH], preferred_element_type=F32)
                            for h in range(0, q_t.shape[1], MXU_WIDTH)], axis=1)


def _tile_softmax(s, bias, vt_tile, shifted):
    if bias is not None:
        rows = bias.shape[0]
        biased = s[:rows] + jnp.concatenate([bias] * (s.shape[1] // Q_BLOCK), axis=1)
        s = biased if rows == s.shape[0] else jnp.concatenate([biased, s[rows:]], axis=0)
    m = None if shifted else jnp.max(s, axis=0, keepdims=True)
    p = jnp.exp2(s if shifted else s - m)
    ol = jnp.dot(vt_tile, p.astype(BF16), preferred_element_type=F32)
    return m, ol[HEAD_DIM:HEAD_DIM + 1], ol[:HEAD_DIM]


def _merge_tiles(parts):
    if parts[0][0] is None:
        weights = [None if mine is None else jnp.where(mine, 1.0, 0.0) for _, _, _, mine in parts]
    else:
        ms = [mi if mine is None else jnp.where(mine, mi, NEG_BIG) for mi, _, _, mine in parts]
        m = functools.reduce(jnp.maximum, ms)
        weights = [jnp.exp2(mi - m) for mi in ms]
    l = None
    o = None
    for w, (_, li, oi, _) in zip(weights, parts):
        if w is not None:
            li, oi = w * li, w * oi
        l = li if l is None else l + li
        o = oi if o is None else o + oi
    return o / l


def _attn_kernel(q_ref, gate_ref, kvc_ref, ks_ref, kw_ref, selmap_ref, qg_ref, kg_ref, shift_ref, o_ref,
                 ksn, kwn, vst, vwt, qt_scr, slot_m, slot_l, slot_o, *, shifted):
    pair = pl.program_id(2)
    seq = ks_ref.shape[3]
    ncmp = kvc_ref.shape[2]
    nsel = selmap_ref.shape[0]
    nqb = seq // Q_BLOCK
    cols = NSA_HPG * Q_BLOCK
    win_tiles = WINDOW // LANES
    chunk_tiles = SEL_CHUNK // LANES
    scale = HEAD_DIM ** -0.5 * LOG2_E

    @pl.when(pair == 0)
    def _prepare_kv():
        lane = lax.broadcasted_iota(jnp.int32, (LANES, LANES), 1)
        row = lax.broadcasted_iota(jnp.int32, (LANES, LANES), 0)
        vrow = lax.broadcasted_iota(jnp.int32, (VT_ROWS, LANES), 0)
        vlane = lax.broadcasted_iota(jnp.int32, (VT_ROWS, LANES), 1)
        pick_v = ((vlane == vrow + HEAD_DIM) & (vrow < HEAD_DIM)).astype(BF16)
        ones_row = (vrow == HEAD_DIM).astype(F32)
        v_plane = lambda tile: (lax.dot_general(pick_v, tile.astype(BF16), (((1,), (1,)), ((), ())),
                                                preferred_element_type=F32) + ones_row).astype(BF16)
        pad_mark = (lane == PAD_LANE).astype(BF16)
        for w in range(win_tiles):
            kwn[w * LANES:(w + 1) * LANES, :] = pad_mark
        vwt[0:win_tiles] = jnp.zeros((win_tiles, VT_ROWS, LANES), BF16)

        def body(c, carry):
            r0 = pl.multiple_of(c * LANES, LANES)
            ts = ks_ref[0, 0, 0, pl.ds(r0, LANES), :]
            in_block = (lane - BLOCK_LANE0 == lax.shift_right_logical(r0 + row, SEL_BLOCK.bit_length() - 1))
            marks = (in_block | (lane == SHIFT_LANE)).astype(F32)
            ksn[pl.ds(r0, LANES), :] = (_k_lane_norm(ts, kg_ref[1:2]) + marks).astype(BF16)
            vst[c] = v_plane(ts)
            tw = kw_ref[0, 0, 0, pl.ds(r0, LANES), :]
            kwn[pl.ds(WINDOW + r0, LANES), :] = (
                _k_lane_norm(tw, kg_ref[2:3]) + (lane == SHIFT_LANE).astype(F32)).astype(BF16)
            vwt[win_tiles + c] = v_plane(tw)
            return carry

        lax.fori_loop(0, seq // LANES, body, 0, unroll=4)

    kc = kvc_ref[0, 0]
    kc_b = kc.astype(BF16)
    vc_t = kc.T[HEAD_DIM:, :].astype(BF16)
    block_ids = (pair, nqb - 1 - pair)

    def block_branches(idx):
        i = block_ids[idx]
        t0 = i * Q_BLOCK
        row0 = pl.multiple_of(t0, Q_BLOCK)

        q = q_ref[0, pl.ds(row0, Q_BLOCK), :]
        heads = []
        for h in range(NSA_HPG // 2):
            qt = q[:, h * LANES:(h + 1) * LANES].T
            for a in range(2):
                x = qt[a * HEAD_DIM:(a + 1) * HEAD_DIM]
                ms = jnp.mean(x * x, axis=0, keepdims=True)
                heads.append(x * lax.rsqrt(ms + EPS) * qg_ref[...] * scale)
        q_top = jnp.concatenate(heads, axis=1).astype(BF16)
        mask_rows = lax.broadcasted_iota(jnp.int32, (LANES - HEAD_DIM, cols), 0) + HEAD_DIM
        q_plain = jnp.concatenate([q_top, jnp.zeros((LANES - HEAD_DIM, cols), BF16)], axis=0)
        shift_rows = lambda br: jnp.where(
            mask_rows == SHIFT_LANE, -jnp.concatenate([shift_ref[br:br + 1, :]] * NSA_HPG, axis=1), 0.0)
        q_win = jnp.concatenate(
            [q_top, jnp.where(mask_rows == PAD_LANE, NEG_BIG, shift_rows(2)).astype(BF16)], axis=0)

        tiles = []
        off = 0
        span = WINDOW + Q_BLOCK
        while off < span:
            kc_w = min(SEL_CHUNK, span - off)
            if off < Q_BLOCK:
                rows_b = min(kc_w, Q_BLOCK - off)
                newest = False
            elif off + kc_w > WINDOW:
                assert off >= WINDOW
                rows_b = kc_w
                newest = True
            else:
                rows_b = 0
                newest = False

            def scores(off=off, kc_w=kc_w):
                k_tile = kwn[pl.ds(pl.multiple_of(t0 + off, LANES), kc_w), :]
                return _tile_scores(k_tile, q_win)

            def finish(s, off=off, kc_w=kc_w, rows_b=rows_b, newest=newest):
                bias = None
                if rows_b:
                    r = off + lax.broadcasted_iota(jnp.int32, (rows_b, Q_BLOCK), 0)
                    c = lax.broadcasted_iota(jnp.int32, (rows_b, Q_BLOCK), 1)
                    bias = jnp.where((r - WINDOW <= c) if newest else (r > c), 0.0, NEG_BIG)
                vt_tile = jnp.concatenate([vwt[i + off // LANES + cc] for cc in range(kc_w // LANES)], axis=1)
                return _tile_softmax(s, bias, vt_tile, shifted)

            tiles.append((scores, finish))
            off += kc_w
        yield tiles

        s = _tile_scores(kc_b, q_plain)
        yield
        tq = t0 + (lax.broadcasted_iota(jnp.int32, (ncmp, cols), 1) & (Q_BLOCK - 1))
        cmp_end = lax.broadcasted_iota(jnp.int32, (ncmp, cols), 0) * CMP_STRIDE + (CMP_BLOCK - 1)
        visible = cmp_end <= tq
        s = jnp.where(visible, s, NEG_BIG)
        m = jnp.max(s, axis=0, keepdims=True)
        p = jnp.where(visible, jnp.exp2(s - m), 0.0)
        l = jnp.sum(p, axis=0, keepdims=True)
        p_c = p / jnp.maximum(l, jnp.finfo(F32).tiny)
        o_c = jnp.dot(vc_t, p_c.astype(BF16), preferred_element_type=F32)

        yield
        p_sum = p_c[:, 0:Q_BLOCK]
        for n in range(1, NSA_HPG):
            p_sum = p_sum + p_c[:, n * Q_BLOCK:(n + 1) * Q_BLOCK]
        sel_map = selmap_ref[...].astype(BF16)
        hi = p_sum.astype(BF16)
        rest = p_sum - hi.astype(F32)
        mid = rest.astype(BF16)
        lo = (rest - mid.astype(F32)).astype(BF16)
        imp = (jnp.dot(sel_map, hi, preferred_element_type=F32) + jnp.dot(sel_map, mid, preferred_element_type=F32)
               + jnp.dot(sel_map, lo, preferred_element_type=F32))
        blk = lax.broadcasted_iota(jnp.int32, (nsel, Q_BLOCK), 0)
        cur = lax.shift_right_logical(t0 + lax.broadcasted_iota(jnp.int32, (nsel, Q_BLOCK), 1), 6)
        valid = blk <= cur
        forced = (blk == 0) | (blk == cur) | (blk == cur - 1)
        val = jnp.where(valid, imp, -jnp.inf)
        val = jnp.where(forced, jnp.inf, val)
        rank = jnp.zeros((nsel, Q_BLOCK), jnp.int32)
        for j2 in range(nsel):
            row = val[j2:j2 + 1, :]
            beats = (row > val) | ((row == val) & (blk > j2))
            rank = rank + beats.astype(jnp.int32)
            yield
        picked = jnp.where((rank < SEL_TOP_N) & valid, 0.0, NEG_BIG).astype(BF16)
        qt_scr[idx] = jnp.concatenate(
            [q_top, jnp.concatenate([picked] * NSA_HPG, axis=1),
             shift_rows(1)[nsel:].astype(BF16)], axis=0)
        gate_t = gate_ref[0, 0, pl.ds(row0, Q_BLOCK), :].T
        return o_c, gate_t

    all_tiles = []
    pending = []
    results = []

    def pump():
        while len(results) + len(pending) < min(len(all_tiles), len(results) + SCORE_LOOKAHEAD + 1):
            pending.append(all_tiles[len(results) + len(pending)][0]())
        results.append(all_tiles[len(results)][1](pending.pop(0)))

    per_block = [None, None]
    running = {idx: block_branches(idx) for idx in range(2)}
    for idx in range(2):
        all_tiles += next(running[idx])
    n_win = len(all_tiles) // 2
    while running:
        for idx, stages in list(running.items()):
            try:
                next(stages)
            except StopIteration as done:
                per_block[idx] = done.value
                del running[idx]

    n_slots = _sel_slot_count(nqb)
    n0 = lax.shift_right_logical(block_ids[0] * Q_BLOCK + Q_BLOCK + SEL_CHUNK - 1, SEL_CHUNK.bit_length() - 1)
    n0_max = (nqb // 2 * Q_BLOCK + SEL_CHUNK - 1) // SEL_CHUNK
    key_minus_query = (lax.broadcasted_iota(jnp.int32, (SEL_CHUNK, Q_BLOCK), 0)
                       - lax.broadcasted_iota(jnp.int32, (SEL_CHUNK, Q_BLOCK), 1))
    sel_tiles = []
    for slot in range(n_slots):
        last_of_block = slot == 0 or slot == n_slots - 1
        if slot == 0:
            which, kb = 0, n0 - 1
        elif slot == n_slots - 1:
            which, kb = 1, n_slots - n0 - 1
        elif slot >= n0_max:
            which, kb = 1, slot - n0
        else:
            which = (slot >= n0).astype(jnp.int32)
            kb = slot - 1 + which * (1 - n0)
        k0 = pl.multiple_of(kb * SEL_CHUNK, SEL_CHUNK)

        def scores(which=which, k0=k0):
            return _tile_scores(ksn[pl.ds(k0, SEL_CHUNK), :], qt_scr[which])

        def finish(s, which=which, kb=kb, k0=k0, last_of_block=last_of_block):
            bias = None
            if last_of_block:
                t0 = block_ids[which] * Q_BLOCK
                bias = jnp.where(key_minus_query <= t0 - k0, 0.0, NEG_BIG)
            vt_tile = jnp.concatenate([vst[kb * chunk_tiles + c] for c in range(chunk_tiles)], axis=1)
            return _tile_softmax(s, bias, vt_tile, shifted)

        sel_tiles.append((scores, finish))

    all_tiles += sel_tiles
    while len(results) < len(all_tiles):
        pump()
    for slot in range(n_slots):
        m, l, o = results[2 * n_win + slot]
        if not shifted:
            slot_m[slot] = m
        slot_l[slot] = l
        slot_o[slot] = o

    for idx in range(2):
        o_c, gate_t = per_block[idx]
        o_w = _merge_tiles([part + (None,) for part in results[idx * n_win:(idx + 1) * n_win]])
        slots = range(0, n0_max) if idx == 0 else range(1, n_slots)
        parts = []
        for slot in slots:
            if slot == 0 or slot >= n0_max:
                mine = None
            else:
                mine = (slot < n0) if idx == 0 else (slot >= n0)
            parts.append((None if shifted else slot_m[slot], slot_l[slot], slot_o[slot], mine))
        o_s = _merge_tiles(parts)
        row0 = pl.multiple_of(block_ids[idx] * Q_BLOCK, Q_BLOCK)
        for hh in range(NSA_HPG // 2):
            tiles = []
            for a in range(2):
                n = 2 * hh + a
                lanes = slice(n * Q_BLOCK, (n + 1) * Q_BLOCK)
                c0 = N_BRANCH * n
                tiles.append(gate_t[c0:c0 + 1, :] * o_c[:, lanes]
                             + gate_t[c0 + 1:c0 + 2, :] * o_s[:, lanes]
                             + gate_t[c0 + 2:c0 + 3, :] * o_w[:, lanes])
            o_ref[0, pl.ds(row0, Q_BLOCK), hh * LANES:(hh + 1) * LANES] = jnp.concatenate(tiles, axis=0).T


def _sel_map_t(seq):
    n_c = seq // CMP_STRIDE - 1
    n_s = seq // SEL_BLOCK
    tok = np.arange(seq)
    start = np.arange(n_c) * CMP_STRIDE
    cover_c = (tok[None, :] >= start[:, None]) & (tok[None, :] < start[:, None] + CMP_BLOCK)
    cover_s = (tok[:, None] // SEL_BLOCK) == np.arange(n_s)[None, :]
    m = cover_c.astype(np.float32) @ cover_s.astype(np.float32) / np.float32(CMP_BLOCK)
    out = np.zeros((n_s, n_c + 1), np.float32)
    out[:, :n_c] = m.T
    return jnp.asarray(out)


def _score_bounds(q_gain, k_gain):
    bound = 1.01 * HEAD_DIM ** 0.5 * LOG2_E * jnp.max(jnp.abs(q_gain)) * jnp.max(jnp.abs(k_gain), axis=-1)
    return jnp.broadcast_to(bound[:, None], (N_BRANCH, LANES)).astype(F32)


def _attention(q, gates, kvc, kv, qg_t, kg_ext, shift, shifted):
    b, s, _ = q.shape
    g = NSA_GROUPS
    nqb = s // Q_BLOCK
    ncmp = s // CMP_STRIDE
    nsel = s // SEL_BLOCK
    assert s % SEL_CHUNK == 0 and nsel % SUBLANES == 0 and nqb % 2 == 0
    gw = NSA_HPG * HEAD_DIM
    cols = NSA_HPG * Q_BLOCK
    n_slots = _sel_slot_count(nqb)

    def kv_spec(j):
        return pl.BlockSpec((1, 1, 1, s, LANES), lambda bi, gi, i: (j, bi, gi, 0, 0))

    return pl.pallas_call(
        functools.partial(_attn_kernel, shifted=shifted),
        grid=(b, g, nqb // 2),
        in_specs=[
            pl.BlockSpec((1, s, gw), lambda bi, gi, i: (bi, 0, gi)),
            pl.BlockSpec((1, 1, s, LANES), lambda bi, gi, i: (bi, gi, 0, 0)),
            pl.BlockSpec((1, 1, ncmp, LANES), lambda bi, gi, i: (bi, gi, 0, 0)),
            kv_spec(1), kv_spec(2),
            pl.BlockSpec((nsel, ncmp), lambda bi, gi, i: (0, 0)),
            pl.BlockSpec((HEAD_DIM, Q_BLOCK), lambda bi, gi, i: (0, 0)),
            pl.BlockSpec((N_BRANCH, LANES), lambda bi, gi, i: (0, 0)),
            pl.BlockSpec((N_BRANCH, LANES), lambda bi, gi, i: (0, 0)),
        ],
        out_specs=pl.BlockSpec((1, s, gw), lambda bi, gi, i: (bi, 0, gi)),
        out_shape=jax.ShapeDtypeStruct((b, s, Q_DIM), F32),
        scratch_shapes=[
            pltpu.VMEM((s, LANES), BF16),
            pltpu.VMEM((WINDOW + s, LANES), BF16),
            pltpu.VMEM((s // LANES, VT_ROWS, LANES), BF16),
            pltpu.VMEM(((WINDOW + s) // LANES, VT_ROWS, LANES), BF16),
            pltpu.VMEM((2, LANES, cols), BF16),
            pltpu.VMEM((n_slots, 1, cols), F32),
            pltpu.VMEM((n_slots, 1, cols), F32),
            pltpu.VMEM((n_slots, HEAD_DIM, cols), F32),
        ],
        compiler_params=_cparams(3),
        name="nsa_attention",
    )(q, gates, kvc, kv, kv, _sel_map_t(s), qg_t, kg_ext, shift)


def _ffn_core(x, g_ref, sc_ref, sh_ref, gate_ref, wg_ref, wu_ref, wo_ref):
    hh = _norm_mod(x, g_ref[...], sc_ref[0], sh_ref[0]).astype(BF16)
    a = jnp.dot(hh, wg_ref[...], preferred_element_type=F32)
    u = jnp.dot(hh, wu_ref[...], preferred_element_type=F32)
    act = (jax.nn.silu(a) * u).astype(BF16)
    return x + gate_ref[0] * jnp.dot(act, wo_ref[...], preferred_element_type=F32)


def _ffn_kernel(x_ref, g_ref, sc_ref, sh_ref, gate_ref, wg_ref, wu_ref, wo_ref, o_ref):
    o_ref[0] = _ffn_core(x_ref[0], g_ref, sc_ref, sh_ref, gate_ref, wg_ref, wu_ref, wo_ref)


def _mix_ffn_kernel(y_ref, wmix_ref, mix_gate_ref, x_ref, g_ref, sc_ref, sh_ref, gate_ref, wg_ref, wu_ref, wo_ref,
                    o_ref):
    mix = jnp.dot(y_ref[0].astype(BF16), wmix_ref[...], preferred_element_type=F32)
    x = x_ref[0] + mix_gate_ref[0] * mix
    o_ref[0] = _ffn_core(x, g_ref, sc_ref, sh_ref, gate_ref, wg_ref, wu_ref, wo_ref)


def _ffn(x, gain, sc, sh, gate, w_in, w_out, layer, mixer_out=None):
    b, s, d = x.shape
    hidden = w_out.shape[1]
    tm = min(ROW_TILE, s)
    vec = pl.BlockSpec((1, 1, d), lambda bi, i: (bi, 0, 0))
    rows = lambda width: pl.BlockSpec((1, tm, width), lambda bi, i: (bi, i, 0))
    in_specs = [
        rows(d), _resident((1, d)), vec, vec, vec,
        _layer_plane((d, hidden), layer, 0), _layer_plane((d, hidden), layer, 1),
        _layer_plane((hidden, d), layer),
    ]
    args = (x, gain, sc, sh, gate, w_in, w_in, w_out)
    body = _ffn_kernel
    if mixer_out is not None:
        y, w_mix, j, mix_gate = mixer_out
        in_specs = [rows(y.shape[-1]), _layer_plane(w_mix.shape[1:], j), vec] + in_specs
        args = (y, w_mix, mix_gate) + args
        body = _mix_ffn_kernel
    return pl.pallas_call(
        body,
        grid=(b, s // tm),
        in_specs=in_specs,
        out_specs=rows(d),
        out_shape=jax.ShapeDtypeStruct((b, s, d), F32),
        compiler_params=_cparams(2),
        name="ffn",
    )(*args)


def _band_starts(width, block_w):
    band = 2 * MXU_WIDTH
    starts = []
    for c0 in range(0, width, MXU_WIDTH):
        c1 = min(c0 + MXU_WIDTH, width)
        lo = (c0 // block_w) * block_w
        hi = ((c1 - 1) // block_w + 1) * block_w
        k0 = min((lo // LANES) * LANES, width - band)
        assert k0 >= 0 and k0 <= lo and hi <= k0 + band
        starts.append(k0)
    return starts


def _band_weights(w_a, w_x):
    nb, bw, _ = w_a.shape
    width = nb * bw
    spread = jnp.asarray(np.tile(np.eye(bw, dtype=np.float32), (1, nb)), dtype=BF16)
    blk = np.arange(width) // bw
    on_diag = jnp.asarray(blk[:, None] == blk[None, :])
    dense = lambda w: jnp.where(
        on_diag, jnp.dot(w.reshape(width, bw).astype(BF16), spread, preferred_element_type=BF16), 0)
    da, dx = dense(w_a), dense(w_x)
    band = 2 * MXU_WIDTH
    tiles = []
    for t, k0 in enumerate(_band_starts(width, bw)):
        c0 = t * MXU_WIDTH
        c1 = min(c0 + MXU_WIDTH, width)
        pad = ((0, 0), (0, MXU_WIDTH - (c1 - c0)))
        tiles.append(jnp.concatenate(
            [jnp.pad(da[k0:k0 + band, c0:c1], pad), jnp.pad(dx[k0:k0 + band, c0:c1], pad)], axis=1))
    return 0.5 * jnp.stack(tiles)


def _rg_mixer_kernel(x_ref, xn_ref, g_ref, sc_ref, sh_ref, res_gate_ref, win_ref, cw_ref, cb_ref, wb_ref, ba_ref,
                     bx_ref, lam_ref, wout_ref, o_ref,
                     u_a, u_b, gate_a, gate_b, h_scr, a_scr, b_scr, y_scr, *, band_starts):
    step = pl.program_id(0)
    nb, ts, d = x_ref.shape
    width = wout_ref.shape[0]
    rows = nb * ts
    halo = CONV_WIDTH - 1
    band = 2 * MXU_WIDTH

    pieces = [(c0, min(c0 + MXU_WIDTH, 2 * width)) for c0 in range(0, 2 * width, MXU_WIDTH)]

    def project_piece(hh, piece, u_dst, gate_dst):
        c0, c1 = pieces[piece]
        res = jnp.dot(hh, win_ref[:, c0:c1], preferred_element_type=F32)
        n_gate = max(0, min(c1, width) - c0)
        if n_gate:
            gate_dst[:, c0:c0 + n_gate] = res[:, :n_gate]
        if c1 > width:
            u0 = max(c0, width) - width
            u_dst[:, :, u0:c1 - width] = jnp.swapaxes(res[:, n_gate:].reshape(nb, ts, c1 - c0 - n_gate), 0, 1)

    def normed(tile_ref):
        return _norm_mod(tile_ref[...], g_ref[...], sc_ref[...], sh_ref[...]).reshape(rows, d).astype(BF16)

    @pl.when(step == 0)
    def _init():
        u_b[...] = jnp.zeros((ts, nb, width), F32)
        h_scr[...] = jnp.zeros(h_scr.shape, F32)
        hh = normed(x_ref)
        for piece in range(len(pieces)):
            project_piece(hh, piece, u_a, gate_a)

    def tile_step(u_cur, gate_cur, u_nxt, gate_nxt):
        prev_tail = u_nxt[ts - halo:ts]
        hh_next = normed(xn_ref)
        todo = list(range(len(pieces)))
        per_stage = -(-len(pieces) // len(band_starts))

        uc = jnp.zeros((ts, nb, width), F32) + cb_ref[...]
        for k in range(CONV_WIDTH):
            shifted = u_cur[...] if k == halo else jnp.concatenate([prev_tail[k:], u_cur[0:ts - halo + k]], axis=0)
            uc = uc + cw_ref[k:k + 1, :] * shifted
        uc = uc.reshape(rows, width)
        ub = uc.astype(BF16)

        z = -lam_ref[...]
        log_a_unit = -RG_C * (jnp.maximum(z, 0.0) + jnp.log1p(jnp.exp(-jnp.abs(z))))

        for t, k0 in enumerate(band_starts):
            c0 = t * MXU_WIDTH
            c1 = min(c0 + MXU_WIDTH, width)
            n = c1 - c0
            zz = jnp.dot(ub[:, k0:k0 + band], wb_ref[t], preferred_element_type=F32)
            for piece in todo[:per_stage]:
                project_piece(hh_next, piece, u_nxt, gate_nxt)
            todo = todo[per_stage:]
            half_unit = 0.5 * log_a_unit[:, c0:c1]
            half_u = 0.5 * uc[:, c0:c1]
            log_a = half_unit * jnp.tanh(zz[:, :n] + ba_ref[:, c0:c1]) + half_unit
            gated_u = half_u * jnp.tanh(zz[:, MXU_WIDTH:MXU_WIDTH + n] + bx_ref[:, c0:c1]) + half_u
            a = jnp.exp(log_a)
            bb = jnp.sqrt(-jnp.tanh(log_a) * (a * a + 1.0)) * gated_u
            for ct in range(n // LANES):
                lanes = slice(ct * LANES, (ct + 1) * LANES)
                a_scr[c0 // LANES + ct] = a[:, lanes]
                b_scr[c0 // LANES + ct] = bb[:, lanes]

        n_ct = width // LANES

        def scan_step(t, h):
            rows_t = pl.ds(pl.multiple_of(t * nb, nb), nb)
            new = []
            for ct in range(n_ct):
                hc = a_scr[ct, rows_t, :] * h[ct] + b_scr[ct, rows_t, :]
                b_scr[ct, rows_t, :] = hc
                new.append(hc)
            return tuple(new)

        h_fin = lax.fori_loop(0, ts, scan_step, tuple(h_scr[ct] for ct in range(n_ct)), unroll=SUBLANES)
        for ct in range(n_ct):
            lanes = slice(ct * LANES, (ct + 1) * LANES)
            h_scr[ct] = h_fin[ct]
            hs = jnp.swapaxes(b_scr[ct].reshape(ts, nb, LANES), 0, 1)
            y_scr[:, lanes] = (hs.reshape(rows, LANES) * jax.nn.gelu(gate_cur[:, lanes])).astype(BF16)
        mix = jnp.dot(y_scr[...], wout_ref[...], preferred_element_type=F32)
        o_ref[...] = x_ref[...] + res_gate_ref[...] * mix.reshape(nb, ts, d)

    @pl.when(step % 2 == 0)
    def _even():
        tile_step(u_a, gate_a, u_b, gate_b)

    @pl.when(step % 2 == 1)
    def _odd():
        tile_step(u_b, gate_b, u_a, gate_a)


def _rg_mixer(x, gain, sc, sh, res_gate, w_in, conv_w, conv_b, w_band, b_a, b_x, lam, w_out, j, block_w):
    b, s, d = x.shape
    width = w_out.shape[1]
    assert b == SUBLANES
    ts = min(RG_TIME_TILE, s)
    rows = b * ts
    band_starts = tuple(_band_starts(width, block_w))
    n_tiles = s // ts
    tile = pl.BlockSpec((b, ts, d), lambda t: (0, t, 0))
    next_tile = pl.BlockSpec((b, ts, d), lambda t: (0, jnp.minimum(t + 1, n_tiles - 1), 0))
    mod = _resident((b, 1, d))
    vec = _resident((1, width))
    kern = functools.partial(_rg_mixer_kernel, band_starts=band_starts)
    return pl.pallas_call(
        kern,
        grid=(n_tiles,),
        in_specs=[
            tile, next_tile, _resident((1, d)), mod, mod, mod,
            _layer_plane(w_in.shape[1:], j),
            _resident((CONV_WIDTH, width)), vec,
            _resident(w_band.shape),
            vec, vec, vec,
            _layer_plane(w_out.shape[1:], j),
        ],
        out_specs=tile,
        out_shape=jax.ShapeDtypeStruct((b, s, d), F32),
        scratch_shapes=[
            pltpu.VMEM((ts, b, width), F32),
            pltpu.VMEM((ts, b, width), F32),
            pltpu.VMEM((rows, width), F32),
            pltpu.VMEM((rows, width), F32),
            pltpu.VMEM((width // LANES, b, LANES), F32),
            pltpu.VMEM((width // LANES, rows, LANES), F32),
            pltpu.VMEM((width // LANES, rows, LANES), F32),
            pltpu.VMEM((rows, width), BF16),
        ],
        compiler_params=_cparams(1),
        name="rg_mixer",
    )(x, x, gain, sc, sh, res_gate, w_in, conv_w, conv_b, w_band, b_a, b_x, lam, w_out)


def kernel(x, c, ada_w, ada_b, norm1_g, norm2_g, nsa_w_in, nsa_w_out, nsa_cmp_pos, nsa_cmp_w1, nsa_cmp_w2, nsa_q_gain, nsa_k_gain, rg_w_in, rg_conv_w, rg_conv_b, rg_w_a, rg_b_a, rg_w_x, rg_b_x, rg_lam, rg_w_out, ffn_w_in, ffn_w_out):
    depth, d, _ = ada_w.shape
    b = x.shape[0]
    mod = _ada_mod(c, ada_w, ada_b).reshape(depth, b, 6, 1, d)
    ffn_w_in, ffn_w_out, nsa_w_out, rg_w_in, rg_w_out = (
        w.astype(BF16) for w in (ffn_w_in, ffn_w_out, nsa_w_out, rg_w_in, rg_w_out))
    for layer in range(depth):
        sh1, sc1, g1, sh2, sc2, g2 = (mod[layer, :, k] for k in range(6))
        n1 = norm1_g[layer].reshape(1, d)
        n2 = norm2_g[layer].reshape(1, d)
        j = layer // N_MIXERS
        if layer % N_MIXERS == 0:
            kg_ext = jnp.concatenate([nsa_k_gain[j], jnp.zeros_like(nsa_k_gain[j])], axis=1)
            qg_t = jnp.broadcast_to(nsa_q_gain[j][:, None], (HEAD_DIM, Q_BLOCK))
            q, kv, gates = _nsa_proj(x, n1, sc1, sh1, _nsa_weight_layout(nsa_w_in[j]))
            kvc = _compress(kv, *_compress_weights(nsa_cmp_pos[j], nsa_cmp_w1[j], nsa_cmp_w2[j]), kg_ext)
            bounds = _score_bounds(nsa_q_gain[j], nsa_k_gain[j])
            o = lax.cond(
                2.0 * jnp.max(bounds) <= MAX_SHIFTED_SCORE_RANGE,
                lambda *a: _attention(*a, bounds, True),
                lambda *a: _attention(*a, jnp.zeros_like(bounds), False),
                q, gates, kvc, kv, qg_t, kg_ext)
            mixer_out = (o, nsa_w_out, j, g1)
        else:
            width = rg_w_out.shape[1]
            block_w = rg_w_a.shape[-1]
            x = _rg_mixer(x, n1, sc1, sh1, g1, rg_w_in,
                          rg_conv_w[j], rg_conv_b[j].reshape(1, width),
                          _band_weights(rg_w_a[j], rg_w_x[j]),
                          0.5 * rg_b_a[j].reshape(1, width), 0.5 * rg_b_x[j].reshape(1, width),
                          rg_lam[j].reshape(1, width), rg_w_out, j, block_w)
            mixer_out = None
        x = _ffn(x, n2, sc2, sh2, g2, ffn_w_in, ffn_w_out, layer, mixer_out)
    return x
```
